```python
import jax, jax.numpy as jnp
from jax import lax
import numpy as np

D_MODEL = 1024
BATCH = 4
SEQ = 4096
DEPTH = 4

GRID_W = 64
CTX_LEN = 256
MLA_HEADS = 4
MLA_Q_RANK = 256
MLA_KV_RANK = 128
MLA_NOPE = 128
MLA_ROPE = 64
MLA_V = 128
MLA_SCALE = (MLA_NOPE + MLA_ROPE) ** -0.5
Q_BLOCK = 128
ROPE_BASE = 10000.0
ML_HEADS = 4
ML_DH = 64
ML_W = ML_HEADS * ML_DH
ML_CHUNK = 128
LRU_W = 256
LRU_BLOCKS = 4
LRU_BD = LRU_W // LRU_BLOCKS
CONV_W = 4
CONV_LEFT = 2
LRU_C = 8.0
D_FF = 4 * D_MODEL
EPS = 1e-6
MLA_IN = MLA_Q_RANK + MLA_KV_RANK + MLA_ROPE
ML_IN = 4 * ML_W + 4 * ML_HEADS
LRU_IN = 2 * LRU_W
D_IN = MLA_IN + ML_IN + LRU_IN
MIX_W = MLA_HEADS * MLA_V + ML_W + LRU_W

kernel_name = "hymba_style_mla_mlstm_rglru_diffusion_trunk"


def rms_norm(x, gain=None):
    xf = x.astype(jnp.float32)
    y = xf * lax.rsqrt(jnp.mean(xf * xf, axis=-1, keepdims=True) + EPS)
    if gain is not None:
        y = y * gain.astype(jnp.float32)
    return y.astype(x.dtype)


def modulate(x, shift, scale):
    return x * (1.0 + scale) + shift


def axial_rope_tables(T):
    rows_n = T // GRID_W
    row = jnp.repeat(jnp.arange(rows_n), GRID_W).astype(jnp.float32)
    col = jnp.tile(jnp.arange(GRID_W), rows_n).astype(jnp.float32)
    half = MLA_ROPE // 2
    freqs = 1.0 / (ROPE_BASE ** (jnp.arange(0, half, 2, dtype=jnp.float32) / half))
    ang = jnp.concatenate([row[:, None] * freqs, col[:, None] * freqs], axis=-1)
    return jnp.cos(ang), jnp.sin(ang)


def apply_rope(x, cos, sin):
    x1, x2 = jnp.split(x.astype(jnp.float32), 2, axis=-1)
    return jnp.concatenate([x1 * cos - x2 * sin, x1 * sin + x2 * cos], axis=-1).astype(x.dtype)


def merge_heads(y):
    B, H, T, d = y.shape
    return y.transpose(0, 2, 1, 3).reshape(B, T, H * d)


def mla_qkv(a, g_q, w_uq, g_kv, w_ukv, rope):
    B, T, _ = a.shape
    c_q, c_kv, k_rope = jnp.split(a, [MLA_Q_RANK, MLA_Q_RANK + MLA_KV_RANK], axis=-1)
    q = (rms_norm(c_q, g_q) @ w_uq).reshape(B, T, MLA_HEADS, MLA_NOPE + MLA_ROPE).transpose(0, 2, 1, 3)
    kv = (rms_norm(c_kv, g_kv) @ w_ukv).reshape(B, T, MLA_HEADS, MLA_NOPE + MLA_V).transpose(0, 2, 1, 3)
    q_nope, q_rope = jnp.split(q, [MLA_NOPE], axis=-1)
    k_nope, v = jnp.split(kv, [MLA_NOPE], axis=-1)
    if rope is not None:
        q_rope = apply_rope(q_rope, *rope)
        k_rope = apply_rope(k_rope, *rope)
    k_rope = jnp.broadcast_to(k_rope[:, None], (B, MLA_HEADS, T, MLA_ROPE))
    q = jnp.concatenate([q_nope, q_rope], axis=-1)
    k = jnp.concatenate([k_nope, k_rope], axis=-1)
    return q, k, v


def attend(q, k, v):
    s = jnp.einsum("bhqd,bhkd->bhqk", q, k).astype(jnp.float32) * MLA_SCALE
    p = jax.nn.softmax(s, axis=-1).astype(v.dtype)
    return jnp.einsum("bhqk,bhkd->bhqd", p, v)


def blocked_attention(q, k, v):
    B, H, T, dk = q.shape
    nb = T // Q_BLOCK
    qb = jnp.moveaxis(q.reshape(B, H, nb, Q_BLOCK, dk), 2, 0)
    ob = lax.map(lambda qi: attend(qi, k, v), qb)
    return jnp.moveaxis(ob, 0, 2).reshape(B, H, T, v.shape[-1])


def mla_mixer(al, ac, rope, g_q, w_uq, g_kv, w_ukv, with_ctx_out):
    ql, kl, vl = mla_qkv(al, g_q, w_uq, g_kv, w_ukv, rope)
    qc, kc, vc = mla_qkv(ac, g_q, w_uq, g_kv, w_ukv, None)
    k_all = jnp.concatenate([kc, kl], axis=2)
    v_all = jnp.concatenate([vc, vl], axis=2)
    y_l = merge_heads(blocked_attention(ql, k_all, v_all))
    y_c = merge_heads(attend(qc, kc, vc)) if with_ctx_out else None
    return y_l, y_c


def mlstm_zero_state(B):
    return (jnp.zeros((B, ML_HEADS, ML_DH, ML_DH), jnp.float32),
            jnp.zeros((B, ML_HEADS, ML_DH), jnp.float32),
            jnp.zeros((B, ML_HEADS), jnp.float32))


def mlstm_scan(q, k, v, i_pre, f_pre, state):
    B, H, T, dh = q.shape
    nc = T // ML_CHUNK

    def chunks(a):
        return jnp.moveaxis(a.reshape(a.shape[:2] + (nc, ML_CHUNK) + a.shape[3:]), 2, 0)

    xs = (chunks(q), chunks(k), chunks(v), chunks(i_pre), chunks(jax.nn.log_sigmoid(f_pre)))
    lower = jnp.tril(jnp.ones((ML_CHUNK, ML_CHUNK), dtype=bool))

    def step(carry, chunk):
        C, n, m = carry
        qc, kc, vc, ic, lfc = chunk
        b = jnp.cumsum(lfc, axis=-1)
        d = jnp.where(lower, b[..., :, None] - b[..., None, :] + ic[..., None, :], -jnp.inf)
        inter = b + m[..., None]
        m_row = jnp.maximum(inter, jnp.max(d, axis=-1))
        w_intra = jnp.exp(d - m_row[..., None])
        w_inter = jnp.exp(inter - m_row)
        s = jnp.einsum("bhtd,bhsd->bhts", qc, kc) * w_intra
        num = jnp.einsum("bhts,bhsd->bhtd", s, vc) + w_inter[..., None] * jnp.einsum("bhtk,bhkv->bhtv", qc, C)
        den = jnp.sum(s, axis=-1) + w_inter * jnp.einsum("bhtk,bhk->bht", qc, n)
        h = num / jnp.maximum(jnp.abs(den), jnp.exp(-m_row))[..., None]
        b_last = b[..., -1]
        g = b_last[..., None] - b + ic
        m_new = jnp.maximum(b_last + m, jnp.max(g, axis=-1))
        w_old = jnp.exp(b_last + m - m_new)
        w_s = jnp.exp(g - m_new[..., None])
        C_new = w_old[..., None, None] * C + jnp.einsum("bhs,bhsk,bhsv->bhkv", w_s, kc, vc)
        n_new = w_old[..., None] * n + jnp.einsum("bhs,bhsk->bhk", w_s, kc)
        return (C_new, n_new, m_new), h

    state, hs = lax.scan(step, state, xs)
    h = jnp.moveaxis(hs, 0, 2).reshape(B, H, T, dh)
    return h, state


def mlstm_bidir(q, k, v, gf, gb, s_f, s_b):
    flip = lambda t: jnp.flip(t, axis=2)
    h_f, s_f = mlstm_scan(q, k, v, gf[0], gf[1], s_f)
    h_b, s_b = mlstm_scan(flip(q), flip(k), flip(v), flip(gb[0]), flip(gb[1]), s_b)
    return h_f + flip(h_b), s_f, s_b


def mlstm_mixer(zl, zc, gate_bias, with_ctx_out):
    def prep(z):
        Bz, T, _ = z.shape
        q, k, v, o, g = jnp.split(z, [ML_W, 2 * ML_W, 3 * ML_W, 4 * ML_W], axis=-1)
        heads = lambda t: t.reshape(Bz, T, ML_HEADS, ML_DH).transpose(0, 2, 1, 3).astype(jnp.float32)
        g = (g + gate_bias).astype(jnp.float32).transpose(0, 2, 1)
        i_f, f_f, i_b, f_b = jnp.split(g, 4, axis=1)
        return (heads(q) * ML_DH ** -0.5, heads(k), heads(v)), o, (i_f, f_f), (i_b, f_b)

    def finish(h, o):
        Bz, _, T, _ = h.shape
        h = rms_norm(h).transpose(0, 2, 1, 3).reshape(Bz, T, ML_W)
        return (jax.nn.sigmoid(o.astype(jnp.float32)) * h).astype(o.dtype)

    qkv_c, o_c, gf_c, gb_c = prep(zc)
    qkv_l, o_l, gf_l, gb_l = prep(zl)
    zero = mlstm_zero_state(zl.shape[0])
    h_c, s_f, s_b = mlstm_bidir(*qkv_c, gf_c, gb_c, zero, zero)
    h_l, _, _ = mlstm_bidir(*qkv_l, gf_l, gb_l, s_f, s_b)
    return finish(h_l, o_l), (finish(h_c, o_c) if with_ctx_out else None)


def dwconv_centred(x, w, b):
    T = x.shape[1]
    xp = jnp.pad(x, ((0, 0), (CONV_LEFT, CONV_W - 1 - CONV_LEFT), (0, 0)))
    return b + sum(xp[:, j:j + T] * w[j] for j in range(CONV_W))


def block_diag(x, w, b):
    B, T, _ = x.shape
    y = jnp.einsum("btgi,gio->btgo", x.reshape(B, T, LRU_BLOCKS, LRU_BD), w)
    return y.reshape(B, T, LRU_W) + b


def linear_scan(a, b, h0):
    b = b.at[:, 0].add(a[:, 0] * h0)
    comb = lambda l, r: (l[0] * r[0], r[0] * l[1] + r[1])
    _, h = lax.associative_scan(comb, (a, b), axis=1)
    return h


def rglru_mixer(rl, rc, conv_w, conv_b, w_a, b_a, w_x, b_x, lam, with_ctx_out):
    def branch(z):
        xb, gb = jnp.split(z, 2, axis=-1)
        return dwconv_centred(xb, conv_w, conv_b).astype(jnp.float32), jax.nn.gelu(gb)

    xs_c, gate_c = branch(rc)
    xs_l, gate_l = branch(rl)

    def gates(xs, d):
        r = jax.nn.sigmoid(block_diag(xs, w_a[d], b_a[d]))
        i = jax.nn.sigmoid(block_diag(xs, w_x[d], b_x[d]))
        log_a = -LRU_C * r * jax.nn.softplus(-lam[d].astype(jnp.float32))
        return jnp.exp(log_a), jnp.sqrt(-jnp.expm1(2.0 * log_a)) * (i * xs)

    def direction(d, rev):
        flip = (lambda t: jnp.flip(t, axis=1)) if rev else (lambda t: t)
        a_c, u_c = gates(flip(xs_c), d)
        h_c = linear_scan(a_c, u_c, jnp.zeros_like(u_c[:, 0]))
        a_l, u_l = gates(flip(xs_l), d)
        h_l = linear_scan(a_l, u_l, h_c[:, -1])
        return flip(h_l), flip(h_c)

    hf_l, hf_c = direction(0, False)
    hb_l, hb_c = direction(1, True)
    y_l = (gate_l * (hf_l + hb_l)).astype(rl.dtype)
    y_c = (gate_c * (hf_c + hb_c)).astype(rc.dtype) if with_ctx_out else None
    return y_l, y_c


def token_mixing(ul, uc, rope, w_in, w_out, g_q, w_uq, g_kv, w_ukv, ml_gate_bias,
                 conv_w, conv_b, w_a, b_a, w_x, b_x, lam, with_ctx_out):
    split_at = [MLA_IN, MLA_IN + ML_IN]
    al, ml_, rl = jnp.split(ul @ w_in, split_at, axis=-1)
    ac, mc, rc = jnp.split(uc @ w_in, split_at, axis=-1)
    ya_l, ya_c = mla_mixer(al, ac, rope, g_q, w_uq, g_kv, w_ukv, with_ctx_out)
    yb_l, yb_c = mlstm_mixer(ml_, mc, ml_gate_bias, with_ctx_out)
    yc_l, yc_c = rglru_mixer(rl, rc, conv_w, conv_b, w_a, b_a, w_x, b_x, lam, with_ctx_out)
    y_l = jnp.concatenate([ya_l, yb_l, yc_l], axis=-1) @ w_out
    y_c = (jnp.concatenate([ya_c, yb_c, yc_c], axis=-1) @ w_out) if with_ctx_out else None
    return y_l, y_c


def squared_relu_mlp(u, w1, w2):
    return jnp.square(jax.nn.relu(u @ w1)) @ w2


def setup_inputs(seed: int = 0) -> dict:
    key = jax.random.key(seed)
    ks = jax.random.split(key, 24)
    L = DEPTH
    nrm = lambda k, shape, s: jax.random.normal(k, shape, jnp.float32) * s
    gk = jax.random.split(ks[11], 4)
    f_bias = jnp.linspace(3.0, 6.0, ML_HEADS, dtype=jnp.float32)
    ml_gate_bias = jnp.concatenate([
        nrm(gk[0], (L, ML_HEADS), 0.1),
        f_bias + nrm(gk[1], (L, ML_HEADS), 0.1),
        nrm(gk[2], (L, ML_HEADS), 0.1),
        f_bias + nrm(gk[3], (L, ML_HEADS), 0.1)], axis=-1)
    a0 = jax.random.uniform(ks[18], (L, 2, LRU_W), jnp.float32, minval=0.9, maxval=0.999)
    return {
        "x": nrm(ks[0], (BATCH, SEQ, D_MODEL), 1.0),
        "c": nrm(ks[1], (BATCH, D_MODEL), 1.0),
        "ctx": nrm(ks[2], (BATCH, CTX_LEN, D_MODEL), 1.0),
        "c_ctx": nrm(ks[3], (D_MODEL,), 1.0),
        "w_mod": nrm(ks[4], (L, D_MODEL, 6 * D_MODEL), 0.5 * D_MODEL ** -0.5),
        "b_mod": nrm(ks[5], (L, 6 * D_MODEL), 0.02),
        "w_in": nrm(ks[6], (L, D_MODEL, D_IN), D_MODEL ** -0.5),
        "mla_g_q": 1.0 + nrm(ks[7], (L, MLA_Q_RANK), 0.1),
        "mla_w_uq": nrm(ks[8], (L, MLA_Q_RANK, MLA_HEADS * (MLA_NOPE + MLA_ROPE)), MLA_Q_RANK ** -0.5),
        "mla_g_kv": 1.0 + nrm(ks[9], (L, MLA_KV_RANK), 0.1),
        "mla_w_ukv": nrm(ks[10], (L, MLA_KV_RANK, MLA_HEADS * (MLA_NOPE + MLA_V)), MLA_KV_RANK ** -0.5),
        "ml_gate_bias": ml_gate_bias,
        "lru_conv_w": nrm(ks[12], (L, CONV_W, LRU_W), CONV_W ** -0.5),
        "lru_conv_b": nrm(ks[13], (L, LRU_W), 0.02),
        "lru_w_a": nrm(ks[14], (L, 2, LRU_BLOCKS, LRU_BD, LRU_BD), LRU_BD ** -0.5),
        "lru_b_a": nrm(ks[15], (L, 2, LRU_W), 0.02),
        "lru_w_x": nrm(ks[16], (L, 2, LRU_BLOCKS, LRU_BD, LRU_BD), LRU_BD ** -0.5),
        "lru_b_x": nrm(ks[17], (L, 2, LRU_W), 0.02),
        "lru_lam": jnp.log(a0) - jnp.log1p(-a0),
        "w_out": nrm(ks[19], (L, MIX_W, D_MODEL), MIX_W ** -0.5),
        "w_ff1": nrm(ks[20], (L, D_MODEL, D_FF), D_MODEL ** -0.5),
        "w_ff2": nrm(ks[21], (L, D_FF, D_MODEL), D_FF ** -0.5),
        "final_g": 1.0 + nrm(ks[22], (D_MODEL,), 0.1),
    }


def reference(x, c, ctx, c_ctx, w_mod, b_mod, w_in, mla_g_q, mla_w_uq, mla_g_kv, mla_w_ukv,
              ml_gate_bias, lru_conv_w, lru_conv_b, lru_w_a, lru_b_a, lru_w_x, lru_b_x, lru_lam,
              w_out, w_ff1, w_ff2, final_g):
    rope = axial_rope_tables(x.shape[1])
    xl, xc = x, ctx
    for l in range(DEPTH):
        with_ctx_out = l < DEPTH - 1
        mod_l = (jax.nn.silu(c) @ w_mod[l] + b_mod[l])[:, None, :]
        mod_c = jax.nn.silu(c_ctx) @ w_mod[l] + b_mod[l]
        sh1l, sc1l, g1l, sh2l, sc2l, g2l = jnp.split(mod_l, 6, axis=-1)
        sh1c, sc1c, g1c, sh2c, sc2c, g2c = jnp.split(mod_c, 6, axis=-1)
        ul = modulate(rms_norm(xl), sh1l, sc1l)
        uc = modulate(rms_norm(xc), sh1c, sc1c)
        yl, yc = token_mixing(ul, uc, rope, w_in[l], w_out[l], mla_g_q[l], mla_w_uq[l], mla_g_kv[l],
                              mla_w_ukv[l], ml_gate_bias[l], lru_conv_w[l], lru_conv_b[l], lru_w_a[l],
                              lru_b_a[l], lru_w_x[l], lru_b_x[l], lru_lam[l], with_ctx_out)
        xl = xl + g1l * yl
        xl = xl + g2l * squared_relu_mlp(modulate(rms_norm(xl), sh2l, sc2l), w_ff1[l], w_ff2[l])
        if with_ctx_out:
            xc = xc + g1c * yc
            xc = xc + g2c * squared_relu_mlp(modulate(rms_norm(xc), sh2c, sc2c), w_ff1[l], w_ff2[l])
    return rms_norm(xl, final_g)
```

```python
import functools

import jax
import jax.numpy as jnp
from jax import lax
from jax.experimental import pallas as pl
from jax.experimental.pallas import tpu as pltpu

F32 = jnp.float32
BF16 = jnp.bfloat16

D_MODEL = 1024
DEPTH = 4
GRID_W = 64
N_CTX = 256
MLA_HEADS = 4
MLA_Q_RANK = 256
MLA_KV_RANK = 128
MLA_NOPE = 128
MLA_ROPE = 64
MLA_V = 128
MLA_QK = MLA_NOPE + MLA_ROPE
MLA_SCALE = MLA_QK ** -0.5
ROPE_BASE = 10000.0
ML_HEADS = 4
ML_DH = 64
ML_W = ML_HEADS * ML_DH
ML_CHUNK = 128
LRU_W = 256
LRU_BLOCKS = 4
LRU_BD = LRU_W // LRU_BLOCKS
CONV_W = 4
CONV_LEFT = 2
LRU_C = 8.0
D_FF = 4 * D_MODEL
EPS = 1e-6
MLA_IN = MLA_Q_RANK + MLA_KV_RANK + MLA_ROPE
ML_IN = 4 * ML_W + 4 * ML_HEADS
LRU_IN = 2 * LRU_W

LANES = 128
SUBLANES = 8
MOD_ROWS = 8
CTX_MOD_ROW = 4

A_W = 512
MQ_W = 4 * ML_W
MG_W = LANES
R_W = 2 * LRU_W
IN_W = A_W + MQ_W + MG_W + R_W


def _cparams(sem, vmem_mb):
    return pltpu.CompilerParams(dimension_semantics=sem, vmem_limit_bytes=vmem_mb * 1024 * 1024)


def _mod_rows(m_ref, b, row0, tm, n_ctx, seg):
    lo, hi = seg * D_MODEL, (seg + 1) * D_MODEL
    lat = m_ref[0, pl.ds(b, 1), lo:hi]
    if n_ctx == 0:
        return lat
    ctx = m_ref[0, CTX_MOD_ROW:CTX_MOD_ROW + 1, lo:hi]
    rows = row0 + lax.broadcasted_iota(jnp.int32, (tm, 1), 0)
    return jnp.where(rows < n_ctx, ctx, lat)


def _pick_tile(n, candidates):
    return next(t for t in candidates if n % t == 0)


def _rms(x):
    return x * lax.rsqrt(jnp.mean(x * x, axis=-1, keepdims=True) + EPS)


def _dot(a, b):
    return jnp.dot(a, b, preferred_element_type=F32)


def _dot_nt(a, b):
    return lax.dot_general(a, b, (((1,), (1,)), ((), ())), preferred_element_type=F32)


def _mods_body(c_ref, w_ref, b_ref, o_ref):
    cv = c_ref[...]
    act = (cv * jax.nn.sigmoid(cv)).astype(BF16)
    o_ref[0] = _dot(act, w_ref[0].astype(BF16)) + b_ref[0]


def _mods_call(cvec, w_mod, b_mod):
    depth, d, n = w_mod.shape
    tn = 1536
    return pl.pallas_call(
        _mods_body,
        grid=(depth, n // tn),
        in_specs=[pl.BlockSpec((MOD_ROWS, d), lambda l, j: (0, 0)),
                  pl.BlockSpec((1, d, tn), lambda l, j: (l, 0, j)),
                  pl.BlockSpec((1, 1, tn), lambda l, j: (l, 0, j))],
        out_specs=pl.BlockSpec((1, MOD_ROWS, tn), lambda l, j: (l, 0, j)),
        out_shape=jax.ShapeDtypeStruct((depth, MOD_ROWS, n), F32),
        compiler_params=_cparams(("arbitrary", "arbitrary"), 40),
        name="mods",
    )(cvec, w_mod, b_mod.reshape(depth, 1, n))


def _in_body(x_ref, m_ref, w_ref, a_ref, q_ref, g_ref, r_ref, *, tm, n_ctx):
    b, i = pl.program_id(0), pl.program_id(1)
    xn = _rms(x_ref[0])
    shift = _mod_rows(m_ref, b, i * tm, tm, n_ctx, 0)
    scale = _mod_rows(m_ref, b, i * tm, tm, n_ctx, 1)
    u = (xn * (1.0 + scale) + shift).astype(BF16)
    a_ref[0] = _dot(u, w_ref[0, :, 0:A_W])
    q_ref[0] = _dot(u, w_ref[0, :, A_W:A_W + MQ_W])
    g_ref[0] = _dot(u, w_ref[0, :, A_W + MQ_W:A_W + MQ_W + MG_W])
    r_ref[0] = _dot(u, w_ref[0, :, A_W + MQ_W + MG_W:IN_W])


def _in_call(x, mods, w_in_p, layer, n_ctx):
    bsz, s, d = x.shape
    tm = _pick_tile(s, (544, 256))
    row = lambda b, i: (b, i, 0)
    return pl.pallas_call(
        functools.partial(_in_body, tm=tm, n_ctx=n_ctx),
        grid=(bsz, s // tm),
        in_specs=[pl.BlockSpec((1, tm, d), row),
                  pl.BlockSpec((1, MOD_ROWS, 6 * d), lambda b, i: (layer, 0, 0)),
                  pl.BlockSpec((1, d, IN_W), lambda b, i: (layer, 0, 0))],
        out_specs=[pl.BlockSpec((1, tm, A_W), row), pl.BlockSpec((1, tm, MQ_W), row),
                   pl.BlockSpec((1, tm, MG_W), row), pl.BlockSpec((1, tm, R_W), row)],
        out_shape=[jax.ShapeDtypeStruct((bsz, s, w), F32) for w in (A_W, MQ_W, MG_W, R_W)],
        compiler_params=_cparams(("parallel", "parallel"), 40),
        name="in_proj",
    )(x, mods, w_in_p)


def _prep_body(a_ref, gq_ref, gkv_ref, wuq_ref, wukv_ref, cos_ref, sin_ref, q_ref, k_ref, v_ref):
    a = a_ref[0]
    nq, nkv = MLA_Q_RANK, MLA_Q_RANK + MLA_KV_RANK
    cq = (_rms(a[:, 0:nq]) * gq_ref[0]).astype(BF16)
    ckv = (_rms(a[:, nq:nkv]) * gkv_ref[0]).astype(BF16)
    q = _dot(cq, wuq_ref[0])
    kv = _dot(ckv, wukv_ref[0])
    cos, sin = cos_ref[...], sin_ref[...]
    half = MLA_ROPE // 2
    hn = MLA_HEADS * MLA_NOPE
    x1, x2 = q[:, hn:hn + LANES], q[:, hn + LANES:hn + 2 * LANES]
    r1 = (x1 * cos - x2 * sin).astype(BF16)
    r2 = (x1 * sin + x2 * cos).astype(BF16)
    k1, k2 = a[:, nkv:nkv + half], a[:, nkv + half:nkv + 2 * half]
    c32, s32 = cos[:, 0:half], sin[:, 0:half]
    kr1 = (k1 * c32 - k2 * s32).astype(BF16)
    kr2 = (k1 * s32 + k2 * c32).astype(BF16)
    for h in range(MLA_HEADS):
        q_ref[0, h, :, 0:MLA_NOPE] = q[:, h * MLA_NOPE:(h + 1) * MLA_NOPE].astype(BF16)
        q_ref[0, h, :, MLA_NOPE:MLA_NOPE + half] = r1[:, h * half:(h + 1) * half]
        q_ref[0, h, :, MLA_NOPE + half:MLA_QK] = r2[:, h * half:(h + 1) * half]
        k_ref[0, h, :, 0:MLA_NOPE] = kv[:, h * MLA_NOPE:(h + 1) * MLA_NOPE].astype(BF16)
        k_ref[0, h, :, MLA_NOPE:MLA_NOPE + half] = kr1
        k_ref[0, h, :, MLA_NOPE + half:MLA_QK] = kr2
    v_ref[0] = kv[:, hn:hn + MLA_HEADS * MLA_V].astype(BF16)


def _prep_call(a, g_q, w_uq_p, g_kv, w_ukv_p, cos4, sin4, layer):
    bsz, s, _ = a.shape
    tm = _pick_tile(s, (544, 256))
    lsel = lambda b, i: (layer, 0, 0)
    hm = jax.ShapeDtypeStruct((bsz, MLA_HEADS, s, MLA_QK), BF16)
    return pl.pallas_call(
        _prep_body,
        grid=(bsz, s // tm),
        in_specs=[pl.BlockSpec((1, tm, A_W), lambda b, i: (b, i, 0)),
                  pl.BlockSpec((1, 1, MLA_Q_RANK), lsel),
                  pl.BlockSpec((1, 1, MLA_KV_RANK), lsel),
                  pl.BlockSpec((1, MLA_Q_RANK, MLA_HEADS * MLA_QK), lsel),
                  pl.BlockSpec((1, MLA_KV_RANK, MLA_HEADS * (MLA_NOPE + MLA_V)), lsel),
                  pl.BlockSpec((tm, LANES), lambda b, i: (i, 0)),
                  pl.BlockSpec((tm, LANES), lambda b, i: (i, 0))],
        out_specs=[pl.BlockSpec((1, MLA_HEADS, tm, MLA_QK), lambda b, i: (b, 0, i, 0)),
                   pl.BlockSpec((1, MLA_HEADS, tm, MLA_QK), lambda b, i: (b, 0, i, 0)),
                   pl.BlockSpec((1, tm, MLA_HEADS * MLA_V), lambda b, i: (b, i, 0))],
        out_shape=[hm, hm, jax.ShapeDtypeStruct((bsz, s, MLA_HEADS * MLA_V), BF16)],
        compiler_params=_cparams(("parallel", "parallel"), 40),
        name="mla_prep",
    )(a, g_q, g_kv, w_uq_p, w_ukv_p, cos4, sin4)


def _attn_body(q_ref, k_ref, v_ref, o_ref, *, n_ctx, s_len, tk):
    q = q_ref[0, 0]
    tq = q.shape[0]

    def run(chunks):
        m = jnp.full((tq, 1), -jnp.inf, F32)
        l = jnp.zeros((tq, 1), F32)
        acc = jnp.zeros((tq, MLA_V), F32)
        for start, size in chunks:
            sc = _dot_nt(q, k_ref[0, 0, start:start + size, :]) * MLA_SCALE
            m_new = jnp.maximum(m, jnp.max(sc, axis=-1, keepdims=True))
            p = jnp.exp(sc - m_new)
            alpha = jnp.exp(m - m_new)
            l = alpha * l + jnp.sum(p, axis=-1, keepdims=True)
            acc = alpha * acc + _dot(p.astype(BF16), v_ref[0, start:start + size, :])
            m = m_new
        o_ref[0] = (acc / l).astype(o_ref.dtype)

    ctx_chunks = [(0, n_ctx)]
    all_chunks = ctx_chunks + [(n_ctx + j * tk, tk) for j in range((s_len - n_ctx) // tk)]
    is_ctx = pl.program_id(2) * tq < n_ctx

    @pl.when(is_ctx)
    def _():
        run(ctx_chunks)

    @pl.when(jnp.logical_not(is_ctx))
    def _():
        run(all_chunks)


def _attn_call(q, k, v, n_ctx):
    bsz, nh, s, dk = q.shape
    tq, tk = 256, 512
    return pl.pallas_call(
        functools.partial(_attn_body, n_ctx=n_ctx, s_len=s, tk=tk),
        grid=(bsz, nh, s // tq),
        in_specs=[pl.BlockSpec((1, 1, tq, dk), lambda b, h, i: (b, h, i, 0)),
                  pl.BlockSpec((1, 1, s, dk), lambda b, h, i: (b, h, 0, 0)),
                  pl.BlockSpec((1, s, MLA_V), lambda b, h, i: (b, 0, h))],
        out_specs=pl.BlockSpec((1, tq, MLA_V), lambda b, h, i: (b, i, h)),
        out_shape=jax.ShapeDtypeStruct((bsz, s, nh * MLA_V), BF16),
        compiler_params=_cparams(("parallel", "parallel", "arbitrary"), 40),
        name="mla_attn",
    )(q, k, v)


def _mlstm_body(xf_ref, gf_ref, xb_ref, gb_ref, bias_ref, hf_ref, hb_ref, c_ref, m_ref):
    lc = ML_CHUNK

    @pl.when(pl.program_id(1) == 0)
    def _():
        c_ref[...] = jnp.zeros_like(c_ref)
        m_ref[...] = jnp.zeros_like(m_ref)

    r_io = lax.broadcasted_iota(jnp.int32, (lc, lc), 0)
    c_io = lax.broadcasted_iota(jnp.int32, (lc, lc), 1)
    lane = lax.broadcasted_iota(jnp.int32, (lc, LANES), 1)
    sub = lax.broadcasted_iota(jnp.int32, (LANES, lc), 0)
    bias = bias_ref[...]

    for d, (x_ref, g_ref, o_ref) in enumerate(((xf_ref, gf_ref, hf_ref), (xb_ref, gb_ref, hb_ref))):
        mask = (r_io >= c_io) if d == 0 else (r_io <= c_io)
        last = lc - 1 if d == 0 else 0
        g = g_ref[0] + bias
        lf = jax.nn.log_sigmoid(g)
        bc = jnp.dot(mask.astype(F32), lf, precision=lax.Precision.HIGHEST,
                     preferred_element_type=F32)
        br = bc.T
        gr = g.T
        x = x_ref[0]
        for pair in range(ML_HEADS // 2):
            qs = x[:, pair * LANES:(pair + 1) * LANES] * (ML_DH ** -0.5)
            ks = x[:, ML_W + pair * LANES:ML_W + (pair + 1) * LANES]
            vs = x[:, 2 * ML_W + pair * LANES:2 * ML_W + (pair + 1) * LANES]
            kst = ks.T
            outs = []
            for odd in range(2):
                h = 2 * pair + odd
                ci, cf = d * 2 * ML_HEADS + h, d * 2 * ML_HEADS + ML_HEADS + h
                own = (lane >= ML_DH) if odd else (lane < ML_DH)
                own_t = (sub >= ML_DH) if odd else (sub < ML_DH)
                den_lane = 0 if odd else ML_DH
                bcol, icol = bc[:, cf:cf + 1], g[:, ci:ci + 1]
                brow, irow = br[cf:cf + 1, :], gr[ci:ci + 1, :]
                st = d * ML_HEADS + h
                m_prev = m_ref[st:st + 1, 0:1]
                dmat = jnp.where(mask, bcol - brow + irow, -jnp.inf)
                inter = bcol + m_prev
                m_row = jnp.maximum(inter, jnp.max(dmat, axis=-1, keepdims=True))
                w_intra = jnp.exp(dmat - m_row)
                w_inter = jnp.exp(inter - m_row)
                qm = jnp.where(own, qs, 0.0).astype(BF16)
                sc = (_dot_nt(qm, ks.astype(BF16)) * w_intra).astype(BF16)
                vaug = jnp.where(own, vs, jnp.where(lane == den_lane, 1.0, 0.0))
                c_old = c_ref[st]
                haug = _dot(sc, vaug.astype(BF16)) + w_inter * _dot(qm, c_old.astype(BF16))
                den = haug[:, den_lane:den_lane + 1]
                outs.append(haug / jnp.maximum(jnp.abs(den), jnp.exp(-m_row)))
                b_last = bcol[last:last + 1, :]
                gvec = b_last - bcol + icol
                m_new = jnp.maximum(b_last + m_prev, jnp.max(gvec, axis=0, keepdims=True))
                w_old = jnp.exp(b_last + m_prev - m_new)
                w_s = jnp.exp(gvec - m_new)
                ktm = jnp.where(own_t, kst, 0.0).astype(BF16)
                c_ref[st] = w_old * c_old + _dot(ktm, (w_s * vaug).astype(BF16))
                m_ref[st:st + 1, :] = jnp.broadcast_to(m_new, (1, LANES))
            o_ref[0, :, pair * LANES:(pair + 1) * LANES] = jnp.where(lane < ML_DH, outs[0], outs[1])


def _mlstm_call(mq, mg, bias_p, n_ctx):
    bsz, s, _ = mq.shape
    nch, ncc = s // ML_CHUNK, n_ctx // ML_CHUNK
    fwd = lambda b, j: (b, j, 0)
    bwd = lambda b, j: (b, jnp.where(j < ncc, ncc - 1 - j, nch - 1 + ncc - j), 0)
    out = jax.ShapeDtypeStruct((bsz, s, ML_W), F32)
    return pl.pallas_call(
        _mlstm_body,
        grid=(bsz, nch),
        in_specs=[pl.BlockSpec((1, ML_CHUNK, MQ_W), fwd), pl.BlockSpec((1, ML_CHUNK, MG_W), fwd),
                  pl.BlockSpec((1, ML_CHUNK, MQ_W), bwd), pl.BlockSpec((1, ML_CHUNK, MG_W), bwd),
                  pl.BlockSpec((1, MG_W), lambda b, j: (0, 0))],
        out_specs=[pl.BlockSpec((1, ML_CHUNK, ML_W), fwd), pl.BlockSpec((1, ML_CHUNK, ML_W), bwd)],
        out_shape=[out, out],
        scratch_shapes=[pltpu.VMEM((2 * ML_HEADS, LANES, LANES), F32),
                        pltpu.VMEM((2 * ML_HEADS, LANES), F32)],
        compiler_params=_cparams(("parallel", "arbitrary"), 40),
        name="mlstm",
    )(mq, mg, mq, mg, bias_p)


LRU_PITCH_PAD = 8


def _lru_body(xb_ref, gb_ref, cw_ref, cb_ref, wa_ref, ba_ref, wx_ref, bx_ref, lam_ref, y_ref,
              xp_ref, xs_ref, a_ref, u_ref, hs_ref, *, n_ctx, s_len):
    n_lat = s_len - n_ctx
    pad = SUBLANES
    lat_off = n_ctx + 2 * pad
    zeros = jnp.zeros((pad, LANES), F32)
    xp_ref[0:pad, :] = zeros
    xp_ref[pad + n_ctx:lat_off, :] = zeros
    xp_ref[lat_off + n_lat:lat_off + n_lat + pad, :] = zeros
    xp_ref[pad:pad + n_ctx, :] = xb_ref[0, 0:n_ctx, :]
    cchunk = 512

    def copy_body(c, _):
        src = pl.multiple_of(n_ctx + c * cchunk, SUBLANES)
        dst = pl.multiple_of(lat_off + c * cchunk, SUBLANES)
        xp_ref[pl.ds(dst, cchunk), :] = xb_ref[0, pl.ds(src, cchunk), :]
        return 0

    lax.fori_loop(0, n_lat // cchunk, copy_body, 0)

    cw = cw_ref[...]
    cb = cb_ref[...]

    def conv(src0, dst0, n):
        acc = cb + xp_ref[src0 - CONV_LEFT:src0 - CONV_LEFT + n, :] * cw[0:1, :]
        for j in range(1, CONV_W):
            acc = acc + xp_ref[src0 - CONV_LEFT + j:src0 - CONV_LEFT + j + n, :] * cw[j:j + 1, :]
        xs_ref[dst0:dst0 + n, :] = acc

    conv(pad, 0, n_ctx)
    for c in range(n_lat // cchunk):
        conv(lat_off + c * cchunk, n_ctx + c * cchunk, cchunk)

    seg_lat = n_lat // SUBLANES
    seg_ctx = n_ctx // SUBLANES
    p_lat, p_ctx = seg_lat + LRU_PITCH_PAD, seg_ctx + LRU_PITCH_PAD
    ctx_base = SUBLANES * p_lat
    row_io = lax.broadcasted_iota(jnp.int32, (SUBLANES, LANES), 0)

    def gates(x, d):
        xb16 = x.astype(BF16)
        r = jax.nn.sigmoid(_dot(xb16, wa_ref[d]) + ba_ref[d])
        i = jax.nn.sigmoid(_dot(xb16, wx_ref[d]) + bx_ref[d])
        log_a = (-LRU_C) * r * jax.nn.softplus(-lam_ref[d])
        a = jnp.exp(log_a)
        return a, jnp.sqrt(jnp.tanh(-log_a) * (1.0 + a * a)) * (i * x)

    def fill(d):
        def lat_body(r, _):
            src = pl.multiple_of(n_ctx + r * seg_lat, SUBLANES)
            dst = pl.multiple_of(r * p_lat, SUBLANES)
            a, u = gates(xs_ref[pl.ds(src, seg_lat), :], d)
            a_ref[pl.ds(dst, seg_lat), :] = a
            u_ref[pl.ds(dst, seg_lat), :] = u
            return 0

        lax.fori_loop(0, SUBLANES, lat_body, 0)
        a, u = gates(xs_ref[0:n_ctx, :], d)
        for r in range(SUBLANES):
            a_ref[ctx_base + r * p_ctx:ctx_base + r * p_ctx + seg_ctx, :] = a[r * seg_ctx:(r + 1) * seg_ctx, :]
            u_ref[ctx_base + r * p_ctx:ctx_base + r * p_ctx + seg_ctx, :] = u[r * seg_ctx:(r + 1) * seg_ctx, :]

    def scan(base, n, pitch, reverse, h0, accumulate):
        def pass1(t, carry):
            h, acum = carry
            j = (n - 1 - t) if reverse else t
            idx = pl.ds(base + j, SUBLANES, stride=pitch)
            a = a_ref[idx, :]
            h = a * h + u_ref[idx, :]
            acum = acum * a
            u_ref[idx, :] = h
            a_ref[idx, :] = acum
            return h, acum

        h_end, a_end = lax.fori_loop(0, n, pass1, (jnp.zeros((SUBLANES, LANES), F32),
                                                    jnp.ones((SUBLANES, LANES), F32)))
        carry = h0
        cvec = jnp.zeros((SUBLANES, LANES), F32)
        order = range(SUBLANES - 1, -1, -1) if reverse else range(SUBLANES)
        for r in order:
            cvec = jnp.where(row_io == r, carry, cvec)
            carry = h_end[r:r + 1, :] + a_end[r:r + 1, :] * carry

        def pass2(t, _):
            idx = pl.ds(base + t, SUBLANES, stride=pitch)
            res = u_ref[idx, :] + a_ref[idx, :] * cvec
            if accumulate:
                res = res + hs_ref[idx, :]
            hs_ref[idx, :] = res
            return 0

        lax.fori_loop(0, n, pass2, 0)
        return carry

    for d in range(2):
        fill(d)
        h_ctx = scan(ctx_base, seg_ctx, p_ctx, d == 1, jnp.zeros((1, LANES), F32), d == 1)
        scan(0, seg_lat, p_lat, d == 1, h_ctx, d == 1)

    def out_body(r, _):
        dst = pl.multiple_of(n_ctx + r * seg_lat, SUBLANES)
        src = pl.multiple_of(r * p_lat, SUBLANES)
        y_ref[0, pl.ds(dst, seg_lat), :] = (jax.nn.gelu(gb_ref[0, pl.ds(dst, seg_lat), :])
                                            * hs_ref[pl.ds(src, seg_lat), :])
        return 0

    lax.fori_loop(0, SUBLANES, out_body, 0)
    for r in range(SUBLANES):
        y_ref[0, r * seg_ctx:(r + 1) * seg_ctx, :] = (
            jax.nn.gelu(gb_ref[0, r * seg_ctx:(r + 1) * seg_ctx, :])
            * hs_ref[ctx_base + r * p_ctx:ctx_base + r * p_ctx + seg_ctx, :])


def _lru_call(r, conv_w, conv_b, wa_bd, b_a, wx_bd, b_x, lam, layer, n_ctx):
    bsz, s, _ = r.shape
    nh = LRU_W // LANES
    n_lat = s - n_ctx
    scan_rows = SUBLANES * (n_lat // SUBLANES + LRU_PITCH_PAD) + SUBLANES * (n_ctx // SUBLANES + LRU_PITCH_PAD)
    vec = lambda b, c: (layer, 0, 0, c)
    return pl.pallas_call(
        functools.partial(_lru_body, n_ctx=n_ctx, s_len=s),
        grid=(bsz, nh),
        in_specs=[pl.BlockSpec((1, s, LANES), lambda b, c: (b, 0, c)),
                  pl.BlockSpec((1, s, LANES), lambda b, c: (b, 0, nh + c)),
                  pl.BlockSpec((None, CONV_W, LANES), lambda b, c: (layer, 0, c)),
                  pl.BlockSpec((None, 1, LANES), lambda b, c: (layer, 0, c)),
                  pl.BlockSpec((None, None, 2, LANES, LANES), lambda b, c: (layer, c, 0, 0, 0)),
                  pl.BlockSpec((None, 2, 1, LANES), vec),
                  pl.BlockSpec((None, None, 2, LANES, LANES), lambda b, c: (layer, c, 0, 0, 0)),
                  pl.BlockSpec((None, 2, 1, LANES), vec),
                  pl.BlockSpec((None, 2, 1, LANES), vec)],
        out_specs=pl.BlockSpec((1, s, LANES), lambda b, c: (b, 0, c)),
        out_shape=jax.ShapeDtypeStruct((bsz, s, LRU_W), F32),
        scratch_shapes=[pltpu.VMEM((s + 3 * SUBLANES, LANES), F32),
                        pltpu.VMEM((s, LANES), F32),
                        pltpu.VMEM((scan_rows, LANES), F32),
                        pltpu.VMEM((scan_rows, LANES), F32),
                        pltpu.VMEM((scan_rows, LANES), F32)],
        compiler_params=_cparams(("parallel", "parallel"), 48),
        name="rglru",
    )(r, r, conv_w, conv_b, wa_bd, b_a, wx_bd, b_x, lam)


def _out_body(x_ref, m_ref, ya_ref, hf_ref, hb_ref, og_ref, yc_ref, w_ref, hn_ref, o_ref, *, tm, n_ctx, blk0):
    b, i = pl.program_id(0), pl.program_id(1)
    hsum = hf_ref[0] + hb_ref[0]
    msq = jnp.dot(hsum * hsum, hn_ref[...], precision=lax.Precision.HIGHEST, preferred_element_type=F32)
    yb = (jax.nn.sigmoid(og_ref[0]) * (hsum * lax.rsqrt(msq + EPS))).astype(BF16)
    na, nb = MLA_HEADS * MLA_V, MLA_HEADS * MLA_V + ML_W
    y = (_dot(ya_ref[0], w_ref[0, 0:na, :]) + _dot(yb, w_ref[0, na:nb, :])
         + _dot(yc_ref[0].astype(BF16), w_ref[0, nb:, :]))
    gate = _mod_rows(m_ref, b, (i + blk0) * tm, tm, n_ctx, 2)
    o_ref[0] = x_ref[0] + gate * y


def _out_call(x, mods, ya, hf, hb, mq, yc, w_out, head_avg, layer, n_ctx, skip_ctx):
    bsz, s, d = x.shape
    tm = 256
    blk0 = n_ctx // tm if skip_ctx else 0
    rows_out = s - blk0 * tm
    row = lambda b, i: (b, i + blk0, 0)
    return pl.pallas_call(
        functools.partial(_out_body, tm=tm, n_ctx=n_ctx, blk0=blk0),
        grid=(bsz, rows_out // tm),
        in_specs=[pl.BlockSpec((1, tm, d), row),
                  pl.BlockSpec((1, MOD_ROWS, 6 * d), lambda b, i: (layer, 0, 0)),
                  pl.BlockSpec((1, tm, MLA_HEADS * MLA_V), row),
                  pl.BlockSpec((1, tm, ML_W), row),
                  pl.BlockSpec((1, tm, ML_W), row),
                  pl.BlockSpec((1, tm, ML_W), lambda b, i: (b, i + blk0, 3)),
                  pl.BlockSpec((1, tm, LRU_W), row),
                  pl.BlockSpec((1, d, d), lambda b, i: (layer, 0, 0)),
                  pl.BlockSpec((ML_W, ML_W), lambda b, i: (0, 0))],
        out_specs=pl.BlockSpec((1, tm, d), lambda b, i: (b, i, 0)),
        out_shape=jax.ShapeDtypeStruct((bsz, rows_out, d), F32),
        compiler_params=_cparams(("parallel", "parallel"), 40),
        name="out_proj",
    )(x, mods, ya, hf, hb, mq, yc, w_out, head_avg)


def _mlp_body(x_ref, m_ref, w1_ref, w2_ref, fg_ref, o_ref, u_ref, acc_ref, *, tm, n_ctx, final):
    b, i, k = pl.program_id(0), pl.program_id(1), pl.program_id(2)

    @pl.when(k == 0)
    def _():
        shift = _mod_rows(m_ref, b, i * tm, tm, n_ctx, 3)
        scale = _mod_rows(m_ref, b, i * tm, tm, n_ctx, 4)
        u_ref[...] = (_rms(x_ref[0]) * (1.0 + scale) + shift).astype(BF16)
        acc_ref[...] = jnp.zeros_like(acc_ref)

    hid = jnp.maximum(_dot(u_ref[...], w1_ref[0]), 0.0)
    acc_ref[...] += _dot((hid * hid).astype(BF16), w2_ref[0])

    @pl.when(k == pl.num_programs(2) - 1)
    def _():
        gate = _mod_rows(m_ref, b, i * tm, tm, n_ctx, 5)
        res = x_ref[0] + gate * acc_ref[...]
        if final:
            res = _rms(res) * fg_ref[...]
        o_ref[0] = res


def _mlp_call(x, mods, w1, w2, final_g, layer, n_ctx, final):
    bsz, s, d = x.shape
    tm = _pick_tile(s, (1088, 1024, 256))
    fc = 512
    return pl.pallas_call(
        functools.partial(_mlp_body, tm=tm, n_ctx=n_ctx, final=final),
        grid=(bsz, s // tm, D_FF // fc),
        in_specs=[pl.BlockSpec((1, tm, d), lambda b, i, k: (b, i, 0)),
                  pl.BlockSpec((1, MOD_ROWS, 6 * d), lambda b, i, k: (layer, 0, 0)),
                  pl.BlockSpec((1, d, fc), lambda b, i, k: (layer, 0, k)),
                  pl.BlockSpec((1, fc, d), lambda b, i, k: (layer, k, 0)),
                  pl.BlockSpec((1, d), lambda b, i, k: (0, 0))],
        out_specs=pl.BlockSpec((1, tm, d), lambda b, i, k: (b, i, 0)),
        out_shape=jax.ShapeDtypeStruct((bsz, s, d), F32),
        scratch_shapes=[pltpu.VMEM((tm, d), BF16), pltpu.VMEM((tm, d), F32)],
        compiler_params=_cparams(("parallel", "parallel", "arbitrary"), 48),
        name="mlp",
    )(x, mods, w1, w2, final_g)


def _rope_tables(seq, n_ctx):
    half = MLA_ROPE // 2
    row = jnp.repeat(jnp.arange(seq // GRID_W), GRID_W).astype(F32)
    col = jnp.tile(jnp.arange(GRID_W), seq // GRID_W).astype(F32)
    freqs = 1.0 / (ROPE_BASE ** (jnp.arange(0, half, 2, dtype=F32) / half))
    ang = jnp.concatenate([row[:, None] * freqs, col[:, None] * freqs], axis=-1)
    cos = jnp.concatenate([jnp.ones((n_ctx, half), F32), jnp.cos(ang)], axis=0)
    sin = jnp.concatenate([jnp.zeros((n_ctx, half), F32), jnp.sin(ang)], axis=0)
    return jnp.tile(cos, (1, MLA_HEADS)), jnp.tile(sin, (1, MLA_HEADS))


def _block_diag_halves(w):
    depth = w.shape[0]
    per = LANES // LRU_BD
    wh = w.reshape(depth, 2, LRU_BLOCKS // per, per, LRU_BD, LRU_BD)
    eye = jnp.eye(per, dtype=w.dtype)
    bd = jnp.einsum("ldcpio,pq->ldcpiqo", wh, eye).reshape(depth, 2, LRU_BLOCKS // per, LANES, LANES)
    return bd.transpose(0, 2, 1, 3, 4).astype(BF16)


def kernel(x, c, ctx, c_ctx, w_mod, b_mod, w_in, mla_g_q, mla_w_uq, mla_g_kv, mla_w_ukv, ml_gate_bias,
           lru_conv_w, lru_conv_b, lru_w_a, lru_b_a, lru_w_x, lru_b_x, lru_lam, w_out, w_ff1, w_ff2, final_g):
    bsz, seq, d = x.shape
    n_ctx = ctx.shape[1]
    depth = w_in.shape[0]
    assert bsz <= CTX_MOD_ROW and d == D_MODEL and n_ctx % 256 == 0

    zc = lambda n: jnp.zeros((depth, d, n), w_in.dtype)
    ml0, ml1 = MLA_IN, MLA_IN + 4 * ML_W
    w_in_p = jnp.concatenate([w_in[:, :, :ml0], zc(A_W - MLA_IN), w_in[:, :, ml0:ml1],
                              w_in[:, :, ml1:ml1 + 4 * ML_HEADS], zc(MG_W - 4 * ML_HEADS),
                              w_in[:, :, ml1 + 4 * ML_HEADS:]], axis=-1).astype(BF16)
    half = MLA_ROPE // 2
    uq = mla_w_uq.reshape(depth, MLA_Q_RANK, MLA_HEADS, MLA_QK)
    w_uq_p = jnp.concatenate([uq[..., :MLA_NOPE].reshape(depth, MLA_Q_RANK, -1),
                              uq[..., MLA_NOPE:MLA_NOPE + half].reshape(depth, MLA_Q_RANK, -1),
                              uq[..., MLA_NOPE + half:].reshape(depth, MLA_Q_RANK, -1)], axis=-1).astype(BF16)
    ukv = mla_w_ukv.reshape(depth, MLA_KV_RANK, MLA_HEADS, MLA_NOPE + MLA_V)
    w_ukv_p = jnp.concatenate([ukv[..., :MLA_NOPE].reshape(depth, MLA_KV_RANK, -1),
                               ukv[..., MLA_NOPE:].reshape(depth, MLA_KV_RANK, -1)], axis=-1).astype(BF16)
    g_q = mla_g_q.reshape(depth, 1, MLA_Q_RANK)
    g_kv = mla_g_kv.reshape(depth, 1, MLA_KV_RANK)
    bias_p = jnp.pad(ml_gate_bias, ((0, 0), (0, MG_W - 4 * ML_HEADS)))
    wa_bd, wx_bd = _block_diag_halves(lru_w_a), _block_diag_halves(lru_w_x)
    vec4 = lambda v: v.reshape(depth, 2, 1, LRU_W)
    conv_b = lru_conv_b.reshape(depth, 1, LRU_W)
    w_out16, w1_16, w2_16 = w_out.astype(BF16), w_ff1.astype(BF16), w_ff2.astype(BF16)
    head_avg = jnp.kron(jnp.eye(ML_HEADS, dtype=F32), jnp.full((ML_DH, ML_DH), 1.0 / ML_DH, F32))
    cos4, sin4 = _rope_tables(seq, n_ctx)
    fg = final_g.reshape(1, d)

    cvec = jnp.concatenate([c, jnp.zeros((CTX_MOD_ROW - bsz, d), c.dtype), c_ctx[None, :],
                            jnp.zeros((MOD_ROWS - CTX_MOD_ROW - 1, d), c.dtype)], axis=0)
    mods = _mods_call(cvec, w_mod, b_mod)

    xs = jnp.concatenate([ctx, x], axis=1)
    for l in range(depth):
        last = l == depth - 1
        a, mq, mg, r = _in_call(xs, mods, w_in_p, l, n_ctx)
        q, k, v = _prep_call(a, g_q, w_uq_p, g_kv, w_ukv_p, cos4, sin4, l)
        ya = _attn_call(q, k, v, n_ctx)
        hf, hb = _mlstm_call(mq, mg, bias_p[l:l + 1], n_ctx)
        yc = _lru_call(r, lru_conv_w, conv_b, wa_bd, vec4(lru_b_a), wx_bd, vec4(lru_b_x), vec4(lru_lam), l, n_ctx)
        x1 = _out_call(xs, mods, ya, hf, hb, mq, yc, w_out16, head_avg, l, n_ctx, skip_ctx=last)
        xs = _mlp_call(x1, mods, w1_16, w2_16, fg, l, 0 if last else n_ctx, final=last)
    return xs
```

```python
import functools

import jax
import jax.numpy as jnp
from jax import lax
from jax.experimental import pallas as pl
from jax.experimental.pallas import tpu as pltpu

F32 = jnp.float32
BF16 = jnp.bfloat16

D_MODEL = 1024
DEPTH = 4
GRID_W = 64
N_CTX = 256
MLA_HEADS = 4
MLA_Q_RANK = 256
MLA_KV_RANK = 128
MLA_NOPE = 128
MLA_ROPE = 64
MLA_V = 128
MLA_QK = MLA_NOPE + MLA_ROPE
MLA_SCALE = MLA_QK ** -0.5
ROPE_BASE = 10000.0
ML_HEADS = 4
ML_DH = 64
ML_W = ML_HEADS * ML_DH
ML_CHUNK = 128
LRU_W = 256
LRU_BLOCKS = 4
LRU_BD = LRU_W // LRU_BLOCKS
CONV_W = 4
CONV_LEFT = 2
LRU_C = 8.0
D_FF = 4 * D_MODEL
EPS = 1e-6
MLA_IN = MLA_Q_RANK + MLA_KV_RANK + MLA_ROPE
ML_IN = 4 * ML_W + 4 * ML_HEADS
LRU_IN = 2 * LRU_W

LANES = 128
SUBLANES = 8
MOD_ROWS = 8
CTX_MOD_ROW = 4

A_W = 512
MQ_W = 4 * ML_W
MG_W = LANES
R_W = 2 * LRU_W
IN_W = A_W + MQ_W + MG_W + R_W


def _cparams(sem, vmem_mb):
    return pltpu.CompilerParams(dimension_semantics=sem, vmem_limit_bytes=vmem_mb * 1024 * 1024)


def _mod_rows(m_ref, b, row0, tm, n_ctx, seg):
    lo, hi = seg * D_MODEL, (seg + 1) * D_MODEL
    lat = m_ref[0, pl.ds(b, 1), lo:hi]
    if n_ctx == 0:
        return lat
    ctx = m_ref[0, CTX_MOD_ROW:CTX_MOD_ROW + 1, lo:hi]
    rows = row0 + lax.broadcasted_iota(jnp.int32, (tm, 1), 0)
    return jnp.where(rows < n_ctx, ctx, lat)


def _pick_tile(n, candidates):
    return next(t for t in candidates if n % t == 0)


def _rms(x):
    return x * lax.rsqrt(jnp.mean(x * x, axis=-1, keepdims=True) + EPS)


def _dot(a, b):
    return jnp.dot(a, b, preferred_element_type=F32)


def _dot_nt(a, b):
    return lax.dot_general(a, b, (((1,), (1,)), ((), ())), preferred_element_type=F32)


def _mods_body(c_ref, w_ref, b_ref, o_ref):
    cv = c_ref[...]
    act = (cv * jax.nn.sigmoid(cv)).astype(BF16)
    o_ref[0] = _dot(act, w_ref[0].astype(BF16)) + b_ref[0]


def _mods_call(cvec, w_mod, b_mod):
    depth, d, n = w_mod.shape
    tn = 1536
    return pl.pallas_call(
        _mods_body,
        grid=(depth, n // tn),
        in_specs=[pl.BlockSpec((MOD_ROWS, d), lambda l, j: (0, 0)),
                  pl.BlockSpec((1, d, tn), lambda l, j: (l, 0, j)),
                  pl.BlockSpec((1, 1, tn), lambda l, j: (l, 0, j))],
        out_specs=pl.BlockSpec((1, MOD_ROWS, tn), lambda l, j: (l, 0, j)),
        out_shape=jax.ShapeDtypeStruct((depth, MOD_ROWS, n), F32),
        compiler_params=_cparams(("arbitrary", "arbitrary"), 40),
        name="mods",
    )(cvec, w_mod, b_mod.reshape(depth, 1, n))


def _in_body(x_ref, m_ref, w_ref, a_ref, q_ref, g_ref, r_ref, *, tm, n_ctx):
    b, i = pl.program_id(0), pl.program_id(1)
    xn = _rms(x_ref[0])
    shift = _mod_rows(m_ref, b, i * tm, tm, n_ctx, 0)
    scale = _mod_rows(m_ref, b, i * tm, tm, n_ctx, 1)
    u = (xn * (1.0 + scale) + shift).astype(BF16)
    a_ref[0] = _dot(u, w_ref[0, :, 0:A_W])
    q_ref[0] = _dot(u, w_ref[0, :, A_W:A_W + MQ_W])
    g_ref[0] = _dot(u, w_ref[0, :, A_W + MQ_W:A_W + MQ_W + MG_W])
    r_ref[0] = _dot(u, w_ref[0, :, A_W + MQ_W + MG_W:IN_W])


def _in_call(x, mods, w_in_p, layer, n_ctx):
    bsz, s, d = x.shape
    tm = _pick_tile(s, (544, 256))
    row = lambda b, i: (b, i, 0)
    return pl.pallas_call(
        functools.partial(_in_body, tm=tm, n_ctx=n_ctx),
        grid=(bsz, s // tm),
        in_specs=[pl.BlockSpec((1, tm, d), row),
                  pl.BlockSpec((1, MOD_ROWS, 6 * d), lambda b, i: (layer, 0, 0)),
                  pl.BlockSpec((1, d, IN_W), lambda b, i: (layer, 0, 0))],
        out_specs=[pl.BlockSpec((1, tm, A_W), row), pl.BlockSpec((1, tm, MQ_W), row),
                   pl.BlockSpec((1, tm, MG_W), row), pl.BlockSpec((1, tm, R_W), row)],
        out_shape=[jax.ShapeDtypeStruct((bsz, s, w), F32) for w in (A_W, MQ_W, MG_W, R_W)],
        compiler_params=_cparams(("parallel", "parallel"), 40),
        name="in_proj",
    )(x, mods, w_in_p)


VT_ROWS = MLA_V + 16
LOG2E = 1.4426950408889634


def _prep_body(a_ref, gq_ref, gkv_ref, wuqt_ref, wukvk_ref, wukvvt_ref, cos_ref, sin_ref, cost_ref, sint_ref,
               qt_ref, k_ref, vt_ref):
    a = a_ref[0]
    tm = a.shape[0]
    nq, nkv = MLA_Q_RANK, MLA_Q_RANK + MLA_KV_RANK
    half = MLA_ROPE // 2
    hn = MLA_HEADS * MLA_NOPE
    cq_t = (_rms(a[:, 0:nq]) * gq_ref[0]).T.astype(BF16)
    ckv = _rms(a[:, nq:nkv]) * gkv_ref[0]
    q_t = _dot(wuqt_ref[0], cq_t) * (MLA_SCALE * LOG2E)
    x1, x2 = q_t[hn:hn + LANES], q_t[hn + LANES:hn + 2 * LANES]
    cos_t, sin_t = cost_ref[...], sint_ref[...]
    r1 = (x1 * cos_t - x2 * sin_t).astype(BF16)
    r2 = (x1 * sin_t + x2 * cos_t).astype(BF16)
    k_nope = _dot(ckv.astype(BF16), wukvk_ref[0])
    v_t = _dot(wukvvt_ref[0], ckv.T.astype(BF16))
    k1, k2 = a[:, nkv:nkv + half], a[:, nkv + half:nkv + 2 * half]
    c32, s32 = cos_ref[:, 0:half], sin_ref[:, 0:half]
    kr1 = (k1 * c32 - k2 * s32).astype(BF16)
    kr2 = (k1 * s32 + k2 * c32).astype(BF16)
    ones = jnp.ones((VT_ROWS - MLA_V, tm), BF16)
    for h in range(MLA_HEADS):
        qt_ref[0, h, 0:MLA_NOPE, :] = q_t[h * MLA_NOPE:(h + 1) * MLA_NOPE].astype(BF16)
        qt_ref[0, h, MLA_NOPE:MLA_NOPE + half, :] = r1[h * half:(h + 1) * half]
        qt_ref[0, h, MLA_NOPE + half:MLA_QK, :] = r2[h * half:(h + 1) * half]
        k_ref[0, h, :, 0:MLA_NOPE] = k_nope[:, h * MLA_NOPE:(h + 1) * MLA_NOPE].astype(BF16)
        k_ref[0, h, :, MLA_NOPE:MLA_NOPE + half] = kr1
        k_ref[0, h, :, MLA_NOPE + half:MLA_QK] = kr2
        vt_ref[0, h, 0:MLA_V, :] = v_t[h * MLA_V:(h + 1) * MLA_V].astype(BF16)
        vt_ref[0, h, MLA_V:VT_ROWS, :] = ones


def _prep_call(a, g_q, w_uq_t, g_kv, w_ukv_k, w_ukv_vt, rope, layer, n_ctx):
    bsz, s, _ = a.shape
    tm = 256
    nblk, nctx_blk = s // tm, n_ctx // tm
    lsel = lambda b, i: (layer, 0, 0)
    cos4, sin4, cos4_t, sin4_t = rope
    return pl.pallas_call(
        _prep_body,
        grid=(bsz, s // tm),
        in_specs=[pl.BlockSpec((1, tm, A_W), lambda b, i: (b, i, 0)),
                  pl.BlockSpec((1, 1, MLA_Q_RANK), lsel),
                  pl.BlockSpec((1, 1, MLA_KV_RANK), lsel),
                  pl.BlockSpec((1, MLA_HEADS * MLA_QK, MLA_Q_RANK), lsel),
                  pl.BlockSpec((1, MLA_KV_RANK, MLA_HEADS * MLA_NOPE), lsel),
                  pl.BlockSpec((1, MLA_HEADS * MLA_V, MLA_KV_RANK), lsel),
                  pl.BlockSpec((tm, LANES), lambda b, i: (i, 0)),
                  pl.BlockSpec((tm, LANES), lambda b, i: (i, 0)),
                  pl.BlockSpec((LANES, tm), lambda b, i: (0, i)),
                  pl.BlockSpec((LANES, tm), lambda b, i: (0, i))],
        out_specs=[pl.BlockSpec((1, MLA_HEADS, MLA_QK, tm), lambda b, i: (b, 0, 0, (i + nblk - nctx_blk) % nblk)),
                   pl.BlockSpec((1, MLA_HEADS, tm, MLA_QK), lambda b, i: (b, 0, i, 0)),
                   pl.BlockSpec((1, MLA_HEADS, VT_ROWS, tm), lambda b, i: (b, 0, 0, i))],
        out_shape=[jax.ShapeDtypeStruct((bsz, MLA_HEADS, MLA_QK, s), BF16),
                   jax.ShapeDtypeStruct((bsz, MLA_HEADS, s, MLA_QK), BF16),
                   jax.ShapeDtypeStruct((bsz, MLA_HEADS, VT_ROWS, s), BF16)],
        compiler_params=_cparams(("parallel", "parallel"), 40),
        name="mla_prep",
    )(a, g_q, g_kv, w_uq_t, w_ukv_k, w_ukv_vt, cos4, sin4, cos4_t, sin4_t)


def _attn_body(qt_ref, k_ref, vt_ref, o_ref, *, chunks):
    qt = qt_ref[0, 0]
    tq = qt.shape[1]
    m = jnp.full((1, tq), -jnp.inf, F32)
    acc = jnp.zeros((VT_ROWS, tq), F32)
    score = lambda start, size: _dot(k_ref[0, 0, start:start + size, :], qt)
    st_next = score(*chunks[0])
    for idx, (start, size) in enumerate(chunks):
        st = st_next
        if idx + 1 < len(chunks):
            st_next = score(*chunks[idx + 1])
        m_new = jnp.maximum(m, jnp.max(st, axis=0, keepdims=True))
        p = jnp.exp2(st - m_new).astype(BF16)
        acc = jnp.exp2(m - m_new) * acc + _dot(vt_ref[0, 0, :, start:start + size], p)
        m = m_new
    out_t = acc[0:MLA_V] / acc[MLA_V:MLA_V + 1]
    o_ref[0] = out_t.T.astype(o_ref.dtype)


ATTN_TQ = 512
ATTN_TK = 512


def _attn_latent_call(qt, k, vt, n_ctx):
    bsz, nh, s, dk = k.shape
    chunks = ((0, n_ctx),) + tuple((n_ctx + j * ATTN_TK, ATTN_TK) for j in range((s - n_ctx) // ATTN_TK))
    return pl.pallas_call(
        functools.partial(_attn_body, chunks=chunks),
        grid=(bsz, nh, (s - n_ctx) // ATTN_TQ),
        in_specs=[pl.BlockSpec((1, 1, dk, ATTN_TQ), lambda b, h, i: (b, h, 0, i)),
                  pl.BlockSpec((1, 1, s, dk), lambda b, h, i: (b, h, 0, 0)),
                  pl.BlockSpec((1, 1, VT_ROWS, s), lambda b, h, i: (b, h, 0, 0))],
        out_specs=pl.BlockSpec((1, ATTN_TQ, MLA_V), lambda b, h, i: (b, i, h)),
        out_shape=jax.ShapeDtypeStruct((bsz, s - n_ctx, nh * MLA_V), BF16),
        compiler_params=_cparams(("parallel", "parallel", "arbitrary"), 48),
        name="mla_attn",
    )(qt, k, vt)


def _attn_ctx_call(qt, k, vt, n_ctx):
    bsz, nh, s, dk = k.shape
    qblk = (s - n_ctx) // n_ctx
    return pl.pallas_call(
        functools.partial(_attn_body, chunks=((0, n_ctx),)),
        grid=(bsz, nh),
        in_specs=[pl.BlockSpec((1, 1, dk, n_ctx), lambda b, h: (b, h, 0, qblk)),
                  pl.BlockSpec((1, 1, n_ctx, dk), lambda b, h: (b, h, 0, 0)),
                  pl.BlockSpec((1, 1, VT_ROWS, n_ctx), lambda b, h: (b, h, 0, 0))],
        out_specs=pl.BlockSpec((1, n_ctx, MLA_V), lambda b, h: (b, 0, h)),
        out_shape=jax.ShapeDtypeStruct((bsz, n_ctx, nh * MLA_V), BF16),
        compiler_params=_cparams(("parallel", "parallel"), 40),
        name="mla_attn_ctx",
    )(qt, k, vt)


def _mlstm_body(xf_ref, gf_ref, xb_ref, gb_ref, bias_ref, hf_ref, hb_ref, c_ref, m_ref):
    lc = ML_CHUNK

    @pl.when(pl.program_id(1) == 0)
    def _():
        c_ref[...] = jnp.zeros_like(c_ref)
        m_ref[...] = jnp.zeros_like(m_ref)

    r_io = lax.broadcasted_iota(jnp.int32, (lc, lc), 0)
    c_io = lax.broadcasted_iota(jnp.int32, (lc, lc), 1)
    lane = lax.broadcasted_iota(jnp.int32, (lc, LANES), 1)
    sub = lax.broadcasted_iota(jnp.int32, (LANES, lc), 0)
    bias = bias_ref[...]

    for d, (x_ref, g_ref, o_ref) in enumerate(((xf_ref, gf_ref, hf_ref), (xb_ref, gb_ref, hb_ref))):
        mask = (r_io >= c_io) if d == 0 else (r_io <= c_io)
        last = lc - 1 if d == 0 else 0
        g = g_ref[0] + bias
        lf = jax.nn.log_sigmoid(g)
        bc = jnp.dot(mask.astype(F32), lf, precision=lax.Precision.HIGHEST,
                     preferred_element_type=F32)
        br = bc.T
        gr = g.T
        x = x_ref[0]
        for pair in range(ML_HEADS // 2):
            qs = x[:, pair * LANES:(pair + 1) * LANES] * (ML_DH ** -0.5)
            ks = x[:, ML_W + pair * LANES:ML_W + (pair + 1) * LANES]
            vs = x[:, 2 * ML_W + pair * LANES:2 * ML_W + (pair + 1) * LANES]
            kst = ks.T
            outs = []
            for odd in range(2):
                h = 2 * pair + odd
                ci, cf = d * 2 * ML_HEADS + h, d * 2 * ML_HEADS + ML_HEADS + h
                own = (lane >= ML_DH) if odd else (lane < ML_DH)
                own_t = (sub >= ML_DH) if odd else (sub < ML_DH)
                den_lane = 0 if odd else ML_DH
                bcol, icol = bc[:, cf:cf + 1], g[:, ci:ci + 1]
                brow, irow = br[cf:cf + 1, :], gr[ci:ci + 1, :]
                st = d * ML_HEADS + h
                m_prev = m_ref[st:st + 1, 0:1]
                dmat = jnp.where(mask, bcol - brow + irow, -jnp.inf)
                inter = bcol + m_prev
                m_row = jnp.maximum(inter, jnp.max(dmat, axis=-1, keepdims=True))
                w_intra = jnp.exp(dmat - m_row)
                w_inter = jnp.exp(inter - m_row)
                qm = jnp.where(own, qs, 0.0).astype(BF16)
                sc = (_dot_nt(qm, ks.astype(BF16)) * w_intra).astype(BF16)
                vaug = jnp.where(own, vs, jnp.where(lane == den_lane, 1.0, 0.0))
                c_old = c_ref[st]
                haug = _dot(sc, vaug.astype(BF16)) + w_inter * _dot(qm, c_old.astype(BF16))
                den = haug[:, den_lane:den_lane + 1]
                outs.append(haug / jnp.maximum(jnp.abs(den), jnp.exp(-m_row)))
                b_last = bcol[last:last + 1, :]
                gvec = b_last - bcol + icol
                m_new = jnp.maximum(b_last + m_prev, jnp.max(gvec, axis=0, keepdims=True))
                w_old = jnp.exp(b_last + m_prev - m_new)
                w_s = jnp.exp(gvec - m_new)
                ktm = jnp.where(own_t, kst, 0.0).astype(BF16)
                c_ref[st] = w_old * c_old + _dot(ktm, (w_s * vaug).astype(BF16))
                m_ref[st:st + 1, :] = jnp.broadcast_to(m_new, (1, LANES))
            o_ref[0, :, pair * LANES:(pair + 1) * LANES] = jnp.where(lane < ML_DH, outs[0], outs[1])


def _mlstm_call(mq, mg, bias_p, n_ctx):
    bsz, s, _ = mq.shape
    nch, ncc = s // ML_CHUNK, n_ctx // ML_CHUNK
    fwd = lambda b, j: (b, j, 0)
    bwd = lambda b, j: (b, jnp.where(j < ncc, ncc - 1 - j, nch - 1 + ncc - j), 0)
    out = jax.ShapeDtypeStruct((bsz, s, ML_W), F32)
    return pl.pallas_call(
        _mlstm_body,
        grid=(bsz, nch),
        in_specs=[pl.BlockSpec((1, ML_CHUNK, MQ_W), fwd), pl.BlockSpec((1, ML_CHUNK, MG_W), fwd),
                  pl.BlockSpec((1, ML_CHUNK, MQ_W), bwd), pl.BlockSpec((1, ML_CHUNK, MG_W), bwd),
                  pl.BlockSpec((1, MG_W), lambda b, j: (0, 0))],
        out_specs=[pl.BlockSpec((1, ML_CHUNK, ML_W), fwd), pl.BlockSpec((1, ML_CHUNK, ML_W), bwd)],
        out_shape=[out, out],
        scratch_shapes=[pltpu.VMEM((2 * ML_HEADS, LANES, LANES), F32),
                        pltpu.VMEM((2 * ML_HEADS, LANES), F32)],
        compiler_params=_cparams(("parallel", "arbitrary"), 40),
        name="mlstm",
    )(mq, mg, mq, mg, bias_p)


LRU_PITCH_PAD = 8


def _lru_body(xb_ref, gb_ref, cw_ref, cb_ref, wa_ref, ba_ref, wx_ref, bx_ref, lam_ref, y_ref,
              xp_ref, xs_ref, a_ref, u_ref, hs_ref, *, n_ctx, s_len):
    n_lat = s_len - n_ctx
    pad = SUBLANES
    lat_off = n_ctx + 2 * pad
    zeros = jnp.zeros((pad, LANES), F32)
    xp_ref[0:pad, :] = zeros
    xp_ref[pad + n_ctx:lat_off, :] = zeros
    xp_ref[lat_off + n_lat:lat_off + n_lat + pad, :] = zeros
    xp_ref[pad:pad + n_ctx, :] = xb_ref[0, 0:n_ctx, :]
    cchunk = 512

    def copy_body(c, _):
        src = pl.multiple_of(n_ctx + c * cchunk, SUBLANES)
        dst = pl.multiple_of(lat_off + c * cchunk, SUBLANES)
        xp_ref[pl.ds(dst, cchunk), :] = xb_ref[0, pl.ds(src, cchunk), :]
        return 0

    lax.fori_loop(0, n_lat // cchunk, copy_body, 0)

    cw = cw_ref[...]
    cb = cb_ref[...]

    def conv(src0, dst0, n):
        acc = cb + xp_ref[src0 - CONV_LEFT:src0 - CONV_LEFT + n, :] * cw[0:1, :]
        for j in range(1, CONV_W):
            acc = acc + xp_ref[src0 - CONV_LEFT + j:src0 - CONV_LEFT + j + n, :] * cw[j:j + 1, :]
        xs_ref[dst0:dst0 + n, :] = acc

    conv(pad, 0, n_ctx)
    for c in range(n_lat // cchunk):
        conv(lat_off + c * cchunk, n_ctx + c * cchunk, cchunk)

    seg_lat = n_lat // SUBLANES
    seg_ctx = n_ctx // SUBLANES
    p_lat, p_ctx = seg_lat + LRU_PITCH_PAD, seg_ctx + LRU_PITCH_PAD
    ctx_base = SUBLANES * p_lat
    row_io = lax.broadcasted_iota(jnp.int32, (SUBLANES, LANES), 0)

    def gates(x, d):
        xb16 = x.astype(BF16)
        r = jax.nn.sigmoid(_dot(xb16, wa_ref[d]) + ba_ref[d])
        i = jax.nn.sigmoid(_dot(xb16, wx_ref[d]) + bx_ref[d])
        log_a = (-LRU_C) * r * jax.nn.softplus(-lam_ref[d])
        a = jnp.exp(log_a)
        return a, jnp.sqrt(jnp.tanh(-log_a) * (1.0 + a * a)) * (i * x)

    def fill(d):
        def lat_body(r, _):
            src = pl.multiple_of(n_ctx + r * seg_lat, SUBLANES)
            dst = pl.multiple_of(r * p_lat, SUBLANES)
            a, u = gates(xs_ref[pl.ds(src, seg_lat), :], d)
            a_ref[pl.ds(dst, seg_lat), :] = a
            u_ref[pl.ds(dst, seg_lat), :] = u
            return 0

        lax.fori_loop(0, SUBLANES, lat_body, 0)
        a, u = gates(xs_ref[0:n_ctx, :], d)
        for r in range(SUBLANES):
            a_ref[ctx_base + r * p_ctx:ctx_base + r * p_ctx + seg_ctx, :] = a[r * seg_ctx:(r + 1) * seg_ctx, :]
            u_ref[ctx_base + r * p_ctx:ctx_base + r * p_ctx + seg_ctx, :] = u[r * seg_ctx:(r + 1) * seg_ctx, :]

    def scan(base, n, pitch, reverse, h0, accumulate):
        def pass1(t, carry):
            h, acum = carry
            j = (n - 1 - t) if reverse else t
            idx = pl.ds(base + j, SUBLANES, stride=pitch)
            a = a_ref[idx, :]
            h = a * h + u_ref[idx, :]
            acum = acum * a
            u_ref[idx, :] = h
            a_ref[idx, :] = acum
            return h, acum

        h_end, a_end = lax.fori_loop(0, n, pass1, (jnp.zeros((SUBLANES, LANES), F32),
                                                    jnp.ones((SUBLANES, LANES), F32)))
        carry = h0
        cvec = jnp.zeros((SUBLANES, LANES), F32)
        order = range(SUBLANES - 1, -1, -1) if reverse else range(SUBLANES)
        for r in order:
            cvec = jnp.where(row_io == r, carry, cvec)
            carry = h_end[r:r + 1, :] + a_end[r:r + 1, :] * carry

        def pass2(t, _):
            idx = pl.ds(base + t, SUBLANES, stride=pitch)
            res = u_ref[idx, :] + a_ref[idx, :] * cvec
            if accumulate:
                res = res + hs_ref[idx, :]
            hs_ref[idx, :] = res
            return 0

        lax.fori_loop(0, n, pass2, 0)
        return carry

    for d in range(2):
        fill(d)
        h_ctx = scan(ctx_base, seg_ctx, p_ctx, d == 1, jnp.zeros((1, LANES), F32), d == 1)
        scan(0, seg_lat, p_lat, d == 1, h_ctx, d == 1)

    def out_body(r, _):
        dst = pl.multiple_of(n_ctx + r * seg_lat, SUBLANES)
        src = pl.multiple_of(r * p_lat, SUBLANES)
        y_ref[0, pl.ds(dst, seg_lat), :] = (jax.nn.gelu(gb_ref[0, pl.ds(dst, seg_lat), :])
                                            * hs_ref[pl.ds(src, seg_lat), :])
        return 0

    lax.fori_loop(0, SUBLANES, out_body, 0)
    for r in range(SUBLANES):
        y_ref[0, r * seg_ctx:(r + 1) * seg_ctx, :] = (
            jax.nn.gelu(gb_ref[0, r * seg_ctx:(r + 1) * seg_ctx, :])
            * hs_ref[ctx_base + r * p_ctx:ctx_base + r * p_ctx + seg_ctx, :])


def _lru_call(r, conv_w, conv_b, wa_bd, b_a, wx_bd, b_x, lam, layer, n_ctx):
    bsz, s, _ = r.shape
    nh = LRU_W // LANES
    n_lat = s - n_ctx
    scan_rows = SUBLANES * (n_lat // SUBLANES + LRU_PITCH_PAD) + SUBLANES * (n_ctx // SUBLANES + LRU_PITCH_PAD)
    vec = lambda b, c: (layer, 0, 0, c)
    return pl.pallas_call(
        functools.partial(_lru_body, n_ctx=n_ctx, s_len=s),
        grid=(bsz, nh),
        in_specs=[pl.BlockSpec((1, s, LANES), lambda b, c: (b, 0, c)),
                  pl.BlockSpec((1, s, LANES), lambda b, c: (b, 0, nh + c)),
                  pl.BlockSpec((None, CONV_W, LANES), lambda b, c: (layer, 0, c)),
                  pl.BlockSpec((None, 1, LANES), lambda b, c: (layer, 0, c)),
                  pl.BlockSpec((None, None, 2, LANES, LANES), lambda b, c: (layer, c, 0, 0, 0)),
                  pl.BlockSpec((None, 2, 1, LANES), vec),
                  pl.BlockSpec((None, None, 2, LANES, LANES), lambda b, c: (layer, c, 0, 0, 0)),
                  pl.BlockSpec((None, 2, 1, LANES), vec),
                  pl.BlockSpec((None, 2, 1, LANES), vec)],
        out_specs=pl.BlockSpec((1, s, LANES), lambda b, c: (b, 0, c)),
        out_shape=jax.ShapeDtypeStruct((bsz, s, LRU_W), F32),
        scratch_shapes=[pltpu.VMEM((s + 3 * SUBLANES, LANES), F32),
                        pltpu.VMEM((s, LANES), F32),
                        pltpu.VMEM((scan_rows, LANES), F32),
                        pltpu.VMEM((scan_rows, LANES), F32),
                        pltpu.VMEM((scan_rows, LANES), F32)],
        compiler_params=_cparams(("parallel", "parallel"), 48),
        name="rglru",
    )(r, r, conv_w, conv_b, wa_bd, b_a, wx_bd, b_x, lam)


def _out_body(x_ref, m_ref, yal_ref, yac_ref, hf_ref, hb_ref, og_ref, yc_ref, w_ref, hn_ref, o_ref,
              *, tm, n_ctx, blk0):
    b, i = pl.program_id(0), pl.program_id(1)
    hsum = hf_ref[0] + hb_ref[0]
    msq = jnp.dot(hsum * hsum, hn_ref[...], precision=lax.Precision.HIGHEST, preferred_element_type=F32)
    yb = (jax.nn.sigmoid(og_ref[0]) * (hsum * lax.rsqrt(msq + EPS))).astype(BF16)
    na, nb = MLA_HEADS * MLA_V, MLA_HEADS * MLA_V + ML_W
    ya = yal_ref[0]
    if blk0 == 0:
        ya = jnp.where((i + blk0) * tm < n_ctx, yac_ref[0], ya)
    y = (_dot(ya, w_ref[0, 0:na, :]) + _dot(yb, w_ref[0, na:nb, :])
         + _dot(yc_ref[0].astype(BF16), w_ref[0, nb:, :]))
    gate = _mod_rows(m_ref, b, (i + blk0) * tm, tm, n_ctx, 2)
    o_ref[0] = x_ref[0] + gate * y


def _out_call(x, mods, ya_lat, ya_ctx, hf, hb, mq, yc, w_out, head_avg, layer, n_ctx, skip_ctx):
    bsz, s, d = x.shape
    tm = n_ctx
    nctx_blk = n_ctx // tm
    blk0 = nctx_blk if skip_ctx else 0
    rows_out = s - blk0 * tm
    row = lambda b, i: (b, i + blk0, 0)
    return pl.pallas_call(
        functools.partial(_out_body, tm=tm, n_ctx=n_ctx, blk0=blk0),
        grid=(bsz, rows_out // tm),
        in_specs=[pl.BlockSpec((1, tm, d), row),
                  pl.BlockSpec((1, MOD_ROWS, 6 * d), lambda b, i: (layer, 0, 0)),
                  pl.BlockSpec((1, tm, MLA_HEADS * MLA_V), lambda b, i: (b, jnp.maximum(i + blk0 - nctx_blk, 0), 0)),
                  pl.BlockSpec((1, tm, MLA_HEADS * MLA_V), lambda b, i: (b, 0, 0)),
                  pl.BlockSpec((1, tm, ML_W), row),
                  pl.BlockSpec((1, tm, ML_W), row),
                  pl.BlockSpec((1, tm, ML_W), lambda b, i: (b, i + blk0, 3)),
                  pl.BlockSpec((1, tm, LRU_W), row),
                  pl.BlockSpec((1, d, d), lambda b, i: (layer, 0, 0)),
                  pl.BlockSpec((ML_W, ML_W), lambda b, i: (0, 0))],
        out_specs=pl.BlockSpec((1, tm, d), lambda b, i: (b, i, 0)),
        out_shape=jax.ShapeDtypeStruct((bsz, rows_out, d), F32),
        compiler_params=_cparams(("parallel", "parallel"), 40),
        name="out_proj",
    )(x, mods, ya_lat, ya_lat if ya_ctx is None else ya_ctx, hf, hb, mq, yc, w_out, head_avg)


def _mlp_body(x_ref, m_ref, w1_ref, w2_ref, fg_ref, o_ref, u_ref, acc_ref, *, tm, n_ctx, final):
    b, i, k = pl.program_id(0), pl.program_id(1), pl.program_id(2)

    @pl.when(k == 0)
    def _():
        shift = _mod_rows(m_ref, b, i * tm, tm, n_ctx, 3)
        scale = _mod_rows(m_ref, b, i * tm, tm, n_ctx, 4)
        u_ref[...] = (_rms(x_ref[0]) * (1.0 + scale) + shift).astype(BF16)
        acc_ref[...] = jnp.zeros_like(acc_ref)

    hid = jnp.maximum(_dot(u_ref[...], w1_ref[0]), 0.0)
    acc_ref[...] += _dot((hid * hid).astype(BF16), w2_ref[0])

    @pl.when(k == pl.num_programs(2) - 1)
    def _():
        gate = _mod_rows(m_ref, b, i * tm, tm, n_ctx, 5)
        res = x_ref[0] + gate * acc_ref[...]
        if final:
            res = _rms(res) * fg_ref[...]
        o_ref[0] = res


def _mlp_call(x, mods, w1, w2, final_g, layer, n_ctx, final):
    bsz, s, d = x.shape
    tm = _pick_tile(s, (1088, 1024, 256))
    fc = 512
    return pl.pallas_call(
        functools.partial(_mlp_body, tm=tm, n_ctx=n_ctx, final=final),
        grid=(bsz, s // tm, D_FF // fc),
        in_specs=[pl.BlockSpec((1, tm, d), lambda b, i, k: (b, i, 0)),
                  pl.BlockSpec((1, MOD_ROWS, 6 * d), lambda b, i, k: (layer, 0, 0)),
                  pl.BlockSpec((1, d, fc), lambda b, i, k: (layer, 0, k)),
                  pl.BlockSpec((1, fc, d), lambda b, i, k: (layer, k, 0)),
                  pl.BlockSpec((1, d), lambda b, i, k: (0, 0))],
        out_specs=pl.BlockSpec((1, tm, d), lambda b, i, k: (b, i, 0)),
        out_shape=jax.ShapeDtypeStruct((bsz, s, d), F32),
        scratch_shapes=[pltpu.VMEM((tm, d), BF16), pltpu.VMEM((tm, d), F32)],
        compiler_params=_cparams(("parallel", "parallel", "arbitrary"), 48),
        name="mlp",
    )(x, mods, w1, w2, final_g)


def _rope_tables(seq, n_ctx):
    half = MLA_ROPE // 2
    row = jnp.repeat(jnp.arange(seq // GRID_W), GRID_W).astype(F32)
    col = jnp.tile(jnp.arange(GRID_W), seq // GRID_W).astype(F32)
    freqs = 1.0 / (ROPE_BASE ** (jnp.arange(0, half, 2, dtype=F32) / half))
    ang = jnp.concatenate([row[:, None] * freqs, col[:, None] * freqs], axis=-1)
    cos = jnp.concatenate([jnp.ones((n_ctx, half), F32), jnp.cos(ang)], axis=0)
    sin = jnp.concatenate([jnp.zeros((n_ctx, half), F32), jnp.sin(ang)], axis=0)
    return jnp.tile(cos, (1, MLA_HEADS)), jnp.tile(sin, (1, MLA_HEADS))


def _block_diag_halves(w):
    depth = w.shape[0]
    per = LANES // LRU_BD
    wh = w.reshape(depth, 2, LRU_BLOCKS // per, per, LRU_BD, LRU_BD)
    eye = jnp.eye(per, dtype=w.dtype)
    bd = jnp.einsum("ldcpio,pq->ldcpiqo", wh, eye).reshape(depth, 2, LRU_BLOCKS // per, LANES, LANES)
    return bd.transpose(0, 2, 1, 3, 4).astype(BF16)


def kernel(x, c, ctx, c_ctx, w_mod, b_mod, w_in, mla_g_q, mla_w_uq, mla_g_kv, mla_w_ukv, ml_gate_bias,
           lru_conv_w, lru_conv_b, lru_w_a, lru_b_a, lru_w_x, lru_b_x, lru_lam, w_out, w_ff1, w_ff2, final_g):
    bsz, seq, d = x.shape
    n_ctx = ctx.shape[1]
    depth = w_in.shape[0]
    assert bsz <= CTX_MOD_ROW and d == D_MODEL and n_ctx % 256 == 0

    zc = lambda n: jnp.zeros((depth, d, n), w_in.dtype)
    ml0, ml1 = MLA_IN, MLA_IN + 4 * ML_W
    w_in_p = jnp.concatenate([w_in[:, :, :ml0], zc(A_W - MLA_IN), w_in[:, :, ml0:ml1],
                              w_in[:, :, ml1:ml1 + 4 * ML_HEADS], zc(MG_W - 4 * ML_HEADS),
                              w_in[:, :, ml1 + 4 * ML_HEADS:]], axis=-1).astype(BF16)
    half = MLA_ROPE // 2
    uq = mla_w_uq.reshape(depth, MLA_Q_RANK, MLA_HEADS, MLA_QK)
    w_uq_t = jnp.concatenate([uq[..., :MLA_NOPE].reshape(depth, MLA_Q_RANK, -1),
                              uq[..., MLA_NOPE:MLA_NOPE + half].reshape(depth, MLA_Q_RANK, -1),
                              uq[..., MLA_NOPE + half:].reshape(depth, MLA_Q_RANK, -1)],
                             axis=-1).astype(BF16).transpose(0, 2, 1)
    ukv = mla_w_ukv.reshape(depth, MLA_KV_RANK, MLA_HEADS, MLA_NOPE + MLA_V)
    w_ukv_k = ukv[..., :MLA_NOPE].reshape(depth, MLA_KV_RANK, -1).astype(BF16)
    w_ukv_vt = ukv[..., MLA_NOPE:].reshape(depth, MLA_KV_RANK, -1).astype(BF16).transpose(0, 2, 1)
    g_q = mla_g_q.reshape(depth, 1, MLA_Q_RANK)
    g_kv = mla_g_kv.reshape(depth, 1, MLA_KV_RANK)
    bias_p = jnp.pad(ml_gate_bias, ((0, 0), (0, MG_W - 4 * ML_HEADS)))
    wa_bd, wx_bd = _block_diag_halves(lru_w_a), _block_diag_halves(lru_w_x)
    vec4 = lambda v: v.reshape(depth, 2, 1, LRU_W)
    conv_b = lru_conv_b.reshape(depth, 1, LRU_W)
    w_out16, w1_16, w2_16 = w_out.astype(BF16), w_ff1.astype(BF16), w_ff2.astype(BF16)
    head_avg = jnp.kron(jnp.eye(ML_HEADS, dtype=F32), jnp.full((ML_DH, ML_DH), 1.0 / ML_DH, F32))
    cos4, sin4 = _rope_tables(seq, n_ctx)
    rope = (cos4, sin4, cos4.T, sin4.T)
    fg = final_g.reshape(1, d)

    cvec = jnp.concatenate([c, jnp.zeros((CTX_MOD_ROW - bsz, d), c.dtype), c_ctx[None, :],
                            jnp.zeros((MOD_ROWS - CTX_MOD_ROW - 1, d), c.dtype)], axis=0)
    mods = _mods_call(cvec, w_mod, b_mod)

    xs = jnp.concatenate([ctx, x], axis=1)
    for l in range(depth):
        last = l == depth - 1
        a, mq, mg, r = _in_call(xs, mods, w_in_p, l, n_ctx)
        qt, k, vt = _prep_call(a, g_q, w_uq_t, g_kv, w_ukv_k, w_ukv_vt, rope, l, n_ctx)
        ya_lat = _attn_latent_call(qt, k, vt, n_ctx)
        ya_ctx = None if last else _attn_ctx_call(qt, k, vt, n_ctx)
        hf, hb = _mlstm_call(mq, mg, bias_p[l:l + 1], n_ctx)
        yc = _lru_call(r, lru_conv_w, conv_b, wa_bd, vec4(lru_b_a), wx_bd, vec4(lru_b_x), vec4(lru_lam), l, n_ctx)
        x1 = _out_call(xs, mods, ya_lat, ya_ctx, hf, hb, mq, yc, w_out16, head_avg, l, n_ctx, skip_ctx=last)
        xs = _mlp_call(x1, mods, w1_16, w2_16, fg, l, 0 if last else n_ctx, final=last)
    return xs
```

```python
import functools

import jax
import jax.numpy as jnp
from jax import lax
from jax.experimental import pallas as pl
from jax.experimental.pallas import tpu as pltpu

F32 = jnp.float32
BF16 = jnp.bfloat16

D_MODEL = 1024
DEPTH = 4
GRID_W = 64
N_CTX = 256
MLA_HEADS = 4
MLA_Q_RANK = 256
MLA_KV_RANK = 128
MLA_NOPE = 128
MLA_ROPE = 64
MLA_V = 128
MLA_QK = MLA_NOPE + MLA_ROPE
MLA_SCALE = MLA_QK ** -0.5
ROPE_BASE = 10000.0
ML_HEADS = 4
ML_DH = 64
ML_W = ML_HEADS * ML_DH
ML_CHUNK = 128
LRU_W = 256
LRU_BLOCKS = 4
LRU_BD = LRU_W // LRU_BLOCKS
CONV_W = 4
CONV_LEFT = 2
LRU_C = 8.0
D_FF = 4 * D_MODEL
EPS = 1e-6
MLA_IN = MLA_Q_RANK + MLA_KV_RANK + MLA_ROPE
ML_IN = 4 * ML_W + 4 * ML_HEADS
LRU_IN = 2 * LRU_W

LANES = 128
SUBLANES = 8
MOD_ROWS = 8
CTX_MOD_ROW = 4

A_W = 512
MQ_W = 4 * ML_W
MG_W = LANES
R_W = 2 * LRU_W
IN_W = A_W + MQ_W + MG_W + R_W


def _cparams(sem, vmem_mb):
    return pltpu.CompilerParams(dimension_semantics=sem, vmem_limit_bytes=vmem_mb * 1024 * 1024)


def _mod_rows(m_ref, b, row0, tm, n_ctx, seg):
    lo, hi = seg * D_MODEL, (seg + 1) * D_MODEL
    lat = m_ref[0, pl.ds(b, 1), lo:hi]
    if n_ctx == 0:
        return lat
    ctx = m_ref[0, CTX_MOD_ROW:CTX_MOD_ROW + 1, lo:hi]
    rows = row0 + lax.broadcasted_iota(jnp.int32, (tm, 1), 0)
    return jnp.where(rows < n_ctx, ctx, lat)


def _pick_tile(n, candidates):
    return next(t for t in candidates if n % t == 0)


def _rms(x):
    return x * lax.rsqrt(jnp.mean(x * x, axis=-1, keepdims=True) + EPS)


def _sigmoid(x):
    return 0.5 * jnp.tanh(0.5 * x) + 0.5


def _dot(a, b):
    return jnp.dot(a, b, preferred_element_type=F32)


def _dot_nt(a, b):
    return lax.dot_general(a, b, (((1,), (1,)), ((), ())), preferred_element_type=F32)


def _mods_body(c_ref, w_ref, b_ref, o_ref):
    cv = c_ref[...]
    act = (cv * jax.nn.sigmoid(cv)).astype(BF16)
    o_ref[0] = _dot(act, w_ref[0].astype(BF16)) + b_ref[0]


def _mods_call(cvec, w_mod, b_mod):
    depth, d, n = w_mod.shape
    tn = 1536
    return pl.pallas_call(
        _mods_body,
        grid=(depth, n // tn),
        in_specs=[pl.BlockSpec((MOD_ROWS, d), lambda l, j: (0, 0)),
                  pl.BlockSpec((1, d, tn), lambda l, j: (l, 0, j)),
                  pl.BlockSpec((1, 1, tn), lambda l, j: (l, 0, j))],
        out_specs=pl.BlockSpec((1, MOD_ROWS, tn), lambda l, j: (l, 0, j)),
        out_shape=jax.ShapeDtypeStruct((depth, MOD_ROWS, n), F32),
        compiler_params=_cparams(("arbitrary", "arbitrary"), 40),
        name="mods",
    )(cvec, w_mod, b_mod.reshape(depth, 1, n))


def _in_body(x_ref, m_ref, w_ref, a_ref, q_ref, g_ref, r_ref, *, tm, n_ctx):
    b, i = pl.program_id(0), pl.program_id(1)
    xn = _rms(x_ref[0])
    shift = _mod_rows(m_ref, b, i * tm, tm, n_ctx, 0)
    scale = _mod_rows(m_ref, b, i * tm, tm, n_ctx, 1)
    u = (xn * (1.0 + scale) + shift).astype(BF16)
    a_ref[0] = _dot(u, w_ref[0, :, 0:A_W])
    q_ref[0] = _dot(u, w_ref[0, :, A_W:A_W + MQ_W])
    g_ref[0] = _dot(u, w_ref[0, :, A_W + MQ_W:A_W + MQ_W + MG_W])
    r_ref[0] = _dot(u, w_ref[0, :, A_W + MQ_W + MG_W:IN_W])


def _in_call(x, mods, w_in_p, layer, n_ctx):
    bsz, s, d = x.shape
    tm = _pick_tile(s, (544, 256))
    row = lambda b, i: (b, i, 0)
    return pl.pallas_call(
        functools.partial(_in_body, tm=tm, n_ctx=n_ctx),
        grid=(bsz, s // tm),
        in_specs=[pl.BlockSpec((1, tm, d), row),
                  pl.BlockSpec((1, MOD_ROWS, 6 * d), lambda b, i: (layer, 0, 0)),
                  pl.BlockSpec((1, d, IN_W), lambda b, i: (layer, 0, 0))],
        out_specs=[pl.BlockSpec((1, tm, A_W), row), pl.BlockSpec((1, tm, MQ_W), row),
                   pl.BlockSpec((1, tm, MG_W), row), pl.BlockSpec((1, tm, R_W), row)],
        out_shape=[jax.ShapeDtypeStruct((bsz, s, w), F32) for w in (A_W, MQ_W, MG_W, R_W)],
        compiler_params=_cparams(("parallel", "parallel"), 40),
        name="in_proj",
    )(x, mods, w_in_p)


VT_ROWS = MLA_V + 16
LOG2E = 1.4426950408889634


def _prep_body(a_ref, gq_ref, gkv_ref, wuqt_ref, wukvk_ref, wukvvt_ref, cos_ref, sin_ref, cost_ref, sint_ref,
               qt_ref, k_ref, vt_ref):
    a = a_ref[0]
    tm = a.shape[0]
    nq, nkv = MLA_Q_RANK, MLA_Q_RANK + MLA_KV_RANK
    half = MLA_ROPE // 2
    hn = MLA_HEADS * MLA_NOPE
    cq_t = (_rms(a[:, 0:nq]) * gq_ref[0]).T.astype(BF16)
    ckv = _rms(a[:, nq:nkv]) * gkv_ref[0]
    q_t = _dot(wuqt_ref[0], cq_t) * (MLA_SCALE * LOG2E)
    x1, x2 = q_t[hn:hn + LANES], q_t[hn + LANES:hn + 2 * LANES]
    cos_t, sin_t = cost_ref[...], sint_ref[...]
    r1 = (x1 * cos_t - x2 * sin_t).astype(BF16)
    r2 = (x1 * sin_t + x2 * cos_t).astype(BF16)
    k_nope = _dot(ckv.astype(BF16), wukvk_ref[0])
    v_t = _dot(wukvvt_ref[0], ckv.T.astype(BF16))
    k1, k2 = a[:, nkv:nkv + half], a[:, nkv + half:nkv + 2 * half]
    c32, s32 = cos_ref[:, 0:half], sin_ref[:, 0:half]
    kr1 = (k1 * c32 - k2 * s32).astype(BF16)
    kr2 = (k1 * s32 + k2 * c32).astype(BF16)
    ones = jnp.ones((VT_ROWS - MLA_V, tm), BF16)
    for h in range(MLA_HEADS):
        qt_ref[0, h, 0:MLA_NOPE, :] = q_t[h * MLA_NOPE:(h + 1) * MLA_NOPE].astype(BF16)
        qt_ref[0, h, MLA_NOPE:MLA_NOPE + half, :] = r1[h * half:(h + 1) * half]
        qt_ref[0, h, MLA_NOPE + half:MLA_QK, :] = r2[h * half:(h + 1) * half]
        k_ref[0, h, :, 0:MLA_NOPE] = k_nope[:, h * MLA_NOPE:(h + 1) * MLA_NOPE].astype(BF16)
        k_ref[0, h, :, MLA_NOPE:MLA_NOPE + half] = kr1
        k_ref[0, h, :, MLA_NOPE + half:MLA_QK] = kr2
        vt_ref[0, h, 0:MLA_V, :] = v_t[h * MLA_V:(h + 1) * MLA_V].astype(BF16)
        vt_ref[0, h, MLA_V:VT_ROWS, :] = ones


def _prep_call(a, g_q, w_uq_t, g_kv, w_ukv_k, w_ukv_vt, rope, layer, n_ctx):
    bsz, s, _ = a.shape
    tm = 256
    nblk, nctx_blk = s // tm, n_ctx // tm
    lsel = lambda b, i: (layer, 0, 0)
    cos4, sin4, cos4_t, sin4_t = rope
    return pl.pallas_call(
        _prep_body,
        grid=(bsz, s // tm),
        in_specs=[pl.BlockSpec((1, tm, A_W), lambda b, i: (b, i, 0)),
                  pl.BlockSpec((1, 1, MLA_Q_RANK), lsel),
                  pl.BlockSpec((1, 1, MLA_KV_RANK), lsel),
                  pl.BlockSpec((1, MLA_HEADS * MLA_QK, MLA_Q_RANK), lsel),
                  pl.BlockSpec((1, MLA_KV_RANK, MLA_HEADS * MLA_NOPE), lsel),
                  pl.BlockSpec((1, MLA_HEADS * MLA_V, MLA_KV_RANK), lsel),
                  pl.BlockSpec((tm, LANES), lambda b, i: (i, 0)),
                  pl.BlockSpec((tm, LANES), lambda b, i: (i, 0)),
                  pl.BlockSpec((LANES, tm), lambda b, i: (0, i)),
                  pl.BlockSpec((LANES, tm), lambda b, i: (0, i))],
        out_specs=[pl.BlockSpec((1, MLA_HEADS, MLA_QK, tm), lambda b, i: (b, 0, 0, (i + nblk - nctx_blk) % nblk)),
                   pl.BlockSpec((1, MLA_HEADS, tm, MLA_QK), lambda b, i: (b, 0, i, 0)),
                   pl.BlockSpec((1, MLA_HEADS, VT_ROWS, tm), lambda b, i: (b, 0, 0, i))],
        out_shape=[jax.ShapeDtypeStruct((bsz, MLA_HEADS, MLA_QK, s), BF16),
                   jax.ShapeDtypeStruct((bsz, MLA_HEADS, s, MLA_QK), BF16),
                   jax.ShapeDtypeStruct((bsz, MLA_HEADS, VT_ROWS, s), BF16)],
        compiler_params=_cparams(("parallel", "parallel"), 40),
        name="mla_prep",
    )(a, g_q, g_kv, w_uq_t, w_ukv_k, w_ukv_vt, cos4, sin4, cos4_t, sin4_t)


def _attn_body(qt_ref, k_ref, vt_ref, o_ref, *, chunks):
    qt = qt_ref[0, 0]
    tq = qt.shape[1]
    m = jnp.full((1, tq), -jnp.inf, F32)
    acc = jnp.zeros((VT_ROWS, tq), F32)
    score = lambda start, size: _dot(k_ref[0, 0, start:start + size, :], qt)
    st_next = score(*chunks[0])
    for idx, (start, size) in enumerate(chunks):
        st = st_next
        if idx + 1 < len(chunks):
            st_next = score(*chunks[idx + 1])
        m_new = jnp.maximum(m, jnp.max(st, axis=0, keepdims=True))
        p = jnp.exp2(st - m_new).astype(BF16)
        acc = jnp.exp2(m - m_new) * acc + _dot(vt_ref[0, 0, :, start:start + size], p)
        m = m_new
    out_t = acc[0:MLA_V] / acc[MLA_V:MLA_V + 1]
    o_ref[0] = out_t.T.astype(o_ref.dtype)


ATTN_TQ = 512
ATTN_TK = 512


def _attn_latent_call(qt, k, vt, n_ctx):
    bsz, nh, s, dk = k.shape
    chunks = ((0, n_ctx),) + tuple((n_ctx + j * ATTN_TK, ATTN_TK) for j in range((s - n_ctx) // ATTN_TK))
    return pl.pallas_call(
        functools.partial(_attn_body, chunks=chunks),
        grid=(bsz, nh, (s - n_ctx) // ATTN_TQ),
        in_specs=[pl.BlockSpec((1, 1, dk, ATTN_TQ), lambda b, h, i: (b, h, 0, i)),
                  pl.BlockSpec((1, 1, s, dk), lambda b, h, i: (b, h, 0, 0)),
                  pl.BlockSpec((1, 1, VT_ROWS, s), lambda b, h, i: (b, h, 0, 0))],
        out_specs=pl.BlockSpec((1, ATTN_TQ, MLA_V), lambda b, h, i: (b, i, h)),
        out_shape=jax.ShapeDtypeStruct((bsz, s - n_ctx, nh * MLA_V), BF16),
        compiler_params=_cparams(("parallel", "parallel", "arbitrary"), 48),
        name="mla_attn",
    )(qt, k, vt)


def _attn_ctx_call(qt, k, vt, n_ctx):
    bsz, nh, s, dk = k.shape
    qblk = (s - n_ctx) // n_ctx
    return pl.pallas_call(
        functools.partial(_attn_body, chunks=((0, n_ctx),)),
        grid=(bsz, nh),
        in_specs=[pl.BlockSpec((1, 1, dk, n_ctx), lambda b, h: (b, h, 0, qblk)),
                  pl.BlockSpec((1, 1, n_ctx, dk), lambda b, h: (b, h, 0, 0)),
                  pl.BlockSpec((1, 1, VT_ROWS, n_ctx), lambda b, h: (b, h, 0, 0))],
        out_specs=pl.BlockSpec((1, n_ctx, MLA_V), lambda b, h: (b, 0, h)),
        out_shape=jax.ShapeDtypeStruct((bsz, n_ctx, nh * MLA_V), BF16),
        compiler_params=_cparams(("parallel", "parallel"), 40),
        name="mla_attn_ctx",
    )(qt, k, vt)


def _mlstm_body(xf_ref, gf_ref, xb_ref, gb_ref, bias_ref, hf_ref, hb_ref, c_ref, m_ref):
    lc = ML_CHUNK
    assert lc == LANES

    @pl.when(pl.program_id(1) == 0)
    def _():
        c_ref[...] = jnp.zeros_like(c_ref)
        m_ref[...] = jnp.zeros_like(m_ref)

    s_io = lax.broadcasted_iota(jnp.int32, (lc, lc), 0)
    t_io = lax.broadcasted_iota(jnp.int32, (lc, lc), 1)
    lane = lax.broadcasted_iota(jnp.int32, (lc, LANES), 1)
    row = lax.broadcasted_iota(jnp.int32, (LANES, lc), 0)
    row8 = lax.broadcasted_iota(jnp.int32, (SUBLANES, lc), 0)
    bias = bias_ref[...]
    ngate = 4 * ML_HEADS
    ones_sq = jnp.ones((lc, lc), BF16)

    probs = []
    c_rows = jnp.zeros((SUBLANES, lc), F32)
    for d, (x_ref, g_ref) in enumerate(((xf_ref, gf_ref), (xb_ref, gb_ref))):
        mask = (s_io <= t_io) if d == 0 else (s_io >= t_io)
        gt = (g_ref[0] + bias).T[0:ngate]
        lf = jax.nn.log_sigmoid(gt)
        hi = lf.astype(BF16)
        r1 = lf - hi.astype(F32)
        mid = r1.astype(BF16)
        lo = (r1 - mid.astype(F32)).astype(BF16)
        sums = _dot(jnp.concatenate([hi, mid, lo], axis=0),
                    jnp.concatenate([mask.astype(BF16), ones_sq], axis=1))
        sums = sums[0:ngate] + sums[ngate:2 * ngate] + sums[2 * ngate:3 * ngate]
        b_run, b_tot = sums[:, 0:lc], sums[:, lc:2 * lc]
        x = x_ref[0]
        for pair in range(ML_HEADS // 2):
            qs = x[:, pair * LANES:(pair + 1) * LANES] * (ML_DH ** -0.5)
            ks = x[:, ML_W + pair * LANES:ML_W + (pair + 1) * LANES]
            vt = x[:, 2 * ML_W + pair * LANES:2 * ML_W + (pair + 1) * LANES].T
            for odd in range(2):
                h = 2 * pair + odd
                ci, cf = d * 2 * ML_HEADS + h, d * 2 * ML_HEADS + ML_HEADS + h
                own = (lane >= ML_DH) if odd else (lane < ML_DH)
                own_r = (row >= ML_DH) if odd else (row < ML_DH)
                den_row = 0 if odd else ML_DH
                j = d * ML_HEADS + h
                brow, irow = b_run[cf:cf + 1], gt[ci:ci + 1]
                c_rows = jnp.where(row8 == j, brow - irow, c_rows)
                probs.append(dict(
                    j=j, d=d, pair=pair, odd=odd, mask=mask, den_row=den_row, brow=brow, irow=irow,
                    btot=b_tot[cf:cf + 1],
                    qm=jnp.where(own, qs, 0.0).astype(BF16), ks=ks.astype(BF16),
                    km=jnp.where(own, ks, 0.0).astype(BF16),
                    vaug=jnp.where(own_r, vt, jnp.where(row == den_row, 1.0, 0.0))))

    c_cols = jnp.concatenate([c_rows, jnp.zeros((LANES - SUBLANES, lc), F32)], axis=0).T

    for p in probs:
        p["c_old"] = c_ref[p["j"]]
        p["kq"] = _dot_nt(p["ks"], p["qm"])
        p["inter"] = _dot_nt(p["c_old"].astype(BF16), p["qm"])
    for p in probs:
        j = p["j"]
        m_prev = m_ref[j:j + 1, :]
        dt = jnp.where(p["mask"], p["brow"] - c_cols[:, j:j + 1], -jnp.inf)
        inter_m = p["brow"] + m_prev
        m_row = jnp.maximum(inter_m, jnp.max(dt, axis=0, keepdims=True))
        p["st"] = (p["kq"] * jnp.exp(dt - m_row)).astype(BF16)
        p["w_inter"] = jnp.exp(inter_m - m_row)
        p["floor"] = jnp.exp(-m_row)
        grow = p["btot"] - p["brow"] + p["irow"]
        m_new = jnp.maximum(p["btot"] + m_prev, jnp.max(grow, axis=1, keepdims=True))
        p["w_old"] = jnp.exp(p["btot"] + m_prev - m_new)
        p["wv"] = (p["vaug"] * jnp.exp(grow - m_new)).astype(BF16)
        m_ref[j:j + 1, :] = m_new
    outs = {}
    for p in probs:
        ht = _dot(p["vaug"].astype(BF16), p["st"]) + p["w_inter"] * p["inter"]
        den = ht[p["den_row"]:p["den_row"] + 1]
        outs[(p["d"], p["pair"], p["odd"])] = ht / jnp.maximum(jnp.abs(den), p["floor"])
    for d, o_ref in enumerate((hf_ref, hb_ref)):
        for pair in range(ML_HEADS // 2):
            both = jnp.where(row < ML_DH, outs[(d, pair, 0)], outs[(d, pair, 1)])
            o_ref[0, :, pair * LANES:(pair + 1) * LANES] = both.T
    for p in probs:
        c_ref[p["j"]] = p["w_old"] * p["c_old"] + _dot(p["wv"], p["km"])


def _mlstm_call(mq, mg, bias_p, n_ctx):
    bsz, s, _ = mq.shape
    nch, ncc = s // ML_CHUNK, n_ctx // ML_CHUNK
    fwd = lambda b, j: (b, j, 0)
    bwd = lambda b, j: (b, jnp.where(j < ncc, ncc - 1 - j, nch - 1 + ncc - j), 0)
    out = jax.ShapeDtypeStruct((bsz, s, ML_W), F32)
    return pl.pallas_call(
        _mlstm_body,
        grid=(bsz, nch),
        in_specs=[pl.BlockSpec((1, ML_CHUNK, MQ_W), fwd), pl.BlockSpec((1, ML_CHUNK, MG_W), fwd),
                  pl.BlockSpec((1, ML_CHUNK, MQ_W), bwd), pl.BlockSpec((1, ML_CHUNK, MG_W), bwd),
                  pl.BlockSpec((1, MG_W), lambda b, j: (0, 0))],
        out_specs=[pl.BlockSpec((1, ML_CHUNK, ML_W), fwd), pl.BlockSpec((1, ML_CHUNK, ML_W), bwd)],
        out_shape=[out, out],
        scratch_shapes=[pltpu.VMEM((2 * ML_HEADS, LANES, LANES), F32),
                        pltpu.VMEM((2 * ML_HEADS, LANES), F32)],
        compiler_params=_cparams(("parallel", "arbitrary"), 40),
        name="mlstm",
    )(mq, mg, mq, mg, bias_p)


LRU_PITCH_PAD = 8
LRU_UNROLL = 8


def _lru_body(xb_ref, gb_ref, cw_ref, cb_ref, wa_ref, ba_ref, wx_ref, bx_ref, lam_ref, y_ref,
              xp_ref, xs_ref, a_ref, u_ref, *, n_ctx, s_len):
    n_lat = s_len - n_ctx
    pad = SUBLANES
    lat_off = n_ctx + 2 * pad
    zeros = jnp.zeros((pad, LANES), F32)
    xp_ref[0:pad, :] = zeros
    xp_ref[pad + n_ctx:lat_off, :] = zeros
    xp_ref[lat_off + n_lat:lat_off + n_lat + pad, :] = zeros
    xp_ref[pad:pad + n_ctx, :] = xb_ref[0, 0:n_ctx, :]
    cchunk = 512

    def copy_body(c, _):
        src = pl.multiple_of(n_ctx + c * cchunk, SUBLANES)
        dst = pl.multiple_of(lat_off + c * cchunk, SUBLANES)
        xp_ref[pl.ds(dst, cchunk), :] = xb_ref[0, pl.ds(src, cchunk), :]
        return 0

    lax.fori_loop(0, n_lat // cchunk, copy_body, 0)

    cw = cw_ref[...]
    cb = cb_ref[...]

    def conv(src0, dst0, n):
        acc = cb + xp_ref[src0 - CONV_LEFT:src0 - CONV_LEFT + n, :] * cw[0:1, :]
        for j in range(1, CONV_W):
            acc = acc + xp_ref[src0 - CONV_LEFT + j:src0 - CONV_LEFT + j + n, :] * cw[j:j + 1, :]
        xs_ref[dst0:dst0 + n, :] = acc

    conv(pad, 0, n_ctx)
    for c in range(n_lat // cchunk):
        conv(lat_off + c * cchunk, n_ctx + c * cchunk, cchunk)

    seg_lat = n_lat // SUBLANES
    seg_ctx = n_ctx // SUBLANES
    p_lat, p_ctx = seg_lat + LRU_PITCH_PAD, seg_ctx + LRU_PITCH_PAD
    ctx_base = SUBLANES * p_lat
    row_io = lax.broadcasted_iota(jnp.int32, (SUBLANES, LANES), 0)

    def gates(x, d):
        xb16 = x.astype(BF16)
        r = _sigmoid(_dot(xb16, wa_ref[d]) + ba_ref[d])
        i = _sigmoid(_dot(xb16, wx_ref[d]) + bx_ref[d])
        log_a = (-LRU_C) * r * jax.nn.softplus(-lam_ref[d])
        a = jnp.exp(log_a)
        return a, jnp.sqrt(jnp.tanh(-log_a) * (1.0 + a * a)) * (i * x)

    def fill(d):
        def lat_body(r, _):
            src = pl.multiple_of(n_ctx + r * seg_lat, SUBLANES)
            dst = pl.multiple_of(r * p_lat, SUBLANES)
            a, u = gates(xs_ref[pl.ds(src, seg_lat), :], d)
            a_ref[d, pl.ds(dst, seg_lat), :] = a
            u_ref[d, pl.ds(dst, seg_lat), :] = u
            return 0

        lax.fori_loop(0, SUBLANES, lat_body, 0)
        a, u = gates(xs_ref[0:n_ctx, :], d)
        for r in range(SUBLANES):
            a_ref[d, ctx_base + r * p_ctx:ctx_base + r * p_ctx + seg_ctx, :] = a[r * seg_ctx:(r + 1) * seg_ctx, :]
            u_ref[d, ctx_base + r * p_ctx:ctx_base + r * p_ctx + seg_ctx, :] = u[r * seg_ctx:(r + 1) * seg_ctx, :]

    def scan(base, n, pitch, h0s):
        def block(tb, carry):
            idx = [[pl.ds(base + (tb * LRU_UNROLL + k if d == 0 else n - 1 - tb * LRU_UNROLL - k),
                          SUBLANES, stride=pitch) for k in range(LRU_UNROLL)] for d in range(2)]
            av = [[a_ref[d, i, :] for i in idx[d]] for d in range(2)]
            uv = [[u_ref[d, i, :] for i in idx[d]] for d in range(2)]
            carry = list(carry)
            for k in range(LRU_UNROLL):
                for d in range(2):
                    h, acum = carry[d]
                    h = av[d][k] * h + uv[d][k]
                    acum = acum * av[d][k]
                    carry[d] = (h, acum)
                    uv[d][k], av[d][k] = h, acum
            for d in range(2):
                for k in range(LRU_UNROLL):
                    u_ref[d, idx[d][k], :] = uv[d][k]
                    a_ref[d, idx[d][k], :] = av[d][k]
            return tuple(carry)

        init = (jnp.zeros((SUBLANES, LANES), F32), jnp.ones((SUBLANES, LANES), F32))
        ends = lax.fori_loop(0, n // LRU_UNROLL, block, (init, init))
        result = []
        for d in range(2):
            h_end, a_end = ends[d]
            carry = h0s[d]
            cvec = jnp.zeros((SUBLANES, LANES), F32)
            for r in (range(SUBLANES) if d == 0 else range(SUBLANES - 1, -1, -1)):
                cvec = jnp.where(row_io == r, carry, cvec)
                carry = h_end[r:r + 1, :] + a_end[r:r + 1, :] * carry
            result.append((cvec, carry))
        return result

    fill(0)
    fill(1)
    zero_state = jnp.zeros((1, LANES), F32)
    ctx_res = scan(ctx_base, seg_ctx, p_ctx, (zero_state, zero_state))
    lat_res = scan(0, seg_lat, p_lat, (ctx_res[0][1], ctx_res[1][1]))
    carries = {(d, "ctx"): ctx_res[d][0] for d in range(2)}
    carries.update({(d, "lat"): lat_res[d][0] for d in range(2)})

    def emit(kind, r, dst0, src0, n):
        hsum = None
        for d in range(2):
            c_in = carries[d, kind][r:r + 1, :]
            part = u_ref[d, src0:src0 + n, :] + a_ref[d, src0:src0 + n, :] * c_in
            hsum = part if hsum is None else hsum + part
        y_ref[0, dst0:dst0 + n, :] = jax.nn.gelu(gb_ref[0, dst0:dst0 + n, :]) * hsum

    for r in range(SUBLANES):
        emit("lat", r, n_ctx + r * seg_lat, r * p_lat, seg_lat)
        emit("ctx", r, r * seg_ctx, ctx_base + r * p_ctx, seg_ctx)


def _lru_call(r, conv_w, conv_b, wa_bd, b_a, wx_bd, b_x, lam, layer, n_ctx):
    bsz, s, _ = r.shape
    nh = LRU_W // LANES
    n_lat = s - n_ctx
    scan_rows = SUBLANES * (n_lat // SUBLANES + LRU_PITCH_PAD) + SUBLANES * (n_ctx // SUBLANES + LRU_PITCH_PAD)
    vec = lambda b, c: (layer, 0, 0, c)
    return pl.pallas_call(
        functools.partial(_lru_body, n_ctx=n_ctx, s_len=s),
        grid=(bsz, nh),
        in_specs=[pl.BlockSpec((1, s, LANES), lambda b, c: (b, 0, c)),
                  pl.BlockSpec((1, s, LANES), lambda b, c: (b, 0, nh + c)),
                  pl.BlockSpec((None, CONV_W, LANES), lambda b, c: (layer, 0, c)),
                  pl.BlockSpec((None, 1, LANES), lambda b, c: (layer, 0, c)),
                  pl.BlockSpec((None, None, 2, LANES, LANES), lambda b, c: (layer, c, 0, 0, 0)),
                  pl.BlockSpec((None, 2, 1, LANES), vec),
                  pl.BlockSpec((None, None, 2, LANES, LANES), lambda b, c: (layer, c, 0, 0, 0)),
                  pl.BlockSpec((None, 2, 1, LANES), vec),
                  pl.BlockSpec((None, 2, 1, LANES), vec)],
        out_specs=pl.BlockSpec((1, s, LANES), lambda b, c: (b, 0, c)),
        out_shape=jax.ShapeDtypeStruct((bsz, s, LRU_W), F32),
        scratch_shapes=[pltpu.VMEM((s + 3 * SUBLANES, LANES), F32),
                        pltpu.VMEM((s, LANES), F32),
                        pltpu.VMEM((2, scan_rows, LANES), F32),
                        pltpu.VMEM((2, scan_rows, LANES), F32)],
        compiler_params=_cparams(("parallel", "parallel"), 48),
        name="rglru",
    )(r, r, conv_w, conv_b, wa_bd, b_a, wx_bd, b_x, lam)


def _out_body(x_ref, m_ref, yal_ref, yac_ref, hf_ref, hb_ref, og_ref, yc_ref, w_ref, hn_ref, o_ref,
              *, tm, n_ctx, blk0):
    b, i = pl.program_id(0), pl.program_id(1)
    hsum = hf_ref[0] + hb_ref[0]
    msq = jnp.dot(hsum * hsum, hn_ref[...], precision=lax.Precision.HIGHEST, preferred_element_type=F32)
    yb = (jax.nn.sigmoid(og_ref[0]) * (hsum * lax.rsqrt(msq + EPS))).astype(BF16)
    na, nb = MLA_HEADS * MLA_V, MLA_HEADS * MLA_V + ML_W
    ya = yal_ref[0]
    if blk0 == 0:
        ya = jnp.where((i + blk0) * tm < n_ctx, yac_ref[0], ya)
    y = (_dot(ya, w_ref[0, 0:na, :]) + _dot(yb, w_ref[0, na:nb, :])
         + _dot(yc_ref[0].astype(BF16), w_ref[0, nb:, :]))
    gate = _mod_rows(m_ref, b, (i + blk0) * tm, tm, n_ctx, 2)
    o_ref[0] = x_ref[0] + gate * y


def _out_call(x, mods, ya_lat, ya_ctx, hf, hb, mq, yc, w_out, head_avg, layer, n_ctx, skip_ctx):
    bsz, s, d = x.shape
    tm = n_ctx
    nctx_blk = n_ctx // tm
    blk0 = nctx_blk if skip_ctx else 0
    rows_out = s - blk0 * tm
    row = lambda b, i: (b, i + blk0, 0)
    return pl.pallas_call(
        functools.partial(_out_body, tm=tm, n_ctx=n_ctx, blk0=blk0),
        grid=(bsz, rows_out // tm),
        in_specs=[pl.BlockSpec((1, tm, d), row),
                  pl.BlockSpec((1, MOD_ROWS, 6 * d), lambda b, i: (layer, 0, 0)),
                  pl.BlockSpec((1, tm, MLA_HEADS * MLA_V), lambda b, i: (b, jnp.maximum(i + blk0 - nctx_blk, 0), 0)),
                  pl.BlockSpec((1, tm, MLA_HEADS * MLA_V), lambda b, i: (b, 0, 0)),
                  pl.BlockSpec((1, tm, ML_W), row),
                  pl.BlockSpec((1, tm, ML_W), row),
                  pl.BlockSpec((1, tm, ML_W), lambda b, i: (b, i + blk0, 3)),
                  pl.BlockSpec((1, tm, LRU_W), row),
                  pl.BlockSpec((1, d, d), lambda b, i: (layer, 0, 0)),
                  pl.BlockSpec((ML_W, ML_W), lambda b, i: (0, 0))],
        out_specs=pl.BlockSpec((1, tm, d), lambda b, i: (b, i, 0)),
        out_shape=jax.ShapeDtypeStruct((bsz, rows_out, d), F32),
        compiler_params=_cparams(("parallel", "parallel"), 40),
        name="out_proj",
    )(x, mods, ya_lat, ya_lat if ya_ctx is None else ya_ctx, hf, hb, mq, yc, w_out, head_avg)


def _mlp_body(x_ref, m_ref, w1_ref, w2_ref, fg_ref, o_ref, u_ref, acc_ref, *, tm, n_ctx, final):
    b, i, k = pl.program_id(0), pl.program_id(1), pl.program_id(2)

    @pl.when(k == 0)
    def _():
        shift = _mod_rows(m_ref, b, i * tm, tm, n_ctx, 3)
        scale = _mod_rows(m_ref, b, i * tm, tm, n_ctx, 4)
        u_ref[...] = (_rms(x_ref[0]) * (1.0 + scale) + shift).astype(BF16)
        acc_ref[...] = jnp.zeros_like(acc_ref)

    hid = jnp.maximum(_dot(u_ref[...], w1_ref[0]), 0.0)
    acc_ref[...] += _dot((hid * hid).astype(BF16), w2_ref[0])

    @pl.when(k == pl.num_programs(2) - 1)
    def _():
        gate = _mod_rows(m_ref, b, i * tm, tm, n_ctx, 5)
        res = x_ref[0] + gate * acc_ref[...]
        if final:
            res = _rms(res) * fg_ref[...]
        o_ref[0] = res


def _mlp_call(x, mods, w1, w2, final_g, layer, n_ctx, final):
    bsz, s, d = x.shape
    tm = _pick_tile(s, (1088, 1024, 256))
    fc = 512
    return pl.pallas_call(
        functools.partial(_mlp_body, tm=tm, n_ctx=n_ctx, final=final),
        grid=(bsz, s // tm, D_FF // fc),
        in_specs=[pl.BlockSpec((1, tm, d), lambda b, i, k: (b, i, 0)),
                  pl.BlockSpec((1, MOD_ROWS, 6 * d), lambda b, i, k: (layer, 0, 0)),
                  pl.BlockSpec((1, d, fc), lambda b, i, k: (layer, 0, k)),
                  pl.BlockSpec((1, fc, d), lambda b, i, k: (layer, k, 0)),
                  pl.BlockSpec((1, d), lambda b, i, k: (0, 0))],
        out_specs=pl.BlockSpec((1, tm, d), lambda b, i, k: (b, i, 0)),
        out_shape=jax.ShapeDtypeStruct((bsz, s, d), F32),
        scratch_shapes=[pltpu.VMEM((tm, d), BF16), pltpu.VMEM((tm, d), F32)],
        compiler_params=_cparams(("parallel", "parallel", "arbitrary"), 48),
        name="mlp",
    )(x, mods, w1, w2, final_g)


def _rope_tables(seq, n_ctx):
    half = MLA_ROPE // 2
    row = jnp.repeat(jnp.arange(seq // GRID_W), GRID_W).astype(F32)
    col = jnp.tile(jnp.arange(GRID_W), seq // GRID_W).astype(F32)
    freqs = 1.0 / (ROPE_BASE ** (jnp.arange(0, half, 2, dtype=F32) / half))
    ang = jnp.concatenate([row[:, None] * freqs, col[:, None] * freqs], axis=-1)
    cos = jnp.concatenate([jnp.ones((n_ctx, half), F32), jnp.cos(ang)], axis=0)
    sin = jnp.concatenate([jnp.zeros((n_ctx, half), F32), jnp.sin(ang)], axis=0)
    return jnp.tile(cos, (1, MLA_HEADS)), jnp.tile(sin, (1, MLA_HEADS))


def _block_diag_halves(w):
    depth = w.shape[0]
    per = LANES // LRU_BD
    wh = w.reshape(depth, 2, LRU_BLOCKS // per, per, LRU_BD, LRU_BD)
    eye = jnp.eye(per, dtype=w.dtype)
    bd = jnp.einsum("ldcpio,pq->ldcpiqo", wh, eye).reshape(depth, 2, LRU_BLOCKS // per, LANES, LANES)
    return bd.transpose(0, 2, 1, 3, 4).astype(BF16)


def kernel(x, c, ctx, c_ctx, w_mod, b_mod, w_in, mla_g_q, mla_w_uq, mla_g_kv, mla_w_ukv, ml_gate_bias,
           lru_conv_w, lru_conv_b, lru_w_a, lru_b_a, lru_w_x, lru_b_x, lru_lam, w_out, w_ff1, w_ff2, final_g):
    bsz, seq, d = x.shape
    n_ctx = ctx.shape[1]
    depth = w_in.shape[0]
    assert bsz <= CTX_MOD_ROW and d == D_MODEL and n_ctx % 256 == 0

    zc = lambda n: jnp.zeros((depth, d, n), w_in.dtype)
    ml0, ml1 = MLA_IN, MLA_IN + 4 * ML_W
    w_in_p = jnp.concatenate([w_in[:, :, :ml0], zc(A_W - MLA_IN), w_in[:, :, ml0:ml1],
                              w_in[:, :, ml1:ml1 + 4 * ML_HEADS], zc(MG_W - 4 * ML_HEADS),
                              w_in[:, :, ml1 + 4 * ML_HEADS:]], axis=-1).astype(BF16)
    half = MLA_ROPE // 2
    uq = mla_w_uq.reshape(depth, MLA_Q_RANK, MLA_HEADS, MLA_QK)
    w_uq_t = jnp.concatenate([uq[..., :MLA_NOPE].reshape(depth, MLA_Q_RANK, -1),
                              uq[..., MLA_NOPE:MLA_NOPE + half].reshape(depth, MLA_Q_RANK, -1),
                              uq[..., MLA_NOPE + half:].reshape(depth, MLA_Q_RANK, -1)],
                             axis=-1).astype(BF16).transpose(0, 2, 1)
    ukv = mla_w_ukv.reshape(depth, MLA_KV_RANK, MLA_HEADS, MLA_NOPE + MLA_V)
    w_ukv_k = ukv[..., :MLA_NOPE].reshape(depth, MLA_KV_RANK, -1).astype(BF16)
    w_ukv_vt = ukv[..., MLA_NOPE:].reshape(depth, MLA_KV_RANK, -1).astype(BF16).transpose(0, 2, 1)
    g_q = mla_g_q.reshape(depth, 1, MLA_Q_RANK)
    g_kv = mla_g_kv.reshape(depth, 1, MLA_KV_RANK)
    bias_p = jnp.pad(ml_gate_bias, ((0, 0), (0, MG_W - 4 * ML_HEADS)))
    wa_bd, wx_bd = _block_diag_halves(lru_w_a), _block_diag_halves(lru_w_x)
    vec4 = lambda v: v.reshape(depth, 2, 1, LRU_W)
    conv_b = lru_conv_b.reshape(depth, 1, LRU_W)
    w_out16, w1_16, w2_16 = w_out.astype(BF16), w_ff1.astype(BF16), w_ff2.astype(BF16)
    head_avg = jnp.kron(jnp.eye(ML_HEADS, dtype=F32), jnp.full((ML_DH, ML_DH), 1.0 / ML_DH, F32))
    cos4, sin4 = _rope_tables(seq, n_ctx)
    rope = (cos4, sin4, cos4.T, sin4.T)
    fg = final_g.reshape(1, d)

    cvec = jnp.concatenate([c, jnp.zeros((CTX_MOD_ROW - bsz, d), c.dtype), c_ctx[None, :],
                            jnp.zeros((MOD_ROWS - CTX_MOD_ROW - 1, d), c.dtype)], axis=0)
    mods = _mods_call(cvec, w_mod, b_mod)

    xs = jnp.concatenate([ctx, x], axis=1)
    for l in range(depth):
        last = l == depth - 1
        a, mq, mg, r = _in_call(xs, mods, w_in_p, l, n_ctx)
        qt, k, vt = _prep_call(a, g_q, w_uq_t, g_kv, w_ukv_k, w_ukv_vt, rope, l, n_ctx)
        ya_lat = _attn_latent_call(qt, k, vt, n_ctx)
        ya_ctx = None if last else _attn_ctx_call(qt, k, vt, n_ctx)
        hf, hb = _mlstm_call(mq, mg, bias_p[l:l + 1], n_ctx)
        yc = _lru_call(r, lru_conv_w, conv_b, wa_bd, vec4(lru_b_a), wx_bd, vec4(lru_b_x), vec4(lru_lam), l, n_ctx)
        x1 = _out_call(xs, mods, ya_lat, ya_ctx, hf, hb, mq, yc, w_out16, head_avg, l, n_ctx, skip_ctx=last)
        xs = _mlp_call(x1, mods, w1_16, w2_16, fg, l, 0 if last else n_ctx, final=last)
    return xs
```

```python
import functools

import jax
import jax.numpy as jnp
from jax import lax
from jax.experimental import pallas as pl
from jax.experimental.pallas import tpu as pltpu

F32 = jnp.float32
BF16 = jnp.bfloat16

D_MODEL = 1024
DEPTH = 4
GRID_W = 64
N_CTX = 256
MLA_HEADS = 4
MLA_Q_RANK = 256
MLA_KV_RANK = 128
MLA_NOPE = 128
MLA_ROPE = 64
MLA_V = 128
MLA_QK = MLA_NOPE + MLA_ROPE
MLA_SCALE = MLA_QK ** -0.5
ROPE_BASE = 10000.0
ML_HEADS = 4
ML_DH = 64
ML_W = ML_HEADS * ML_DH
ML_CHUNK = 128
LRU_W = 256
LRU_BLOCKS = 4
LRU_BD = LRU_W // LRU_BLOCKS
CONV_W = 4
CONV_LEFT = 2
LRU_C = 8.0
D_FF = 4 * D_MODEL
EPS = 1e-6
MLA_IN = MLA_Q_RANK + MLA_KV_RANK + MLA_ROPE
ML_IN = 4 * ML_W + 4 * ML_HEADS
LRU_IN = 2 * LRU_W

LANES = 128
SUBLANES = 8
MOD_ROWS = 8
CTX_MOD_ROW = 4

A_W = 512
MQ_W = 4 * ML_W
MG_W = LANES
R_W = 2 * LRU_W
IN_W = A_W + MQ_W + MG_W + R_W


def _cparams(sem, vmem_mb):
    return pltpu.CompilerParams(dimension_semantics=sem, vmem_limit_bytes=vmem_mb * 1024 * 1024)


def _mod_rows(m_ref, b, row0, tm, n_ctx, seg):
    lo, hi = seg * D_MODEL, (seg + 1) * D_MODEL
    lat = m_ref[0, pl.ds(b, 1), lo:hi]
    if n_ctx == 0:
        return lat
    ctx = m_ref[0, CTX_MOD_ROW:CTX_MOD_ROW + 1, lo:hi]
    rows = row0 + lax.broadcasted_iota(jnp.int32, (tm, 1), 0)
    return jnp.where(rows < n_ctx, ctx, lat)


def _pick_tile(n, candidates):
    return next(t for t in candidates if n % t == 0)


def _rms(x):
    return x * lax.rsqrt(jnp.mean(x * x, axis=-1, keepdims=True) + EPS)


def _sigmoid(x):
    return 0.5 * jnp.tanh(0.5 * x) + 0.5


def _dot(a, b):
    return jnp.dot(a, b, preferred_element_type=F32)


def _dot_nt(a, b):
    return lax.dot_general(a, b, (((1,), (1,)), ((), ())), preferred_element_type=F32)


def _mods_body(c_ref, w_ref, b_ref, o_ref):
    cv = c_ref[...]
    act = (cv * jax.nn.sigmoid(cv)).astype(BF16)
    o_ref[0] = _dot(act, w_ref[0].astype(BF16)) + b_ref[0]


def _mods_call(cvec, w_mod, b_mod):
    depth, d, n = w_mod.shape
    tn = 1536
    return pl.pallas_call(
        _mods_body,
        grid=(depth, n // tn),
        in_specs=[pl.BlockSpec((MOD_ROWS, d), lambda l, j: (0, 0)),
                  pl.BlockSpec((1, d, tn), lambda l, j: (l, 0, j)),
                  pl.BlockSpec((1, 1, tn), lambda l, j: (l, 0, j))],
        out_specs=pl.BlockSpec((1, MOD_ROWS, tn), lambda l, j: (l, 0, j)),
        out_shape=jax.ShapeDtypeStruct((depth, MOD_ROWS, n), F32),
        compiler_params=_cparams(("arbitrary", "arbitrary"), 40),
        name="mods",
    )(cvec, w_mod, b_mod.reshape(depth, 1, n))


def _in_body(x_ref, m_ref, w_ref, a_ref, q_ref, g_ref, r_ref, *, tm, n_ctx):
    b, i = pl.program_id(0), pl.program_id(1)
    xn = _rms(x_ref[0])
    shift = _mod_rows(m_ref, b, i * tm, tm, n_ctx, 0)
    scale = _mod_rows(m_ref, b, i * tm, tm, n_ctx, 1)
    u = (xn * (1.0 + scale) + shift).astype(BF16)
    a_ref[0] = _dot(u, w_ref[0, :, 0:A_W])
    q_ref[0] = _dot(u, w_ref[0, :, A_W:A_W + MQ_W])
    g_ref[0] = _dot(u, w_ref[0, :, A_W + MQ_W:A_W + MQ_W + MG_W])
    r_ref[0] = _dot(u, w_ref[0, :, A_W + MQ_W + MG_W:IN_W])


def _in_call(x, mods, w_in_p, layer, n_ctx):
    bsz, s, d = x.shape
    tm = _pick_tile(s, (544, 256))
    row = lambda b, i: (b, i, 0)
    return pl.pallas_call(
        functools.partial(_in_body, tm=tm, n_ctx=n_ctx),
        grid=(bsz, s // tm),
        in_specs=[pl.BlockSpec((1, tm, d), row),
                  pl.BlockSpec((1, MOD_ROWS, 6 * d), lambda b, i: (layer, 0, 0)),
                  pl.BlockSpec((1, d, IN_W), lambda b, i: (layer, 0, 0))],
        out_specs=[pl.BlockSpec((1, tm, A_W), row), pl.BlockSpec((1, tm, MQ_W), row),
                   pl.BlockSpec((1, tm, MG_W), row), pl.BlockSpec((1, tm, R_W), row)],
        out_shape=[jax.ShapeDtypeStruct((bsz, s, w), F32) for w in (A_W, MQ_W, MG_W, R_W)],
        compiler_params=_cparams(("parallel", "parallel"), 40),
        name="in_proj",
    )(x, mods, w_in_p)


VT_ROWS = MLA_V + 16
LOG2E = 1.4426950408889634


def _prep_body(a_ref, gq_ref, gkv_ref, wuqt_ref, wukvk_ref, wukvvt_ref, cos_ref, sin_ref, cost_ref, sint_ref,
               qt_ref, k_ref, vt_ref):
    a = a_ref[0]
    tm = a.shape[0]
    nq, nkv = MLA_Q_RANK, MLA_Q_RANK + MLA_KV_RANK
    half = MLA_ROPE // 2
    hn = MLA_HEADS * MLA_NOPE
    cq_t = (_rms(a[:, 0:nq]) * gq_ref[0]).T.astype(BF16)
    ckv = _rms(a[:, nq:nkv]) * gkv_ref[0]
    q_t = _dot(wuqt_ref[0], cq_t) * (MLA_SCALE * LOG2E)
    x1, x2 = q_t[hn:hn + LANES], q_t[hn + LANES:hn + 2 * LANES]
    cos_t, sin_t = cost_ref[...], sint_ref[...]
    r1 = (x1 * cos_t - x2 * sin_t).astype(BF16)
    r2 = (x1 * sin_t + x2 * cos_t).astype(BF16)
    k_nope = _dot(ckv.astype(BF16), wukvk_ref[0])
    v_t = _dot(wukvvt_ref[0], ckv.T.astype(BF16))
    k1, k2 = a[:, nkv:nkv + half], a[:, nkv + half:nkv + 2 * half]
    c32, s32 = cos_ref[:, 0:half], sin_ref[:, 0:half]
    kr1 = (k1 * c32 - k2 * s32).astype(BF16)
    kr2 = (k1 * s32 + k2 * c32).astype(BF16)
    ones = jnp.ones((VT_ROWS - MLA_V, tm), BF16)
    for h in range(MLA_HEADS):
        qt_ref[0, h, 0:MLA_NOPE, :] = q_t[h * MLA_NOPE:(h + 1) * MLA_NOPE].astype(BF16)
        qt_ref[0, h, MLA_NOPE:MLA_NOPE + half, :] = r1[h * half:(h + 1) * half]
        qt_ref[0, h, MLA_NOPE + half:MLA_QK, :] = r2[h * half:(h + 1) * half]
        k_ref[0, h, :, 0:MLA_NOPE] = k_nope[:, h * MLA_NOPE:(h + 1) * MLA_NOPE].astype(BF16)
        k_ref[0, h, :, MLA_NOPE:MLA_NOPE + half] = kr1
        k_ref[0, h, :, MLA_NOPE + half:MLA_QK] = kr2
        vt_ref[0, h, 0:MLA_V, :] = v_t[h * MLA_V:(h + 1) * MLA_V].astype(BF16)
        vt_ref[0, h, MLA_V:VT_ROWS, :] = ones


def _prep_call(a, g_q, w_uq_t, g_kv, w_ukv_k, w_ukv_vt, rope, layer, n_ctx):
    bsz, s, _ = a.shape
    tm = 256
    nblk, nctx_blk = s // tm, n_ctx // tm
    lsel = lambda b, i: (layer, 0, 0)
    cos4, sin4, cos4_t, sin4_t = rope
    return pl.pallas_call(
        _prep_body,
        grid=(bsz, s // tm),
        in_specs=[pl.BlockSpec((1, tm, A_W), lambda b, i: (b, i, 0)),
                  pl.BlockSpec((1, 1, MLA_Q_RANK), lsel),
                  pl.BlockSpec((1, 1, MLA_KV_RANK), lsel),
                  pl.BlockSpec((1, MLA_HEADS * MLA_QK, MLA_Q_RANK), lsel),
                  pl.BlockSpec((1, MLA_KV_RANK, MLA_HEADS * MLA_NOPE), lsel),
                  pl.BlockSpec((1, MLA_HEADS * MLA_V, MLA_KV_RANK), lsel),
                  pl.BlockSpec((tm, LANES), lambda b, i: (i, 0)),
                  pl.BlockSpec((tm, LANES), lambda b, i: (i, 0)),
                  pl.BlockSpec((LANES, tm), lambda b, i: (0, i)),
                  pl.BlockSpec((LANES, tm), lambda b, i: (0, i))],
        out_specs=[pl.BlockSpec((1, MLA_HEADS, MLA_QK, tm), lambda b, i: (b, 0, 0, (i + nblk - nctx_blk) % nblk)),
                   pl.BlockSpec((1, MLA_HEADS, tm, MLA_QK), lambda b, i: (b, 0, i, 0)),
                   pl.BlockSpec((1, MLA_HEADS, VT_ROWS, tm), lambda b, i: (b, 0, 0, i))],
        out_shape=[jax.ShapeDtypeStruct((bsz, MLA_HEADS, MLA_QK, s), BF16),
                   jax.ShapeDtypeStruct((bsz, MLA_HEADS, s, MLA_QK), BF16),
                   jax.ShapeDtypeStruct((bsz, MLA_HEADS, VT_ROWS, s), BF16)],
        compiler_params=_cparams(("parallel", "parallel"), 40),
        name="mla_prep",
    )(a, g_q, g_kv, w_uq_t, w_ukv_k, w_ukv_vt, cos4, sin4, cos4_t, sin4_t)


def _attn_body(qt_ref, k_ref, vt_ref, o_ref, *, chunks, tq):
    nsub = qt_ref.shape[3] // tq
    qts = [qt_ref[0, 0, :, b * tq:(b + 1) * tq] for b in range(nsub)]
    work = [(b, c) for b in range(nsub) for c in chunks]
    score = lambda b, c: _dot(k_ref[0, 0, c[0]:c[0] + c[1], :], qts[b])
    pending = [score(*w) for w in work[:ATTN_AHEAD]]
    m = acc = None
    for idx, (b, (start, size)) in enumerate(work):
        if (start, size) == chunks[0]:
            m = jnp.full((1, tq), -jnp.inf, F32)
            acc = jnp.zeros((VT_ROWS, tq), F32)
        st = pending.pop(0)
        if idx + ATTN_AHEAD < len(work):
            pending.append(score(*work[idx + ATTN_AHEAD]))
        m_new = jnp.maximum(m, jnp.max(st, axis=0, keepdims=True))
        p = jnp.exp2(st - m_new).astype(BF16)
        acc = jnp.exp2(m - m_new) * acc + _dot(vt_ref[0, 0, :, start:start + size], p)
        m = m_new
        if (start, size) == chunks[-1]:
            out_t = acc[0:MLA_V] / acc[MLA_V:MLA_V + 1]
            o_ref[0, b * tq:(b + 1) * tq, :] = out_t.T.astype(o_ref.dtype)


ATTN_TQ = 512
ATTN_SUB = 4
ATTN_TK = 512
ATTN_AHEAD = 2


def _attn_latent_call(qt, k, vt, n_ctx):
    bsz, nh, s, dk = k.shape
    chunks = ((0, n_ctx),) + tuple((n_ctx + j * ATTN_TK, ATTN_TK) for j in range((s - n_ctx) // ATTN_TK))
    tqb = ATTN_TQ * ATTN_SUB
    assert (s - n_ctx) % tqb == 0 and (s - n_ctx) % ATTN_TK == 0
    return pl.pallas_call(
        functools.partial(_attn_body, chunks=chunks, tq=ATTN_TQ),
        grid=(bsz, nh, (s - n_ctx) // tqb),
        in_specs=[pl.BlockSpec((1, 1, dk, tqb), lambda b, h, i: (b, h, 0, i)),
                  pl.BlockSpec((1, 1, s, dk), lambda b, h, i: (b, h, 0, 0)),
                  pl.BlockSpec((1, 1, VT_ROWS, s), lambda b, h, i: (b, h, 0, 0))],
        out_specs=pl.BlockSpec((1, tqb, MLA_V), lambda b, h, i: (b, i, h)),
        out_shape=jax.ShapeDtypeStruct((bsz, s - n_ctx, nh * MLA_V), BF16),
        compiler_params=_cparams(("parallel", "parallel", "arbitrary"), 48),
        name="mla_attn",
    )(qt, k, vt)


def _attn_ctx_call(qt, k, vt, n_ctx):
    bsz, nh, s, dk = k.shape
    qblk = (s - n_ctx) // n_ctx
    return pl.pallas_call(
        functools.partial(_attn_body, chunks=((0, n_ctx),), tq=n_ctx),
        grid=(bsz, nh),
        in_specs=[pl.BlockSpec((1, 1, dk, n_ctx), lambda b, h: (b, h, 0, qblk)),
                  pl.BlockSpec((1, 1, n_ctx, dk), lambda b, h: (b, h, 0, 0)),
                  pl.BlockSpec((1, 1, VT_ROWS, n_ctx), lambda b, h: (b, h, 0, 0))],
        out_specs=pl.BlockSpec((1, n_ctx, MLA_V), lambda b, h: (b, 0, h)),
        out_shape=jax.ShapeDtypeStruct((bsz, n_ctx, nh * MLA_V), BF16),
        compiler_params=_cparams(("parallel", "parallel"), 40),
        name="mla_attn_ctx",
    )(qt, k, vt)


ML_BLOCK_CHUNKS = 2


def _mlstm_body(xf_ref, gf_ref, xb_ref, gb_ref, bias_ref, hf_ref, hb_ref, c_ref, m_ref):
    lc = ML_CHUNK
    assert lc == LANES

    @pl.when(pl.program_id(1) == 0)
    def _():
        c_ref[...] = jnp.zeros_like(c_ref)
        m_ref[...] = jnp.zeros_like(m_ref)

    s_io = lax.broadcasted_iota(jnp.int32, (lc, lc), 0)
    t_io = lax.broadcasted_iota(jnp.int32, (lc, lc), 1)
    lane = lax.broadcasted_iota(jnp.int32, (lc, LANES), 1)
    row = lax.broadcasted_iota(jnp.int32, (LANES, lc), 0)
    row8 = lax.broadcasted_iota(jnp.int32, (SUBLANES, lc), 0)
    bias = bias_ref[...]
    ngate = 4 * ML_HEADS
    ones_sq = jnp.ones((lc, lc), BF16)
    nsub = xf_ref.shape[1] // lc
    for t in range(nsub):
        _mlstm_chunk((xf_ref, xb_ref), (gf_ref, gb_ref), (hf_ref, hb_ref), c_ref, m_ref,
                     (t * lc, (nsub - 1 - t) * lc), bias, ones_sq, (s_io, t_io, lane, row, row8))


def _mlstm_chunk(x_refs, g_refs, o_refs, c_ref, m_ref, row0, bias, ones_sq, iotas):
    lc = ML_CHUNK
    ngate = 4 * ML_HEADS
    s_io, t_io, lane, row, row8 = iotas
    probs = []
    c_rows = jnp.zeros((SUBLANES, lc), F32)
    for d in range(2):
        x_ref, g_ref = x_refs[d], g_refs[d]
        rows = slice(row0[d], row0[d] + lc)
        mask = (s_io <= t_io) if d == 0 else (s_io >= t_io)
        gt = (g_ref[0, rows, :] + bias).T[0:ngate]
        lf = jax.nn.log_sigmoid(gt)
        hi = lf.astype(BF16)
        r1 = lf - hi.astype(F32)
        mid = r1.astype(BF16)
        lo = (r1 - mid.astype(F32)).astype(BF16)
        sums = _dot(jnp.concatenate([hi, mid, lo], axis=0),
                    jnp.concatenate([mask.astype(BF16), ones_sq], axis=1))
        sums = sums[0:ngate] + sums[ngate:2 * ngate] + sums[2 * ngate:3 * ngate]
        b_run, b_tot = sums[:, 0:lc], sums[:, lc:2 * lc]
        x = x_ref[0, rows, :]
        for pair in range(ML_HEADS // 2):
            qs = x[:, pair * LANES:(pair + 1) * LANES] * (ML_DH ** -0.5)
            ks = x[:, ML_W + pair * LANES:ML_W + (pair + 1) * LANES]
            vt = x[:, 2 * ML_W + pair * LANES:2 * ML_W + (pair + 1) * LANES].T
            for odd in range(2):
                h = 2 * pair + odd
                ci, cf = d * 2 * ML_HEADS + h, d * 2 * ML_HEADS + ML_HEADS + h
                own = (lane >= ML_DH) if odd else (lane < ML_DH)
                own_r = (row >= ML_DH) if odd else (row < ML_DH)
                den_row = 0 if odd else ML_DH
                j = d * ML_HEADS + h
                brow, irow = b_run[cf:cf + 1], gt[ci:ci + 1]
                c_rows = jnp.where(row8 == j, brow - irow, c_rows)
                probs.append(dict(
                    j=j, d=d, pair=pair, odd=odd, mask=mask, den_row=den_row, brow=brow, irow=irow,
                    btot=b_tot[cf:cf + 1],
                    qm=jnp.where(own, qs, 0.0).astype(BF16), ks=ks.astype(BF16),
                    km=jnp.where(own, ks, 0.0).astype(BF16),
                    vaug=jnp.where(own_r, vt, jnp.where(row == den_row, 1.0, 0.0))))

    c_cols = jnp.concatenate([c_rows, jnp.zeros((LANES - SUBLANES, lc), F32)], axis=0).T

    for p in probs:
        p["c_old"] = c_ref[p["j"]]
        p["kq"] = _dot_nt(p["ks"], p["qm"])
        p["inter"] = _dot_nt(p["c_old"].astype(BF16), p["qm"])
    for p in probs:
        j = p["j"]
        m_prev = m_ref[j:j + 1, :]
        dt = jnp.where(p["mask"], p["brow"] - c_cols[:, j:j + 1], -jnp.inf)
        inter_m = p["brow"] + m_prev
        m_row = jnp.maximum(inter_m, jnp.max(dt, axis=0, keepdims=True))
        p["st"] = (p["kq"] * jnp.exp(dt - m_row)).astype(BF16)
        p["w_inter"] = jnp.exp(inter_m - m_row)
        p["floor"] = jnp.exp(-m_row)
        grow = p["btot"] - p["brow"] + p["irow"]
        m_new = jnp.maximum(p["btot"] + m_prev, jnp.max(grow, axis=1, keepdims=True))
        p["w_old"] = jnp.exp(p["btot"] + m_prev - m_new)
        p["wv"] = (p["vaug"] * jnp.exp(grow - m_new)).astype(BF16)
        m_ref[j:j + 1, :] = m_new
    outs = {}
    for p in probs:
        ht = _dot(p["vaug"].astype(BF16), p["st"]) + p["w_inter"] * p["inter"]
        den = ht[p["den_row"]:p["den_row"] + 1]
        outs[(p["d"], p["pair"], p["odd"])] = ht / jnp.maximum(jnp.abs(den), p["floor"])
    for d, o_ref in enumerate(o_refs):
        for pair in range(ML_HEADS // 2):
            both = jnp.where(row < ML_DH, outs[(d, pair, 0)], outs[(d, pair, 1)])
            o_ref[0, row0[d]:row0[d] + lc, pair * LANES:(pair + 1) * LANES] = both.T
    for p in probs:
        c_ref[p["j"]] = p["w_old"] * p["c_old"] + _dot(p["wv"], p["km"])


def _mlstm_call(mq, mg, bias_p, n_ctx):
    bsz, s, _ = mq.shape
    rows = ML_BLOCK_CHUNKS * ML_CHUNK
    assert n_ctx % rows == 0 and s % rows == 0
    nch, ncc = s // rows, n_ctx // rows
    fwd = lambda b, j: (b, j, 0)
    bwd = lambda b, j: (b, jnp.where(j < ncc, ncc - 1 - j, nch - 1 + ncc - j), 0)
    out = jax.ShapeDtypeStruct((bsz, s, ML_W), F32)
    return pl.pallas_call(
        _mlstm_body,
        grid=(bsz, nch),
        in_specs=[pl.BlockSpec((1, rows, MQ_W), fwd), pl.BlockSpec((1, rows, MG_W), fwd),
                  pl.BlockSpec((1, rows, MQ_W), bwd), pl.BlockSpec((1, rows, MG_W), bwd),
                  pl.BlockSpec((1, MG_W), lambda b, j: (0, 0))],
        out_specs=[pl.BlockSpec((1, rows, ML_W), fwd), pl.BlockSpec((1, rows, ML_W), bwd)],
        out_shape=[out, out],
        scratch_shapes=[pltpu.VMEM((2 * ML_HEADS, LANES, LANES), F32),
                        pltpu.VMEM((2 * ML_HEADS, LANES), F32)],
        compiler_params=_cparams(("parallel", "arbitrary"), 40),
        name="mlstm",
    )(mq, mg, mq, mg, bias_p)


LRU_PITCH_PAD = 8
LRU_UNROLL = 8


def _lru_body(xb_ref, gb_ref, cw_ref, cb_ref, wa_ref, ba_ref, wx_ref, bx_ref, lam_ref, y_ref,
              xp_ref, xs_ref, a_ref, u_ref, *, n_ctx, s_len):
    n_lat = s_len - n_ctx
    pad = SUBLANES
    lat_off = n_ctx + 2 * pad
    zeros = jnp.zeros((pad, LANES), F32)
    xp_ref[0:pad, :] = zeros
    xp_ref[pad + n_ctx:lat_off, :] = zeros
    xp_ref[lat_off + n_lat:lat_off + n_lat + pad, :] = zeros
    xp_ref[pad:pad + n_ctx, :] = xb_ref[0, 0:n_ctx, :]
    cchunk = 512

    def copy_body(c, _):
        src = pl.multiple_of(n_ctx + c * cchunk, SUBLANES)
        dst = pl.multiple_of(lat_off + c * cchunk, SUBLANES)
        xp_ref[pl.ds(dst, cchunk), :] = xb_ref[0, pl.ds(src, cchunk), :]
        return 0

    lax.fori_loop(0, n_lat // cchunk, copy_body, 0)

    cw = cw_ref[...]
    cb = cb_ref[...]

    def conv(src0, dst0, n):
        acc = cb + xp_ref[src0 - CONV_LEFT:src0 - CONV_LEFT + n, :] * cw[0:1, :]
        for j in range(1, CONV_W):
            acc = acc + xp_ref[src0 - CONV_LEFT + j:src0 - CONV_LEFT + j + n, :] * cw[j:j + 1, :]
        xs_ref[dst0:dst0 + n, :] = acc

    conv(pad, 0, n_ctx)
    for c in range(n_lat // cchunk):
        conv(lat_off + c * cchunk, n_ctx + c * cchunk, cchunk)

    seg_lat = n_lat // SUBLANES
    seg_ctx = n_ctx // SUBLANES
    p_lat, p_ctx = seg_lat + LRU_PITCH_PAD, seg_ctx + LRU_PITCH_PAD
    ctx_base = SUBLANES * p_lat
    row_io = lax.broadcasted_iota(jnp.int32, (SUBLANES, LANES), 0)

    def gates(x, d):
        xb16 = x.astype(BF16)
        r = _sigmoid(_dot(xb16, wa_ref[d]) + ba_ref[d])
        i = _sigmoid(_dot(xb16, wx_ref[d]) + bx_ref[d])
        log_a = (-LRU_C) * r * jax.nn.softplus(-lam_ref[d])
        a = jnp.exp(log_a)
        return a, jnp.sqrt(jnp.tanh(-log_a) * (1.0 + a * a)) * (i * x)

    def fill(d):
        def lat_body(r, _):
            src = pl.multiple_of(n_ctx + r * seg_lat, SUBLANES)
            dst = pl.multiple_of(r * p_lat, SUBLANES)
            a, u = gates(xs_ref[pl.ds(src, seg_lat), :], d)
            a_ref[d, pl.ds(dst, seg_lat), :] = a
            u_ref[d, pl.ds(dst, seg_lat), :] = u
            return 0

        lax.fori_loop(0, SUBLANES, lat_body, 0)
        a, u = gates(xs_ref[0:n_ctx, :], d)
        for r in range(SUBLANES):
            a_ref[d, ctx_base + r * p_ctx:ctx_base + r * p_ctx + seg_ctx, :] = a[r * seg_ctx:(r + 1) * seg_ctx, :]
            u_ref[d, ctx_base + r * p_ctx:ctx_base + r * p_ctx + seg_ctx, :] = u[r * seg_ctx:(r + 1) * seg_ctx, :]

    def scan(base, n, pitch, h0s):
        def block(tb, carry):
            idx = [[pl.ds(base + (tb * LRU_UNROLL + k if d == 0 else n - 1 - tb * LRU_UNROLL - k),
                          SUBLANES, stride=pitch) for k in range(LRU_UNROLL)] for d in range(2)]
            av = [[a_ref[d, i, :] for i in idx[d]] for d in range(2)]
            uv = [[u_ref[d, i, :] for i in idx[d]] for d in range(2)]
            carry = list(carry)
            for k in range(LRU_UNROLL):
                for d in range(2):
                    h, acum = carry[d]
                    h = av[d][k] * h + uv[d][k]
                    acum = acum * av[d][k]
                    carry[d] = (h, acum)
                    uv[d][k], av[d][k] = h, acum
            for d in range(2):
                for k in range(LRU_UNROLL):
                    u_ref[d, idx[d][k], :] = uv[d][k]
                    a_ref[d, idx[d][k], :] = av[d][k]
            return tuple(carry)

        init = (jnp.zeros((SUBLANES, LANES), F32), jnp.ones((SUBLANES, LANES), F32))
        ends = lax.fori_loop(0, n // LRU_UNROLL, block, (init, init))
        result = []
        for d in range(2):
            h_end, a_end = ends[d]
            carry = h0s[d]
            cvec = jnp.zeros((SUBLANES, LANES), F32)
            for r in (range(SUBLANES) if d == 0 else range(SUBLANES - 1, -1, -1)):
                cvec = jnp.where(row_io == r, carry, cvec)
                carry = h_end[r:r + 1, :] + a_end[r:r + 1, :] * carry
            result.append((cvec, carry))
        return result

    fill(0)
    fill(1)
    zero_state = jnp.zeros((1, LANES), F32)
    ctx_res = scan(ctx_base, seg_ctx, p_ctx, (zero_state, zero_state))
    lat_res = scan(0, seg_lat, p_lat, (ctx_res[0][1], ctx_res[1][1]))
    carries = {(d, "ctx"): ctx_res[d][0] for d in range(2)}
    carries.update({(d, "lat"): lat_res[d][0] for d in range(2)})

    def emit(kind, r, dst0, src0, n):
        hsum = None
        for d in range(2):
            c_in = carries[d, kind][r:r + 1, :]
            part = u_ref[d, src0:src0 + n, :] + a_ref[d, src0:src0 + n, :] * c_in
            hsum = part if hsum is None else hsum + part
        y_ref[0, dst0:dst0 + n, :] = jax.nn.gelu(gb_ref[0, dst0:dst0 + n, :]) * hsum

    for r in range(SUBLANES):
        emit("lat", r, n_ctx + r * seg_lat, r * p_lat, seg_lat)
        emit("ctx", r, r * seg_ctx, ctx_base + r * p_ctx, seg_ctx)


def _lru_call(r, conv_w, conv_b, wa_bd, b_a, wx_bd, b_x, lam, layer, n_ctx):
    bsz, s, _ = r.shape
    nh = LRU_W // LANES
    n_lat = s - n_ctx
    scan_rows = SUBLANES * (n_lat // SUBLANES + LRU_PITCH_PAD) + SUBLANES * (n_ctx // SUBLANES + LRU_PITCH_PAD)
    vec = lambda b, c: (layer, 0, 0, c)
    return pl.pallas_call(
        functools.partial(_lru_body, n_ctx=n_ctx, s_len=s),
        grid=(bsz, nh),
        in_specs=[pl.BlockSpec((1, s, LANES), lambda b, c: (b, 0, c)),
                  pl.BlockSpec((1, s, LANES), lambda b, c: (b, 0, nh + c)),
                  pl.BlockSpec((None, CONV_W, LANES), lambda b, c: (layer, 0, c)),
                  pl.BlockSpec((None, 1, LANES), lambda b, c: (layer, 0, c)),
                  pl.BlockSpec((None, None, 2, LANES, LANES), lambda b, c: (layer, c, 0, 0, 0)),
                  pl.BlockSpec((None, 2, 1, LANES), vec),
                  pl.BlockSpec((None, None, 2, LANES, LANES), lambda b, c: (layer, c, 0, 0, 0)),
                  pl.BlockSpec((None, 2, 1, LANES), vec),
                  pl.BlockSpec((None, 2, 1, LANES), vec)],
        out_specs=pl.BlockSpec((1, s, LANES), lambda b, c: (b, 0, c)),
        out_shape=jax.ShapeDtypeStruct((bsz, s, LRU_W), F32),
        scratch_shapes=[pltpu.VMEM((s + 3 * SUBLANES, LANES), F32),
                        pltpu.VMEM((s, LANES), F32),
                        pltpu.VMEM((2, scan_rows, LANES), F32),
                        pltpu.VMEM((2, scan_rows, LANES), F32)],
        compiler_params=_cparams(("parallel", "parallel"), 48),
        name="rglru",
    )(r, r, conv_w, conv_b, wa_bd, b_a, wx_bd, b_x, lam)


def _out_body(x_ref, m_ref, yal_ref, yac_ref, hf_ref, hb_ref, og_ref, yc_ref, w_ref, hn_ref, o_ref,
              *, tm, n_ctx, blk0):
    b, i = pl.program_id(0), pl.program_id(1)
    hsum = hf_ref[0] + hb_ref[0]
    msq = jnp.dot(hsum * hsum, hn_ref[...], precision=lax.Precision.HIGHEST, preferred_element_type=F32)
    yb = (jax.nn.sigmoid(og_ref[0]) * (hsum * lax.rsqrt(msq + EPS))).astype(BF16)
    na, nb = MLA_HEADS * MLA_V, MLA_HEADS * MLA_V + ML_W
    ya = yal_ref[0]
    if blk0 == 0:
        ya = jnp.where((i + blk0) * tm < n_ctx, yac_ref[0], ya)
    y = (_dot(ya, w_ref[0, 0:na, :]) + _dot(yb, w_ref[0, na:nb, :])
         + _dot(yc_ref[0].astype(BF16), w_ref[0, nb:, :]))
    gate = _mod_rows(m_ref, b, (i + blk0) * tm, tm, n_ctx, 2)
    o_ref[0] = x_ref[0] + gate * y


def _out_call(x, mods, ya_lat, ya_ctx, hf, hb, mq, yc, w_out, head_avg, layer, n_ctx, skip_ctx):
    bsz, s, d = x.shape
    tm = n_ctx
    nctx_blk = n_ctx // tm
    blk0 = nctx_blk if skip_ctx else 0
    rows_out = s - blk0 * tm
    row = lambda b, i: (b, i + blk0, 0)
    return pl.pallas_call(
        functools.partial(_out_body, tm=tm, n_ctx=n_ctx, blk0=blk0),
        grid=(bsz, rows_out // tm),
        in_specs=[pl.BlockSpec((1, tm, d), row),
                  pl.BlockSpec((1, MOD_ROWS, 6 * d), lambda b, i: (layer, 0, 0)),
                  pl.BlockSpec((1, tm, MLA_HEADS * MLA_V), lambda b, i: (b, jnp.maximum(i + blk0 - nctx_blk, 0), 0)),
                  pl.BlockSpec((1, tm, MLA_HEADS * MLA_V), lambda b, i: (b, 0, 0)),
                  pl.BlockSpec((1, tm, ML_W), row),
                  pl.BlockSpec((1, tm, ML_W), row),
                  pl.BlockSpec((1, tm, ML_W), lambda b, i: (b, i + blk0, 3)),
                  pl.BlockSpec((1, tm, LRU_W), row),
                  pl.BlockSpec((1, d, d), lambda b, i: (layer, 0, 0)),
                  pl.BlockSpec((ML_W, ML_W), lambda b, i: (0, 0))],
        out_specs=pl.BlockSpec((1, tm, d), lambda b, i: (b, i, 0)),
        out_shape=jax.ShapeDtypeStruct((bsz, rows_out, d), F32),
        compiler_params=_cparams(("parallel", "parallel"), 40),
        name="out_proj",
    )(x, mods, ya_lat, ya_lat if ya_ctx is None else ya_ctx, hf, hb, mq, yc, w_out, head_avg)


def _mlp_body(x_ref, m_ref, w1_ref, w2_ref, fg_ref, o_ref, u_ref, acc_ref, *, tm, n_ctx, final):
    b, i, k = pl.program_id(0), pl.program_id(1), pl.program_id(2)

    @pl.when(k == 0)
    def _():
        shift = _mod_rows(m_ref, b, i * tm, tm, n_ctx, 3)
        scale = _mod_rows(m_ref, b, i * tm, tm, n_ctx, 4)
        u_ref[...] = (_rms(x_ref[0]) * (1.0 + scale) + shift).astype(BF16)
        acc_ref[...] = jnp.zeros_like(acc_ref)

    hid = jnp.maximum(_dot(u_ref[...], w1_ref[0]), 0.0)
    acc_ref[...] += _dot((hid * hid).astype(BF16), w2_ref[0])

    @pl.when(k == pl.num_programs(2) - 1)
    def _():
        gate = _mod_rows(m_ref, b, i * tm, tm, n_ctx, 5)
        res = x_ref[0] + gate * acc_ref[...]
        if final:
            res = _rms(res) * fg_ref[...]
        o_ref[0] = res


def _mlp_call(x, mods, w1, w2, final_g, layer, n_ctx, final):
    bsz, s, d = x.shape
    tm = _pick_tile(s, (1088, 1024, 256))
    fc = 512
    return pl.pallas_call(
        functools.partial(_mlp_body, tm=tm, n_ctx=n_ctx, final=final),
        grid=(bsz, s // tm, D_FF // fc),
        in_specs=[pl.BlockSpec((1, tm, d), lambda b, i, k: (b, i, 0)),
                  pl.BlockSpec((1, MOD_ROWS, 6 * d), lambda b, i, k: (layer, 0, 0)),
                  pl.BlockSpec((1, d, fc), lambda b, i, k: (layer, 0, k)),
                  pl.BlockSpec((1, fc, d), lambda b, i, k: (layer, k, 0)),
                  pl.BlockSpec((1, d), lambda b, i, k: (0, 0))],
        out_specs=pl.BlockSpec((1, tm, d), lambda b, i, k: (b, i, 0)),
        out_shape=jax.ShapeDtypeStruct((bsz, s, d), F32),
        scratch_shapes=[pltpu.VMEM((tm, d), BF16), pltpu.VMEM((tm, d), F32)],
        compiler_params=_cparams(("parallel", "parallel", "arbitrary"), 48),
        name="mlp",
    )(x, mods, w1, w2, final_g)


def _rope_tables(seq, n_ctx):
    half = MLA_ROPE // 2
    row = jnp.repeat(jnp.arange(seq // GRID_W), GRID_W).astype(F32)
    col = jnp.tile(jnp.arange(GRID_W), seq // GRID_W).astype(F32)
    freqs = 1.0 / (ROPE_BASE ** (jnp.arange(0, half, 2, dtype=F32) / half))
    ang = jnp.concatenate([row[:, None] * freqs, col[:, None] * freqs], axis=-1)
    cos = jnp.concatenate([jnp.ones((n_ctx, half), F32), jnp.cos(ang)], axis=0)
    sin = jnp.concatenate([jnp.zeros((n_ctx, half), F32), jnp.sin(ang)], axis=0)
    return jnp.tile(cos, (1, MLA_HEADS)), jnp.tile(sin, (1, MLA_HEADS))


def _block_diag_halves(w):
    depth = w.shape[0]
    per = LANES // LRU_BD
    wh = w.reshape(depth, 2, LRU_BLOCKS // per, per, LRU_BD, LRU_BD)
    eye = jnp.eye(per, dtype=w.dtype)
    bd = jnp.einsum("ldcpio,pq->ldcpiqo", wh, eye).reshape(depth, 2, LRU_BLOCKS // per, LANES, LANES)
    return bd.transpose(0, 2, 1, 3, 4).astype(BF16)


def kernel(x, c, ctx, c_ctx, w_mod, b_mod, w_in, mla_g_q, mla_w_uq, mla_g_kv, mla_w_ukv, ml_gate_bias,
           lru_conv_w, lru_conv_b, lru_w_a, lru_b_a, lru_w_x, lru_b_x, lru_lam, w_out, w_ff1, w_ff2, final_g):
    bsz, seq, d = x.shape
    n_ctx = ctx.shape[1]
    depth = w_in.shape[0]
    assert bsz <= CTX_MOD_ROW and d == D_MODEL and n_ctx % 256 == 0

    zc = lambda n: jnp.zeros((depth, d, n), w_in.dtype)
    ml0, ml1 = MLA_IN, MLA_IN + 4 * ML_W
    w_in_p = jnp.concatenate([w_in[:, :, :ml0], zc(A_W - MLA_IN), w_in[:, :, ml0:ml1],
                              w_in[:, :, ml1:ml1 + 4 * ML_HEADS], zc(MG_W - 4 * ML_HEADS),
                              w_in[:, :, ml1 + 4 * ML_HEADS:]], axis=-1).astype(BF16)
    half = MLA_ROPE // 2
    uq = mla_w_uq.reshape(depth, MLA_Q_RANK, MLA_HEADS, MLA_QK)
    w_uq_t = jnp.concatenate([uq[..., :MLA_NOPE].reshape(depth, MLA_Q_RANK, -1),
                              uq[..., MLA_NOPE:MLA_NOPE + half].reshape(depth, MLA_Q_RANK, -1),
                              uq[..., MLA_NOPE + half:].reshape(depth, MLA_Q_RANK, -1)],
                             axis=-1).astype(BF16).transpose(0, 2, 1)
    ukv = mla_w_ukv.reshape(depth, MLA_KV_RANK, MLA_HEADS, MLA_NOPE + MLA_V)
    w_ukv_k = ukv[..., :MLA_NOPE].reshape(depth, MLA_KV_RANK, -1).astype(BF16)
    w_ukv_vt = ukv[..., MLA_NOPE:].reshape(depth, MLA_KV_RANK, -1).astype(BF16).transpose(0, 2, 1)
    g_q = mla_g_q.reshape(depth, 1, MLA_Q_RANK)
    g_kv = mla_g_kv.reshape(depth, 1, MLA_KV_RANK)
    bias_p = jnp.pad(ml_gate_bias, ((0, 0), (0, MG_W - 4 * ML_HEADS)))
    wa_bd, wx_bd = _block_diag_halves(lru_w_a), _block_diag_halves(lru_w_x)
    vec4 = lambda v: v.reshape(depth, 2, 1, LRU_W)
    conv_b = lru_conv_b.reshape(depth, 1, LRU_W)
    w_out16, w1_16, w2_16 = w_out.astype(BF16), w_ff1.astype(BF16), w_ff2.astype(BF16)
    head_avg = jnp.kron(jnp.eye(ML_HEADS, dtype=F32), jnp.full((ML_DH, ML_DH), 1.0 / ML_DH, F32))
    cos4, sin4 = _rope_tables(seq, n_ctx)
    rope = (cos4, sin4, cos4.T, sin4.T)
    fg = final_g.reshape(1, d)

    cvec = jnp.concatenate([c, jnp.zeros((CTX_MOD_ROW - bsz, d), c.dtype), c_ctx[None, :],
                            jnp.zeros((MOD_ROWS - CTX_MOD_ROW - 1, d), c.dtype)], axis=0)
    mods = _mods_call(cvec, w_mod, b_mod)

    xs = jnp.concatenate([ctx, x], axis=1)
    for l in range(depth):
        last = l == depth - 1
        a, mq, mg, r = _in_call(xs, mods, w_in_p, l, n_ctx)
        qt, k, vt = _prep_call(a, g_q, w_uq_t, g_kv, w_ukv_k, w_ukv_vt, rope, l, n_ctx)
        ya_lat = _attn_latent_call(qt, k, vt, n_ctx)
        ya_ctx = None if last else _attn_ctx_call(qt, k, vt, n_ctx)
        hf, hb = _mlstm_call(mq, mg, bias_p[l:l + 1], n_ctx)
        yc = _lru_call(r, lru_conv_w, conv_b, wa_bd, vec4(lru_b_a), wx_bd, vec4(lru_b_x), vec4(lru_lam), l, n_ctx)
        x1 = _out_call(xs, mods, ya_lat, ya_ctx, hf, hb, mq, yc, w_out16, head_avg, l, n_ctx, skip_ctx=last)
        xs = _mlp_call(x1, mods, w1_16, w2_16, fg, l, 0 if last else n_ctx, final=last)
    return xs
```

```python
import functools

import jax
import jax.numpy as jnp
from jax import lax
from jax.experimental import pallas as pl
from jax.experimental.pallas import tpu as pltpu

F32 = jnp.float32
BF16 = jnp.bfloat16

D_MODEL = 1024
DEPTH = 4
GRID_W = 64
N_CTX = 256
MLA_HEADS = 4
MLA_Q_RANK = 256
MLA_KV_RANK = 128
MLA_NOPE = 128
MLA_ROPE = 64
MLA_V = 128
MLA_QK = MLA_NOPE + MLA_ROPE
MLA_SCALE = MLA_QK ** -0.5
ROPE_BASE = 10000.0
ML_HEADS = 4
ML_DH = 64
ML_W = ML_HEADS * ML_DH
ML_CHUNK = 128
LRU_W = 256
LRU_BLOCKS = 4
LRU_BD = LRU_W // LRU_BLOCKS
CONV_W = 4
CONV_LEFT = 2
LRU_C = 8.0
D_FF = 4 * D_MODEL
EPS = 1e-6
MLA_IN = MLA_Q_RANK + MLA_KV_RANK + MLA_ROPE
ML_IN = 4 * ML_W + 4 * ML_HEADS
LRU_IN = 2 * LRU_W

LANES = 128
SUBLANES = 8
MOD_ROWS = 8
CTX_MOD_ROW = 4

A_W = 512
MQ_W = 4 * ML_W
MG_W = LANES
R_W = 2 * LRU_W
IN_W = A_W + MQ_W + MG_W + R_W


def _cparams(sem, vmem_mb):
    return pltpu.CompilerParams(dimension_semantics=sem, vmem_limit_bytes=vmem_mb * 1024 * 1024)


def _mod_rows(m_ref, b, row0, tm, ctx_start, seg):
    lo, hi = seg * D_MODEL, (seg + 1) * D_MODEL
    lat = m_ref[0, pl.ds(b, 1), lo:hi]
    if ctx_start is None:
        return lat
    ctx = m_ref[0, CTX_MOD_ROW:CTX_MOD_ROW + 1, lo:hi]
    rows = row0 + lax.broadcasted_iota(jnp.int32, (tm, 1), 0)
    return jnp.where(rows >= ctx_start, ctx, lat)


def _pick_tile(n, candidates):
    return next(t for t in candidates if n % t == 0)


def _rms(x):
    return x * lax.rsqrt(jnp.mean(x * x, axis=-1, keepdims=True) + EPS)


def _sigmoid(x):
    return 0.5 * jnp.tanh(0.5 * x) + 0.5


def _dot(a, b):
    return jnp.dot(a, b, preferred_element_type=F32)


def _dot_nt(a, b):
    return lax.dot_general(a, b, (((1,), (1,)), ((), ())), preferred_element_type=F32)


def _mods_body(c_ref, w_ref, b_ref, o_ref):
    cv = c_ref[...]
    act = (cv * jax.nn.sigmoid(cv)).astype(BF16)
    o_ref[0] = _dot(act, w_ref[0].astype(BF16)) + b_ref[0]


def _mods_call(cvec, w_mod, b_mod):
    depth, d, n = w_mod.shape
    tn = 1536
    return pl.pallas_call(
        _mods_body,
        grid=(depth, n // tn),
        in_specs=[pl.BlockSpec((MOD_ROWS, d), lambda l, j: (0, 0)),
                  pl.BlockSpec((1, d, tn), lambda l, j: (l, 0, j)),
                  pl.BlockSpec((1, 1, tn), lambda l, j: (l, 0, j))],
        out_specs=pl.BlockSpec((1, MOD_ROWS, tn), lambda l, j: (l, 0, j)),
        out_shape=jax.ShapeDtypeStruct((depth, MOD_ROWS, n), F32),
        compiler_params=_cparams(("arbitrary", "arbitrary"), 40),
        name="mods",
    )(cvec, w_mod, b_mod.reshape(depth, 1, n))


def _in_body(x_ref, m_ref, w_ref, a_ref, q_ref, g_ref, r_ref, *, tm, ctx_start):
    b, i = pl.program_id(0), pl.program_id(1)
    xn = _rms(x_ref[0])
    shift = _mod_rows(m_ref, b, i * tm, tm, ctx_start, 0)
    scale = _mod_rows(m_ref, b, i * tm, tm, ctx_start, 1)
    u = (xn * (1.0 + scale) + shift).astype(BF16)
    a_ref[0] = _dot(u, w_ref[0, :, 0:A_W])
    q_ref[0] = _dot(u, w_ref[0, :, A_W:A_W + MQ_W])
    g_ref[0] = _dot(u, w_ref[0, :, A_W + MQ_W:A_W + MQ_W + MG_W])
    r_ref[0] = _dot(u, w_ref[0, :, A_W + MQ_W + MG_W:IN_W])


def _in_call(x, mods, w_in_p, layer, n_ctx):
    bsz, s, d = x.shape
    tm = _pick_tile(s, (1088, 256))
    row = lambda b, i: (b, i, 0)
    return pl.pallas_call(
        functools.partial(_in_body, tm=tm, ctx_start=s - n_ctx),
        grid=(bsz, s // tm),
        in_specs=[pl.BlockSpec((1, tm, d), row),
                  pl.BlockSpec((1, MOD_ROWS, 6 * d), lambda b, i: (layer, 0, 0)),
                  pl.BlockSpec((1, d, IN_W), lambda b, i: (layer, 0, 0))],
        out_specs=[pl.BlockSpec((1, tm, A_W), row), pl.BlockSpec((1, tm, MQ_W), row),
                   pl.BlockSpec((1, tm, MG_W), row), pl.BlockSpec((1, tm, R_W), row)],
        out_shape=[jax.ShapeDtypeStruct((bsz, s, w), F32) for w in (A_W, MQ_W, MG_W, R_W)],
        compiler_params=_cparams(("parallel", "parallel"), 52),
        name="in_proj",
    )(x, mods, w_in_p)


VT_ROWS = MLA_V + 16
LOG2E = 1.4426950408889634


def _prep_body(a_ref, gq_ref, gkv_ref, wuqt_ref, wukvk_ref, wukvvt_ref, cos_ref, sin_ref, cost_ref, sint_ref,
               qt_ref, k_ref, vt_ref):
    a = a_ref[0]
    tm = a.shape[0]
    nq, nkv = MLA_Q_RANK, MLA_Q_RANK + MLA_KV_RANK
    half = MLA_ROPE // 2
    hn = MLA_HEADS * MLA_NOPE
    cq_t = (_rms(a[:, 0:nq]) * gq_ref[0]).T.astype(BF16)
    ckv = _rms(a[:, nq:nkv]) * gkv_ref[0]
    q_t = _dot(wuqt_ref[0], cq_t) * (MLA_SCALE * LOG2E)
    x1, x2 = q_t[hn:hn + LANES], q_t[hn + LANES:hn + 2 * LANES]
    cos_t, sin_t = cost_ref[...], sint_ref[...]
    r1 = (x1 * cos_t - x2 * sin_t).astype(BF16)
    r2 = (x1 * sin_t + x2 * cos_t).astype(BF16)
    k_nope = _dot(ckv.astype(BF16), wukvk_ref[0])
    v_t = _dot(wukvvt_ref[0], ckv.T.astype(BF16))
    k1, k2 = a[:, nkv:nkv + half], a[:, nkv + half:nkv + 2 * half]
    c32, s32 = cos_ref[:, 0:half], sin_ref[:, 0:half]
    kr1 = (k1 * c32 - k2 * s32).astype(BF16)
    kr2 = (k1 * s32 + k2 * c32).astype(BF16)
    ones = jnp.ones((VT_ROWS - MLA_V, tm), BF16)
    for h in range(MLA_HEADS):
        qt_ref[0, h, 0:MLA_NOPE, :] = q_t[h * MLA_NOPE:(h + 1) * MLA_NOPE].astype(BF16)
        qt_ref[0, h, MLA_NOPE:MLA_NOPE + half, :] = r1[h * half:(h + 1) * half]
        qt_ref[0, h, MLA_NOPE + half:MLA_QK, :] = r2[h * half:(h + 1) * half]
        k_ref[0, h, :, 0:MLA_NOPE] = k_nope[:, h * MLA_NOPE:(h + 1) * MLA_NOPE].astype(BF16)
        k_ref[0, h, :, MLA_NOPE:MLA_NOPE + half] = kr1
        k_ref[0, h, :, MLA_NOPE + half:MLA_QK] = kr2
        vt_ref[0, h, 0:MLA_V, :] = v_t[h * MLA_V:(h + 1) * MLA_V].astype(BF16)
        vt_ref[0, h, MLA_V:VT_ROWS, :] = ones


def _prep_call(a, g_q, w_uq_t, g_kv, w_ukv_k, w_ukv_vt, rope, layer):
    bsz, s, _ = a.shape
    tm = 256
    lsel = lambda b, i: (layer, 0, 0)
    cos4, sin4, cos4_t, sin4_t = rope
    return pl.pallas_call(
        _prep_body,
        grid=(bsz, s // tm),
        in_specs=[pl.BlockSpec((1, tm, A_W), lambda b, i: (b, i, 0)),
                  pl.BlockSpec((1, 1, MLA_Q_RANK), lsel),
                  pl.BlockSpec((1, 1, MLA_KV_RANK), lsel),
                  pl.BlockSpec((1, MLA_HEADS * MLA_QK, MLA_Q_RANK), lsel),
                  pl.BlockSpec((1, MLA_KV_RANK, MLA_HEADS * MLA_NOPE), lsel),
                  pl.BlockSpec((1, MLA_HEADS * MLA_V, MLA_KV_RANK), lsel),
                  pl.BlockSpec((tm, LANES), lambda b, i: (i, 0)),
                  pl.BlockSpec((tm, LANES), lambda b, i: (i, 0)),
                  pl.BlockSpec((LANES, tm), lambda b, i: (0, i)),
                  pl.BlockSpec((LANES, tm), lambda b, i: (0, i))],
        out_specs=[pl.BlockSpec((1, MLA_HEADS, MLA_QK, tm), lambda b, i: (b, 0, 0, i)),
                   pl.BlockSpec((1, MLA_HEADS, tm, MLA_QK), lambda b, i: (b, 0, i, 0)),
                   pl.BlockSpec((1, MLA_HEADS, VT_ROWS, tm), lambda b, i: (b, 0, 0, i))],
        out_shape=[jax.ShapeDtypeStruct((bsz, MLA_HEADS, MLA_QK, s), BF16),
                   jax.ShapeDtypeStruct((bsz, MLA_HEADS, s, MLA_QK), BF16),
                   jax.ShapeDtypeStruct((bsz, MLA_HEADS, VT_ROWS, s), BF16)],
        compiler_params=_cparams(("parallel", "parallel"), 40),
        name="mla_prep",
    )(a, g_q, g_kv, w_uq_t, w_ukv_k, w_ukv_vt, cos4, sin4, cos4_t, sin4_t)


def _attn_body(qt_ref, k_ref, vt_ref, o_ref, *, subs, zero_rows):
    work = [(q0, tq, c, c is chunks[0], c is chunks[-1]) for q0, tq, chunks in subs for c in chunks]
    score = lambda w: _dot(k_ref[0, 0, w[2][0]:w[2][0] + w[2][1], :],
                           qt_ref[0, 0, :, w[0]:w[0] + w[1]])
    pending = [score(w) for w in work[:ATTN_AHEAD]]
    m = acc = None
    for idx, (q0, tq, (start, size), first, last) in enumerate(work):
        if first:
            m = jnp.full((1, tq), -jnp.inf, F32)
            acc = jnp.zeros((VT_ROWS, tq), F32)
        st = pending.pop(0)
        if idx + ATTN_AHEAD < len(work):
            pending.append(score(work[idx + ATTN_AHEAD]))
        m_new = jnp.maximum(m, jnp.max(st, axis=0, keepdims=True))
        p = jnp.exp2(st - m_new).astype(BF16)
        acc = jnp.exp2(m - m_new) * acc + _dot(vt_ref[0, 0, :, start:start + size], p)
        m = m_new
        if last:
            out_t = acc[0:MLA_V] / acc[MLA_V:MLA_V + 1]
            o_ref[0, q0:q0 + tq, :] = out_t.T.astype(o_ref.dtype)
    if zero_rows is not None:
        o_ref[0, zero_rows[0]:zero_rows[1], :] = jnp.zeros((zero_rows[1] - zero_rows[0], MLA_V), o_ref.dtype)


ATTN_TQ = 512
ATTN_TK = 512
ATTN_AHEAD = 2


def _attn_call(qt, k, vt, n_ctx, with_ctx):
    bsz, nh, s, dk = k.shape
    n_lat = s - n_ctx
    assert n_lat % ATTN_TQ == 0 and n_lat % ATTN_TK == 0
    ctx_chunk = (n_lat, n_ctx)
    lat_chunks = (ctx_chunk,) + tuple((j * ATTN_TK, ATTN_TK) for j in range(n_lat // ATTN_TK))
    subs = [(j * ATTN_TQ, ATTN_TQ, lat_chunks) for j in range(n_lat // ATTN_TQ)]
    if with_ctx:
        subs.append((n_lat, n_ctx, (ctx_chunk,)))
    return pl.pallas_call(
        functools.partial(_attn_body, subs=tuple(subs), zero_rows=None if with_ctx else (n_lat, s)),
        grid=(bsz, nh),
        in_specs=[pl.BlockSpec((1, 1, dk, s), lambda b, h: (b, h, 0, 0)),
                  pl.BlockSpec((1, 1, s, dk), lambda b, h: (b, h, 0, 0)),
                  pl.BlockSpec((1, 1, VT_ROWS, s), lambda b, h: (b, h, 0, 0))],
        out_specs=pl.BlockSpec((1, s, MLA_V), lambda b, h: (b, 0, h)),
        out_shape=jax.ShapeDtypeStruct((bsz, s, nh * MLA_V), BF16),
        compiler_params=_cparams(("parallel", "parallel"), 48),
        name="mla_attn",
    )(qt, k, vt)


ML_BLOCK_CHUNKS = 2


def _mlstm_body(xf_ref, gf_ref, xb_ref, gb_ref, bias_ref, hf_ref, hb_ref, c_ref, m_ref):
    lc = ML_CHUNK
    assert lc == LANES

    @pl.when(pl.program_id(1) == 0)
    def _():
        c_ref[...] = jnp.zeros_like(c_ref)
        m_ref[...] = jnp.zeros_like(m_ref)

    s_io = lax.broadcasted_iota(jnp.int32, (lc, lc), 0)
    t_io = lax.broadcasted_iota(jnp.int32, (lc, lc), 1)
    lane = lax.broadcasted_iota(jnp.int32, (lc, LANES), 1)
    row = lax.broadcasted_iota(jnp.int32, (LANES, lc), 0)
    row8 = lax.broadcasted_iota(jnp.int32, (SUBLANES, lc), 0)
    bias = bias_ref[...]
    ngate = 4 * ML_HEADS
    ones_sq = jnp.ones((lc, lc), BF16)
    nsub = xf_ref.shape[1] // lc
    for t in range(nsub):
        _mlstm_chunk((xf_ref, xb_ref), (gf_ref, gb_ref), (hf_ref, hb_ref), c_ref, m_ref,
                     (t * lc, (nsub - 1 - t) * lc), bias, ones_sq, (s_io, t_io, lane, row, row8))


def _mlstm_chunk(x_refs, g_refs, o_refs, c_ref, m_ref, row0, bias, ones_sq, iotas):
    lc = ML_CHUNK
    ngate = 4 * ML_HEADS
    s_io, t_io, lane, row, row8 = iotas
    probs = []
    c_rows = jnp.zeros((SUBLANES, lc), F32)
    for d in range(2):
        x_ref, g_ref = x_refs[d], g_refs[d]
        rows = slice(row0[d], row0[d] + lc)
        mask = (s_io <= t_io) if d == 0 else (s_io >= t_io)
        gt = (g_ref[0, rows, :] + bias).T[0:ngate]
        lf = jax.nn.log_sigmoid(gt)
        hi = lf.astype(BF16)
        r1 = lf - hi.astype(F32)
        mid = r1.astype(BF16)
        lo = (r1 - mid.astype(F32)).astype(BF16)
        sums = _dot(jnp.concatenate([hi, mid, lo], axis=0),
                    jnp.concatenate([mask.astype(BF16), ones_sq], axis=1))
        sums = sums[0:ngate] + sums[ngate:2 * ngate] + sums[2 * ngate:3 * ngate]
        b_run, b_tot = sums[:, 0:lc], sums[:, lc:2 * lc]
        x = x_ref[0, rows, :]
        for pair in range(ML_HEADS // 2):
            qs = x[:, pair * LANES:(pair + 1) * LANES] * (ML_DH ** -0.5)
            ks = x[:, ML_W + pair * LANES:ML_W + (pair + 1) * LANES]
            vt = x[:, 2 * ML_W + pair * LANES:2 * ML_W + (pair + 1) * LANES].T
            for odd in range(2):
                h = 2 * pair + odd
                ci, cf = d * 2 * ML_HEADS + h, d * 2 * ML_HEADS + ML_HEADS + h
                own = (lane >= ML_DH) if odd else (lane < ML_DH)
                own_r = (row >= ML_DH) if odd else (row < ML_DH)
                den_row = 0 if odd else ML_DH
                j = d * ML_HEADS + h
                brow, irow = b_run[cf:cf + 1], gt[ci:ci + 1]
                c_rows = jnp.where(row8 == j, brow - irow, c_rows)
                probs.append(dict(
                    j=j, d=d, pair=pair, odd=odd, mask=mask, den_row=den_row, brow=brow, irow=irow,
                    btot=b_tot[cf:cf + 1],
                    qm=jnp.where(own, qs, 0.0).astype(BF16), ks=ks.astype(BF16),
                    km=jnp.where(own, ks, 0.0).astype(BF16),
                    vaug=jnp.where(own_r, vt, jnp.where(row == den_row, 1.0, 0.0))))

    c_cols = jnp.concatenate([c_rows, jnp.zeros((LANES - SUBLANES, lc), F32)], axis=0).T

    for p in probs:
        p["c_old"] = c_ref[p["j"]]
        p["kq"] = _dot_nt(p["ks"], p["qm"])
        p["inter"] = _dot_nt(p["c_old"].astype(BF16), p["qm"])
    for p in probs:
        j = p["j"]
        m_prev = m_ref[j:j + 1, :]
        dt = jnp.where(p["mask"], p["brow"] - c_cols[:, j:j + 1], -jnp.inf)
        inter_m = p["brow"] + m_prev
        m_row = jnp.maximum(inter_m, jnp.max(dt, axis=0, keepdims=True))
        p["st"] = (p["kq"] * jnp.exp(dt - m_row)).astype(BF16)
        p["w_inter"] = jnp.exp(inter_m - m_row)
        p["floor"] = jnp.exp(-m_row)
        grow = p["btot"] - p["brow"] + p["irow"]
        m_new = jnp.maximum(p["btot"] + m_prev, jnp.max(grow, axis=1, keepdims=True))
        p["w_old"] = jnp.exp(p["btot"] + m_prev - m_new)
        p["wv"] = (p["vaug"] * jnp.exp(grow - m_new)).astype(BF16)
        m_ref[j:j + 1, :] = m_new
    outs = {}
    for p in probs:
        ht = _dot(p["vaug"].astype(BF16), p["st"]) + p["w_inter"] * p["inter"]
        den = ht[p["den_row"]:p["den_row"] + 1]
        outs[(p["d"], p["pair"], p["odd"])] = ht / jnp.maximum(jnp.abs(den), p["floor"])
    for d, o_ref in enumerate(o_refs):
        for pair in range(ML_HEADS // 2):
            both = jnp.where(row < ML_DH, outs[(d, pair, 0)], outs[(d, pair, 1)])
            o_ref[0, row0[d]:row0[d] + lc, pair * LANES:(pair + 1) * LANES] = both.T
    for p in probs:
        c_ref[p["j"]] = p["w_old"] * p["c_old"] + _dot(p["wv"], p["km"])


def _mlstm_call(mq, mg, bias_p, n_ctx):
    bsz, s, _ = mq.shape
    rows = ML_BLOCK_CHUNKS * ML_CHUNK
    assert n_ctx % rows == 0 and s % rows == 0
    nch, ncc = s // rows, n_ctx // rows
    nlc = nch - ncc
    fwd = lambda b, j: (b, jnp.where(j < ncc, nlc + j, j - ncc), 0)
    bwd = lambda b, j: (b, nch - 1 - j, 0)
    out = jax.ShapeDtypeStruct((bsz, s, ML_W), F32)
    return pl.pallas_call(
        _mlstm_body,
        grid=(bsz, nch),
        in_specs=[pl.BlockSpec((1, rows, MQ_W), fwd), pl.BlockSpec((1, rows, MG_W), fwd),
                  pl.BlockSpec((1, rows, MQ_W), bwd), pl.BlockSpec((1, rows, MG_W), bwd),
                  pl.BlockSpec((1, MG_W), lambda b, j: (0, 0))],
        out_specs=[pl.BlockSpec((1, rows, ML_W), fwd), pl.BlockSpec((1, rows, ML_W), bwd)],
        out_shape=[out, out],
        scratch_shapes=[pltpu.VMEM((2 * ML_HEADS, LANES, LANES), F32),
                        pltpu.VMEM((2 * ML_HEADS, LANES), F32)],
        compiler_params=_cparams(("parallel", "arbitrary"), 40),
        name="mlstm",
    )(mq, mg, mq, mg, bias_p)


LRU_PITCH_PAD = 8
LRU_UNROLL = 8


def _lru_body(xb_ref, gb_ref, cw_ref, cb_ref, wa_ref, ba_ref, wx_ref, bx_ref, lam_ref, y_ref,
              xp_ref, xs_ref, a_ref, u_ref, *, n_ctx, s_len):
    n_lat = s_len - n_ctx
    pad = SUBLANES
    lat_off = n_ctx + 2 * pad
    zeros = jnp.zeros((pad, LANES), F32)
    xp_ref[0:pad, :] = zeros
    xp_ref[pad + n_ctx:lat_off, :] = zeros
    xp_ref[lat_off + n_lat:lat_off + n_lat + pad, :] = zeros
    xp_ref[pad:pad + n_ctx, :] = xb_ref[0, n_lat:n_lat + n_ctx, :]
    cchunk = 512

    def copy_body(c, _):
        src = pl.multiple_of(c * cchunk, SUBLANES)
        dst = pl.multiple_of(lat_off + c * cchunk, SUBLANES)
        xp_ref[pl.ds(dst, cchunk), :] = xb_ref[0, pl.ds(src, cchunk), :]
        return 0

    lax.fori_loop(0, n_lat // cchunk, copy_body, 0)

    cw = cw_ref[...]
    cb = cb_ref[...]

    def conv(src0, dst0, n):
        acc = cb + xp_ref[src0 - CONV_LEFT:src0 - CONV_LEFT + n, :] * cw[0:1, :]
        for j in range(1, CONV_W):
            acc = acc + xp_ref[src0 - CONV_LEFT + j:src0 - CONV_LEFT + j + n, :] * cw[j:j + 1, :]
        xs_ref[dst0:dst0 + n, :] = acc

    conv(pad, n_lat, n_ctx)
    for c in range(n_lat // cchunk):
        conv(lat_off + c * cchunk, c * cchunk, cchunk)

    seg_lat = n_lat // SUBLANES
    seg_ctx = n_ctx // SUBLANES
    p_lat, p_ctx = seg_lat + LRU_PITCH_PAD, seg_ctx + LRU_PITCH_PAD
    ctx_base = SUBLANES * p_lat
    row_io = lax.broadcasted_iota(jnp.int32, (SUBLANES, LANES), 0)

    def gates(x, d):
        xb16 = x.astype(BF16)
        r = _sigmoid(_dot(xb16, wa_ref[d]) + ba_ref[d])
        i = _sigmoid(_dot(xb16, wx_ref[d]) + bx_ref[d])
        log_a = (-LRU_C) * r * jax.nn.softplus(-lam_ref[d])
        a = jnp.exp(log_a)
        return a, jnp.sqrt(jnp.tanh(-log_a) * (1.0 + a * a)) * (i * x)

    def fill(d):
        def lat_body(r, _):
            src = pl.multiple_of(r * seg_lat, SUBLANES)
            dst = pl.multiple_of(r * p_lat, SUBLANES)
            a, u = gates(xs_ref[pl.ds(src, seg_lat), :], d)
            a_ref[d, pl.ds(dst, seg_lat), :] = a
            u_ref[d, pl.ds(dst, seg_lat), :] = u
            return 0

        lax.fori_loop(0, SUBLANES, lat_body, 0)
        a, u = gates(xs_ref[n_lat:n_lat + n_ctx, :], d)
        for r in range(SUBLANES):
            a_ref[d, ctx_base + r * p_ctx:ctx_base + r * p_ctx + seg_ctx, :] = a[r * seg_ctx:(r + 1) * seg_ctx, :]
            u_ref[d, ctx_base + r * p_ctx:ctx_base + r * p_ctx + seg_ctx, :] = u[r * seg_ctx:(r + 1) * seg_ctx, :]

    def scan(base, n, pitch, h0s):
        def block(tb, carry):
            idx = [[pl.ds(base + (tb * LRU_UNROLL + k if d == 0 else n - 1 - tb * LRU_UNROLL - k),
                          SUBLANES, stride=pitch) for k in range(LRU_UNROLL)] for d in range(2)]
            av = [[a_ref[d, i, :] for i in idx[d]] for d in range(2)]
            uv = [[u_ref[d, i, :] for i in idx[d]] for d in range(2)]
            carry = list(carry)
            for k in range(LRU_UNROLL):
                for d in range(2):
                    h, acum = carry[d]
                    h = av[d][k] * h + uv[d][k]
                    acum = acum * av[d][k]
                    carry[d] = (h, acum)
                    uv[d][k], av[d][k] = h, acum
            for d in range(2):
                for k in range(LRU_UNROLL):
                    u_ref[d, idx[d][k], :] = uv[d][k]
                    a_ref[d, idx[d][k], :] = av[d][k]
            return tuple(carry)

        init = (jnp.zeros((SUBLANES, LANES), F32), jnp.ones((SUBLANES, LANES), F32))
        ends = lax.fori_loop(0, n // LRU_UNROLL, block, (init, init))
        result = []
        for d in range(2):
            h_end, a_end = ends[d]
            carry = h0s[d]
            cvec = jnp.zeros((SUBLANES, LANES), F32)
            for r in (range(SUBLANES) if d == 0 else range(SUBLANES - 1, -1, -1)):
                cvec = jnp.where(row_io == r, carry, cvec)
                carry = h_end[r:r + 1, :] + a_end[r:r + 1, :] * carry
            result.append((cvec, carry))
        return result

    fill(0)
    fill(1)
    zero_state = jnp.zeros((1, LANES), F32)
    ctx_res = scan(ctx_base, seg_ctx, p_ctx, (zero_state, zero_state))
    lat_res = scan(0, seg_lat, p_lat, (ctx_res[0][1], ctx_res[1][1]))
    carries = {(d, "ctx"): ctx_res[d][0] for d in range(2)}
    carries.update({(d, "lat"): lat_res[d][0] for d in range(2)})

    def emit(kind, r, dst0, src0, n):
        hsum = None
        for d in range(2):
            c_in = carries[d, kind][r:r + 1, :]
            part = u_ref[d, src0:src0 + n, :] + a_ref[d, src0:src0 + n, :] * c_in
            hsum = part if hsum is None else hsum + part
        y_ref[0, dst0:dst0 + n, :] = jax.nn.gelu(gb_ref[0, dst0:dst0 + n, :]) * hsum

    for r in range(SUBLANES):
        emit("lat", r, r * seg_lat, r * p_lat, seg_lat)
        emit("ctx", r, n_lat + r * seg_ctx, ctx_base + r * p_ctx, seg_ctx)


def _lru_call(r, conv_w, conv_b, wa_bd, b_a, wx_bd, b_x, lam, layer, n_ctx):
    bsz, s, _ = r.shape
    nh = LRU_W // LANES
    n_lat = s - n_ctx
    scan_rows = SUBLANES * (n_lat // SUBLANES + LRU_PITCH_PAD) + SUBLANES * (n_ctx // SUBLANES + LRU_PITCH_PAD)
    vec = lambda b, c: (layer, 0, 0, c)
    return pl.pallas_call(
        functools.partial(_lru_body, n_ctx=n_ctx, s_len=s),
        grid=(bsz, nh),
        in_specs=[pl.BlockSpec((1, s, LANES), lambda b, c: (b, 0, c)),
                  pl.BlockSpec((1, s, LANES), lambda b, c: (b, 0, nh + c)),
                  pl.BlockSpec((None, CONV_W, LANES), lambda b, c: (layer, 0, c)),
                  pl.BlockSpec((None, 1, LANES), lambda b, c: (layer, 0, c)),
                  pl.BlockSpec((None, None, 2, LANES, LANES), lambda b, c: (layer, c, 0, 0, 0)),
                  pl.BlockSpec((None, 2, 1, LANES), vec),
                  pl.BlockSpec((None, None, 2, LANES, LANES), lambda b, c: (layer, c, 0, 0, 0)),
                  pl.BlockSpec((None, 2, 1, LANES), vec),
                  pl.BlockSpec((None, 2, 1, LANES), vec)],
        out_specs=pl.BlockSpec((1, s, LANES), lambda b, c: (b, 0, c)),
        out_shape=jax.ShapeDtypeStruct((bsz, s, LRU_W), F32),
        scratch_shapes=[pltpu.VMEM((s + 3 * SUBLANES, LANES), F32),
                        pltpu.VMEM((s, LANES), F32),
                        pltpu.VMEM((2, scan_rows, LANES), F32),
                        pltpu.VMEM((2, scan_rows, LANES), F32)],
        compiler_params=_cparams(("parallel", "parallel"), 48),
        name="rglru",
    )(r, r, conv_w, conv_b, wa_bd, b_a, wx_bd, b_x, lam)


def _mix_mlp_body(x_ref, m_ref, ya_ref, hf_ref, hb_ref, og_ref, yc_ref, wo_ref, hn_ref, w1_ref, w2_ref, fg_ref,
                  o_ref, x1_ref, u_ref, acc_ref, *, tm, ctx_start, final):
    b, i, k = pl.program_id(0), pl.program_id(1), pl.program_id(2)

    @pl.when(k == 0)
    def _():
        hsum = hf_ref[0] + hb_ref[0]
        msq = jnp.dot(hsum * hsum, hn_ref[...], precision=lax.Precision.HIGHEST, preferred_element_type=F32)
        yb = (_sigmoid(og_ref[0]) * (hsum * lax.rsqrt(msq + EPS))).astype(BF16)
        na, nb = MLA_HEADS * MLA_V, MLA_HEADS * MLA_V + ML_W
        y = (_dot(ya_ref[0], wo_ref[0, 0:na, :]) + _dot(yb, wo_ref[0, na:nb, :])
             + _dot(yc_ref[0].astype(BF16), wo_ref[0, nb:, :]))
        x1 = x_ref[0] + _mod_rows(m_ref, b, i * tm, tm, ctx_start, 2) * y
        x1_ref[...] = x1
        shift = _mod_rows(m_ref, b, i * tm, tm, ctx_start, 3)
        scale = _mod_rows(m_ref, b, i * tm, tm, ctx_start, 4)
        u_ref[...] = (_rms(x1) * (1.0 + scale) + shift).astype(BF16)
        acc_ref[...] = jnp.zeros_like(acc_ref)

    hid = jnp.maximum(_dot(u_ref[...], w1_ref[0]), 0.0)
    acc_ref[...] += _dot((hid * hid).astype(BF16), w2_ref[0])

    @pl.when(k == pl.num_programs(2) - 1)
    def _():
        res = x1_ref[...] + _mod_rows(m_ref, b, i * tm, tm, ctx_start, 5) * acc_ref[...]
        if final:
            res = _rms(res) * fg_ref[...]
        o_ref[0] = res


MLP_FF_CHUNK = 1024


def _mix_mlp_call(x, mods, ya, hf, hb, mq, yc, w_out, head_avg, w1, w2, final_g, layer, n_ctx, final):
    bsz, s, d = x.shape
    rows_out = s - n_ctx if final else s
    tm = _pick_tile(rows_out, (544, 512, 256))
    fc = MLP_FF_CHUNK
    row = lambda b, i, k: (b, i, 0)
    lsel = lambda b, i, k: (layer, 0, 0)
    return pl.pallas_call(
        functools.partial(_mix_mlp_body, tm=tm, ctx_start=None if final else s - n_ctx, final=final),
        grid=(bsz, rows_out // tm, D_FF // fc),
        in_specs=[pl.BlockSpec((1, tm, d), row),
                  pl.BlockSpec((1, MOD_ROWS, 6 * d), lsel),
                  pl.BlockSpec((1, tm, MLA_HEADS * MLA_V), row),
                  pl.BlockSpec((1, tm, ML_W), row),
                  pl.BlockSpec((1, tm, ML_W), row),
                  pl.BlockSpec((1, tm, ML_W), lambda b, i, k: (b, i, 3)),
                  pl.BlockSpec((1, tm, LRU_W), row),
                  pl.BlockSpec((1, d, d), lsel),
                  pl.BlockSpec((ML_W, ML_W), lambda b, i, k: (0, 0)),
                  pl.BlockSpec((1, d, fc), lambda b, i, k: (layer, 0, k)),
                  pl.BlockSpec((1, fc, d), lambda b, i, k: (layer, k, 0)),
                  pl.BlockSpec((1, d), lambda b, i, k: (0, 0))],
        out_specs=pl.BlockSpec((1, tm, d), row),
        out_shape=jax.ShapeDtypeStruct((bsz, rows_out, d), F32),
        scratch_shapes=[pltpu.VMEM((tm, d), F32), pltpu.VMEM((tm, d), BF16), pltpu.VMEM((tm, d), F32)],
        compiler_params=_cparams(("parallel", "parallel", "arbitrary"), 48),
        name="mix_mlp",
    )(x, mods, ya, hf, hb, mq, yc, w_out, head_avg, w1, w2, final_g)


def _rope_tables(seq, n_ctx):
    half = MLA_ROPE // 2
    row = jnp.repeat(jnp.arange(seq // GRID_W), GRID_W).astype(F32)
    col = jnp.tile(jnp.arange(GRID_W), seq // GRID_W).astype(F32)
    freqs = 1.0 / (ROPE_BASE ** (jnp.arange(0, half, 2, dtype=F32) / half))
    ang = jnp.concatenate([row[:, None] * freqs, col[:, None] * freqs], axis=-1)
    cos = jnp.concatenate([jnp.cos(ang), jnp.ones((n_ctx, half), F32)], axis=0)
    sin = jnp.concatenate([jnp.sin(ang), jnp.zeros((n_ctx, half), F32)], axis=0)
    return jnp.tile(cos, (1, MLA_HEADS)), jnp.tile(sin, (1, MLA_HEADS))


def _block_diag_halves(w):
    depth = w.shape[0]
    per = LANES // LRU_BD
    wh = w.reshape(depth, 2, LRU_BLOCKS // per, per, LRU_BD, LRU_BD)
    eye = jnp.eye(per, dtype=w.dtype)
    bd = jnp.einsum("ldcpio,pq->ldcpiqo", wh, eye).reshape(depth, 2, LRU_BLOCKS // per, LANES, LANES)
    return bd.transpose(0, 2, 1, 3, 4).astype(BF16)


def kernel(x, c, ctx, c_ctx, w_mod, b_mod, w_in, mla_g_q, mla_w_uq, mla_g_kv, mla_w_ukv, ml_gate_bias,
           lru_conv_w, lru_conv_b, lru_w_a, lru_b_a, lru_w_x, lru_b_x, lru_lam, w_out, w_ff1, w_ff2, final_g):
    bsz, seq, d = x.shape
    n_ctx = ctx.shape[1]
    depth = w_in.shape[0]
    assert bsz <= CTX_MOD_ROW and d == D_MODEL and n_ctx % 256 == 0

    zc = lambda n: jnp.zeros((depth, d, n), w_in.dtype)
    ml0, ml1 = MLA_IN, MLA_IN + 4 * ML_W
    w_in_p = jnp.concatenate([w_in[:, :, :ml0], zc(A_W - MLA_IN), w_in[:, :, ml0:ml1],
                              w_in[:, :, ml1:ml1 + 4 * ML_HEADS], zc(MG_W - 4 * ML_HEADS),
                              w_in[:, :, ml1 + 4 * ML_HEADS:]], axis=-1).astype(BF16)
    half = MLA_ROPE // 2
    uq = mla_w_uq.reshape(depth, MLA_Q_RANK, MLA_HEADS, MLA_QK)
    w_uq_t = jnp.concatenate([uq[..., :MLA_NOPE].reshape(depth, MLA_Q_RANK, -1),
                              uq[..., MLA_NOPE:MLA_NOPE + half].reshape(depth, MLA_Q_RANK, -1),
                              uq[..., MLA_NOPE + half:].reshape(depth, MLA_Q_RANK, -1)],
                             axis=-1).astype(BF16).transpose(0, 2, 1)
    ukv = mla_w_ukv.reshape(depth, MLA_KV_RANK, MLA_HEADS, MLA_NOPE + MLA_V)
    w_ukv_k = ukv[..., :MLA_NOPE].reshape(depth, MLA_KV_RANK, -1).astype(BF16)
    w_ukv_vt = ukv[..., MLA_NOPE:].reshape(depth, MLA_KV_RANK, -1).astype(BF16).transpose(0, 2, 1)
    g_q = mla_g_q.reshape(depth, 1, MLA_Q_RANK)
    g_kv = mla_g_kv.reshape(depth, 1, MLA_KV_RANK)
    bias_p = jnp.pad(ml_gate_bias, ((0, 0), (0, MG_W - 4 * ML_HEADS)))
    wa_bd, wx_bd = _block_diag_halves(lru_w_a), _block_diag_halves(lru_w_x)
    vec4 = lambda v: v.reshape(depth, 2, 1, LRU_W)
    conv_b = lru_conv_b.reshape(depth, 1, LRU_W)
    w_out16, w1_16, w2_16 = w_out.astype(BF16), w_ff1.astype(BF16), w_ff2.astype(BF16)
    head_avg = jnp.kron(jnp.eye(ML_HEADS, dtype=F32), jnp.full((ML_DH, ML_DH), 1.0 / ML_DH, F32))
    cos4, sin4 = _rope_tables(seq, n_ctx)
    rope = (cos4, sin4, cos4.T, sin4.T)
    fg = final_g.reshape(1, d)

    cvec = jnp.concatenate([c, jnp.zeros((CTX_MOD_ROW - bsz, d), c.dtype), c_ctx[None, :],
                            jnp.zeros((MOD_ROWS - CTX_MOD_ROW - 1, d), c.dtype)], axis=0)
    mods = _mods_call(cvec, w_mod, b_mod)

    xs = jnp.concatenate([x, ctx], axis=1)
    for l in range(depth):
        last = l == depth - 1
        a, mq, mg, r = _in_call(xs, mods, w_in_p, l, n_ctx)
        qt, k, vt = _prep_call(a, g_q, w_uq_t, g_kv, w_ukv_k, w_ukv_vt, rope, l)
        ya = _attn_call(qt, k, vt, n_ctx, with_ctx=not last)
        hf, hb = _mlstm_call(mq, mg, bias_p[l:l + 1], n_ctx)
        yc = _lru_call(r, lru_conv_w, conv_b, wa_bd, vec4(lru_b_a), wx_bd, vec4(lru_b_x), vec4(lru_lam), l, n_ctx)
        xs = _mix_mlp_call(xs, mods, ya, hf, hb, mq, yc, w_out16, head_avg, w1_16, w2_16, fg, l, n_ctx,
                           final=last)
    return xs
```

```python
import functools

import jax
import jax.numpy as jnp
from jax import lax
from jax.experimental import pallas as pl
from jax.experimental.pallas import tpu as pltpu

F32 = jnp.float32
BF16 = jnp.bfloat16

D_MODEL = 1024
DEPTH = 4
GRID_W = 64
N_CTX = 256
MLA_HEADS = 4
MLA_Q_RANK = 256
MLA_KV_RANK = 128
MLA_NOPE = 128
MLA_ROPE = 64
MLA_V = 128
MLA_QK = MLA_NOPE + MLA_ROPE
MLA_SCALE = MLA_QK ** -0.5
ROPE_BASE = 10000.0
ML_HEADS = 4
ML_DH = 64
ML_W = ML_HEADS * ML_DH
ML_CHUNK = 128
LRU_W = 256
LRU_BLOCKS = 4
LRU_BD = LRU_W // LRU_BLOCKS
CONV_W = 4
CONV_LEFT = 2
LRU_C = 8.0
D_FF = 4 * D_MODEL
EPS = 1e-6
MLA_IN = MLA_Q_RANK + MLA_KV_RANK + MLA_ROPE
ML_IN = 4 * ML_W + 4 * ML_HEADS
LRU_IN = 2 * LRU_W

LANES = 128
SUBLANES = 8
MOD_ROWS = 8
CTX_MOD_ROW = 4

A_W = 512
MQ_W = 4 * ML_W
MG_W = LANES
R_W = 2 * LRU_W
IN_W = A_W + MQ_W + MG_W + R_W


def _cparams(sem, vmem_mb):
    return pltpu.CompilerParams(dimension_semantics=sem, vmem_limit_bytes=vmem_mb * 1024 * 1024)


def _mod_rows(m_ref, b, row0, tm, ctx_start, seg):
    lo, hi = seg * D_MODEL, (seg + 1) * D_MODEL
    lat = m_ref[0, pl.ds(b, 1), lo:hi]
    if ctx_start is None:
        return lat
    ctx = m_ref[0, CTX_MOD_ROW:CTX_MOD_ROW + 1, lo:hi]
    rows = row0 + lax.broadcasted_iota(jnp.int32, (tm, 1), 0)
    return jnp.where(rows >= ctx_start, ctx, lat)


def _pick_tile(n, candidates):
    return next(t for t in candidates if n % t == 0)


def _rms(x):
    return x * lax.rsqrt(jnp.mean(x * x, axis=-1, keepdims=True) + EPS)


def _sigmoid(x):
    return 0.5 * jnp.tanh(0.5 * x) + 0.5


def _dot(a, b):
    return jnp.dot(a, b, preferred_element_type=F32)


def _dot_nt(a, b):
    return lax.dot_general(a, b, (((1,), (1,)), ((), ())), preferred_element_type=F32)


def _mods_body(c_ref, w_ref, b_ref, o_ref):
    cv = c_ref[...]
    act = (cv * jax.nn.sigmoid(cv)).astype(BF16)
    o_ref[0] = _dot(act, w_ref[0].astype(BF16)) + b_ref[0]


def _mods_call(cvec, w_mod, b_mod):
    depth, d, n = w_mod.shape
    tn = 1536
    return pl.pallas_call(
        _mods_body,
        grid=(depth, n // tn),
        in_specs=[pl.BlockSpec((MOD_ROWS, d), lambda l, j: (0, 0)),
                  pl.BlockSpec((1, d, tn), lambda l, j: (l, 0, j)),
                  pl.BlockSpec((1, 1, tn), lambda l, j: (l, 0, j))],
        out_specs=pl.BlockSpec((1, MOD_ROWS, tn), lambda l, j: (l, 0, j)),
        out_shape=jax.ShapeDtypeStruct((depth, MOD_ROWS, n), F32),
        compiler_params=_cparams(("arbitrary", "arbitrary"), 40),
        name="mods",
    )(cvec, w_mod, b_mod.reshape(depth, 1, n))


def _in_body(x_ref, m_ref, w_ref, a_ref, q_ref, g_ref, r_ref, *, tm, ctx_start):
    b, i = pl.program_id(0), pl.program_id(1)
    xn = _rms(x_ref[0])
    shift = _mod_rows(m_ref, b, i * tm, tm, ctx_start, 0)
    scale = _mod_rows(m_ref, b, i * tm, tm, ctx_start, 1)
    u = (xn * (1.0 + scale) + shift).astype(BF16)
    a_ref[0] = _dot(u, w_ref[0, :, 0:A_W])
    q_ref[0] = _dot(u, w_ref[0, :, A_W:A_W + MQ_W])
    g_ref[0] = _dot(u, w_ref[0, :, A_W + MQ_W:A_W + MQ_W + MG_W])
    r_ref[0] = _dot(u, w_ref[0, :, A_W + MQ_W + MG_W:IN_W])


def _in_call(x, mods, w_in_p, layer, n_ctx):
    bsz, s, d = x.shape
    tm = _pick_tile(s, (1088, 256))
    row = lambda b, i: (b, i, 0)
    return pl.pallas_call(
        functools.partial(_in_body, tm=tm, ctx_start=s - n_ctx),
        grid=(bsz, s // tm),
        in_specs=[pl.BlockSpec((1, tm, d), row),
                  pl.BlockSpec((1, MOD_ROWS, 6 * d), lambda b, i: (layer, 0, 0)),
                  pl.BlockSpec((1, d, IN_W), lambda b, i: (layer, 0, 0))],
        out_specs=[pl.BlockSpec((1, tm, A_W), row), pl.BlockSpec((1, tm, MQ_W), row),
                   pl.BlockSpec((1, tm, MG_W), row), pl.BlockSpec((1, tm, R_W), row)],
        out_shape=[jax.ShapeDtypeStruct((bsz, s, w), F32) for w in (A_W, MQ_W, MG_W, R_W)],
        compiler_params=_cparams(("parallel", "parallel"), 52),
        name="in_proj",
    )(x, mods, w_in_p)


LOG2E = 1.4426950408889634


def _prep_body(a_ref, gq_ref, gkv_ref, wuqt_ref, wukvk_ref, wukvvt_ref, cos_ref, sin_ref, cost_ref, sint_ref,
               qt_ref, k_ref, vt_ref):
    for bb in range(a_ref.shape[0]):
        _prep_sample(bb, a_ref, gq_ref, gkv_ref, wuqt_ref, wukvk_ref, wukvvt_ref, cos_ref, sin_ref, cost_ref,
                     sint_ref, qt_ref, k_ref, vt_ref)


def _prep_sample(bb, a_ref, gq_ref, gkv_ref, wuqt_ref, wukvk_ref, wukvvt_ref, cos_ref, sin_ref, cost_ref, sint_ref,
                 qt_ref, k_ref, vt_ref):
    a = a_ref[bb]
    nq, nkv = MLA_Q_RANK, MLA_Q_RANK + MLA_KV_RANK
    half = MLA_ROPE // 2
    hn = MLA_HEADS * MLA_NOPE
    cq_t = (_rms(a[:, 0:nq]) * gq_ref[0]).T.astype(BF16)
    ckv = _rms(a[:, nq:nkv]) * gkv_ref[0]
    q_t = _dot(wuqt_ref[0], cq_t) * (MLA_SCALE * LOG2E)
    x1, x2 = q_t[hn:hn + LANES], q_t[hn + LANES:hn + 2 * LANES]
    cos_t, sin_t = cost_ref[...], sint_ref[...]
    r1 = (x1 * cos_t - x2 * sin_t).astype(BF16)
    r2 = (x1 * sin_t + x2 * cos_t).astype(BF16)
    k_nope = _dot(ckv.astype(BF16), wukvk_ref[0])
    v_t = _dot(wukvvt_ref[0], ckv.T.astype(BF16))
    k1, k2 = a[:, nkv:nkv + half], a[:, nkv + half:nkv + 2 * half]
    c32, s32 = cos_ref[:, 0:half], sin_ref[:, 0:half]
    kr1 = (k1 * c32 - k2 * s32).astype(BF16)
    kr2 = (k1 * s32 + k2 * c32).astype(BF16)
    for h in range(MLA_HEADS):
        qt_ref[bb, h, 0:MLA_NOPE, :] = q_t[h * MLA_NOPE:(h + 1) * MLA_NOPE].astype(BF16)
        qt_ref[bb, h, MLA_NOPE:MLA_NOPE + half, :] = r1[h * half:(h + 1) * half]
        qt_ref[bb, h, MLA_NOPE + half:MLA_QK, :] = r2[h * half:(h + 1) * half]
        k_ref[bb, h, :, 0:MLA_NOPE] = k_nope[:, h * MLA_NOPE:(h + 1) * MLA_NOPE].astype(BF16)
        k_ref[bb, h, :, MLA_NOPE:MLA_NOPE + half] = kr1
        k_ref[bb, h, :, MLA_NOPE + half:MLA_QK] = kr2
        vt_ref[bb, h] = v_t[h * MLA_V:(h + 1) * MLA_V].astype(BF16)


def _prep_call(a, g_q, w_uq_t, g_kv, w_ukv_k, w_ukv_vt, rope, layer):
    bsz, s, _ = a.shape
    tm = 256
    lsel = lambda i: (layer, 0, 0)
    cos4, sin4, cos4_t, sin4_t = rope
    return pl.pallas_call(
        _prep_body,
        grid=(s // tm,),
        in_specs=[pl.BlockSpec((bsz, tm, A_W), lambda i: (0, i, 0)),
                  pl.BlockSpec((1, 1, MLA_Q_RANK), lsel),
                  pl.BlockSpec((1, 1, MLA_KV_RANK), lsel),
                  pl.BlockSpec((1, MLA_HEADS * MLA_QK, MLA_Q_RANK), lsel),
                  pl.BlockSpec((1, MLA_KV_RANK, MLA_HEADS * MLA_NOPE), lsel),
                  pl.BlockSpec((1, MLA_HEADS * MLA_V, MLA_KV_RANK), lsel),
                  pl.BlockSpec((tm, LANES), lambda i: (i, 0)),
                  pl.BlockSpec((tm, LANES), lambda i: (i, 0)),
                  pl.BlockSpec((LANES, tm), lambda i: (0, i)),
                  pl.BlockSpec((LANES, tm), lambda i: (0, i))],
        out_specs=[pl.BlockSpec((bsz, MLA_HEADS, MLA_QK, tm), lambda i: (0, 0, 0, i)),
                   pl.BlockSpec((bsz, MLA_HEADS, tm, MLA_QK), lambda i: (0, 0, i, 0)),
                   pl.BlockSpec((bsz, MLA_HEADS, MLA_V, tm), lambda i: (0, 0, 0, i))],
        out_shape=[jax.ShapeDtypeStruct((bsz, MLA_HEADS, MLA_QK, s), BF16),
                   jax.ShapeDtypeStruct((bsz, MLA_HEADS, s, MLA_QK), BF16),
                   jax.ShapeDtypeStruct((bsz, MLA_HEADS, MLA_V, s), BF16)],
        compiler_params=_cparams(("parallel",), 40),
        name="mla_prep",
    )(a, g_q, g_kv, w_uq_t, w_ukv_k, w_ukv_vt, cos4, sin4, cos4_t, sin4_t)


def _attn_body(qt_ref, k_ref, vt_ref, o_ref, *, subs, zero_rows):
    work = [(q0, tq, c, c is chunks[0], c is chunks[-1]) for q0, tq, chunks in subs for c in chunks]
    score = lambda w: _dot(k_ref[0, 0, w[2][0]:w[2][0] + w[2][1], :],
                           qt_ref[0, 0, :, w[0]:w[0] + w[1]])
    pending = [score(w) for w in work[:ATTN_AHEAD]]
    m = l = acc = None
    for idx, (q0, tq, (start, size), first, last) in enumerate(work):
        if first:
            m = jnp.full((1, tq), -jnp.inf, F32)
            l = jnp.zeros((1, tq), F32)
            acc = jnp.zeros((MLA_V, tq), F32)
        st = pending.pop(0)
        if idx + ATTN_AHEAD < len(work):
            pending.append(score(work[idx + ATTN_AHEAD]))
        m_new = jnp.maximum(m, jnp.max(st, axis=0, keepdims=True))
        p = jnp.exp2(st - m_new)
        alpha = jnp.exp2(m - m_new)
        l = alpha * l + jnp.sum(p, axis=0, keepdims=True)
        acc = alpha * acc + _dot(vt_ref[0, 0, :, start:start + size], p.astype(BF16))
        m = m_new
        if last:
            o_ref[0, q0:q0 + tq, :] = (acc / l).T.astype(o_ref.dtype)
    if zero_rows is not None:
        o_ref[0, zero_rows[0]:zero_rows[1], :] = jnp.zeros((zero_rows[1] - zero_rows[0], MLA_V), o_ref.dtype)


ATTN_TQ = 512
ATTN_TK = 512
ATTN_AHEAD = 2


def _attn_call(qt, k, vt, n_ctx, with_ctx):
    bsz, nh, s, dk = k.shape
    n_lat = s - n_ctx
    assert n_lat % ATTN_TQ == 0 and n_lat % ATTN_TK == 0
    ctx_chunk = (n_lat, n_ctx)
    lat_chunks = (ctx_chunk,) + tuple((j * ATTN_TK, ATTN_TK) for j in range(n_lat // ATTN_TK))
    subs = [(j * ATTN_TQ, ATTN_TQ, lat_chunks) for j in range(n_lat // ATTN_TQ)]
    if with_ctx:
        subs.append((n_lat, n_ctx, (ctx_chunk,)))
    return pl.pallas_call(
        functools.partial(_attn_body, subs=tuple(subs), zero_rows=None if with_ctx else (n_lat, s)),
        grid=(bsz, nh),
        in_specs=[pl.BlockSpec((1, 1, dk, s), lambda b, h: (b, h, 0, 0)),
                  pl.BlockSpec((1, 1, s, dk), lambda b, h: (b, h, 0, 0)),
                  pl.BlockSpec((1, 1, MLA_V, s), lambda b, h: (b, h, 0, 0))],
        out_specs=pl.BlockSpec((1, s, MLA_V), lambda b, h: (b, 0, h)),
        out_shape=jax.ShapeDtypeStruct((bsz, s, nh * MLA_V), BF16),
        compiler_params=_cparams(("parallel", "parallel"), 48),
        name="mla_attn",
    )(qt, k, vt)


ML_BLOCK_CHUNKS = 2


def _mlstm_body(xf_ref, gf_ref, xb_ref, gb_ref, bias_ref, hf_ref, hb_ref, c_ref, m_ref):
    lc = ML_CHUNK
    assert lc == LANES

    @pl.when(pl.program_id(1) == 0)
    def _():
        c_ref[...] = jnp.zeros_like(c_ref)
        m_ref[...] = jnp.zeros_like(m_ref)

    s_io = lax.broadcasted_iota(jnp.int32, (lc, lc), 0)
    t_io = lax.broadcasted_iota(jnp.int32, (lc, lc), 1)
    lane = lax.broadcasted_iota(jnp.int32, (lc, LANES), 1)
    row = lax.broadcasted_iota(jnp.int32, (LANES, lc), 0)
    row8 = lax.broadcasted_iota(jnp.int32, (SUBLANES, lc), 0)
    bias = bias_ref[...]
    ngate = 4 * ML_HEADS
    ones_sq = jnp.ones((lc, lc), BF16)
    nsub = xf_ref.shape[1] // lc
    for t in range(nsub):
        _mlstm_chunk((xf_ref, xb_ref), (gf_ref, gb_ref), (hf_ref, hb_ref), c_ref, m_ref,
                     (t * lc, (nsub - 1 - t) * lc), bias, ones_sq, (s_io, t_io, lane, row, row8))


def _mlstm_chunk(x_refs, g_refs, o_refs, c_ref, m_ref, row0, bias, ones_sq, iotas):
    lc = ML_CHUNK
    ngate = 4 * ML_HEADS
    s_io, t_io, lane, row, row8 = iotas
    probs = []
    c_rows = jnp.zeros((SUBLANES, lc), F32)
    for d in range(2):
        x_ref, g_ref = x_refs[d], g_refs[d]
        rows = slice(row0[d], row0[d] + lc)
        mask = (s_io <= t_io) if d == 0 else (s_io >= t_io)
        gt = (g_ref[0, rows, :] + bias).T[0:ngate]
        lf = jax.nn.log_sigmoid(gt)
        hi = lf.astype(BF16)
        r1 = lf - hi.astype(F32)
        mid = r1.astype(BF16)
        lo = (r1 - mid.astype(F32)).astype(BF16)
        sums = _dot(jnp.concatenate([hi, mid, lo], axis=0),
                    jnp.concatenate([mask.astype(BF16), ones_sq], axis=1))
        sums = sums[0:ngate] + sums[ngate:2 * ngate] + sums[2 * ngate:3 * ngate]
        b_run, b_tot = sums[:, 0:lc], sums[:, lc:2 * lc]
        x = x_ref[0, rows, :]
        for pair in range(ML_HEADS // 2):
            qs = x[:, pair * LANES:(pair + 1) * LANES] * (ML_DH ** -0.5)
            ks = x[:, ML_W + pair * LANES:ML_W + (pair + 1) * LANES]
            vt = x[:, 2 * ML_W + pair * LANES:2 * ML_W + (pair + 1) * LANES].T
            for odd in range(2):
                h = 2 * pair + odd
                ci, cf = d * 2 * ML_HEADS + h, d * 2 * ML_HEADS + ML_HEADS + h
                own = (lane >= ML_DH) if odd else (lane < ML_DH)
                own_r = (row >= ML_DH) if odd else (row < ML_DH)
                den_row = 0 if odd else ML_DH
                j = d * ML_HEADS + h
                brow, irow = b_run[cf:cf + 1], gt[ci:ci + 1]
                c_rows = jnp.where(row8 == j, brow - irow, c_rows)
                probs.append(dict(
                    j=j, d=d, pair=pair, odd=odd, mask=mask, den_row=den_row, brow=brow, irow=irow,
                    btot=b_tot[cf:cf + 1],
                    qm=jnp.where(own, qs, 0.0).astype(BF16), ks=ks.astype(BF16),
                    km=jnp.where(own, ks, 0.0).astype(BF16),
                    vaug=jnp.where(own_r, vt, jnp.where(row == den_row, 1.0, 0.0))))

    c_cols = jnp.concatenate([c_rows, jnp.zeros((LANES - SUBLANES, lc), F32)], axis=0).T

    for p in probs:
        p["c_old"] = c_ref[p["j"]]
        p["kq"] = _dot_nt(p["ks"], p["qm"])
        p["inter"] = _dot_nt(p["c_old"].astype(BF16), p["qm"])
    for p in probs:
        j = p["j"]
        m_prev = m_ref[j:j + 1, :]
        dt = jnp.where(p["mask"], p["brow"] - c_cols[:, j:j + 1], -jnp.inf)
        inter_m = p["brow"] + m_prev
        m_row = jnp.maximum(inter_m, jnp.max(dt, axis=0, keepdims=True))
        p["st"] = (p["kq"] * jnp.exp(dt - m_row)).astype(BF16)
        p["w_inter"] = jnp.exp(inter_m - m_row)
        p["floor"] = jnp.exp(-m_row)
        grow = p["btot"] - p["brow"] + p["irow"]
        m_new = jnp.maximum(p["btot"] + m_prev, jnp.max(grow, axis=1, keepdims=True))
        p["w_old"] = jnp.exp(p["btot"] + m_prev - m_new)
        p["wv"] = (p["vaug"] * jnp.exp(grow - m_new)).astype(BF16)
        m_ref[j:j + 1, :] = m_new
    outs = {}
    for p in probs:
        ht = _dot(p["vaug"].astype(BF16), p["st"]) + p["w_inter"] * p["inter"]
        den = ht[p["den_row"]:p["den_row"] + 1]
        outs[(p["d"], p["pair"], p["odd"])] = ht / jnp.maximum(jnp.abs(den), p["floor"])
    for d, o_ref in enumerate(o_refs):
        for pair in range(ML_HEADS // 2):
            both = jnp.where(row < ML_DH, outs[(d, pair, 0)], outs[(d, pair, 1)])
            o_ref[0, row0[d]:row0[d] + lc, pair * LANES:(pair + 1) * LANES] = both.T
    for p in probs:
        c_ref[p["j"]] = p["w_old"] * p["c_old"] + _dot(p["wv"], p["km"])


def _mlstm_call(mq, mg, bias_p, n_ctx):
    bsz, s, _ = mq.shape
    rows = ML_BLOCK_CHUNKS * ML_CHUNK
    assert n_ctx % rows == 0 and s % rows == 0
    nch, ncc = s // rows, n_ctx // rows
    nlc = nch - ncc
    fwd = lambda b, j: (b, jnp.where(j < ncc, nlc + j, j - ncc), 0)
    bwd = lambda b, j: (b, nch - 1 - j, 0)
    out = jax.ShapeDtypeStruct((bsz, s, ML_W), F32)
    return pl.pallas_call(
        _mlstm_body,
        grid=(bsz, nch),
        in_specs=[pl.BlockSpec((1, rows, MQ_W), fwd), pl.BlockSpec((1, rows, MG_W), fwd),
                  pl.BlockSpec((1, rows, MQ_W), bwd), pl.BlockSpec((1, rows, MG_W), bwd),
                  pl.BlockSpec((1, MG_W), lambda b, j: (0, 0))],
        out_specs=[pl.BlockSpec((1, rows, ML_W), fwd), pl.BlockSpec((1, rows, ML_W), bwd)],
        out_shape=[out, out],
        scratch_shapes=[pltpu.VMEM((2 * ML_HEADS, LANES, LANES), F32),
                        pltpu.VMEM((2 * ML_HEADS, LANES), F32)],
        compiler_params=_cparams(("parallel", "arbitrary"), 40),
        name="mlstm",
    )(mq, mg, mq, mg, bias_p)


LRU_PITCH_PAD = 8
LRU_UNROLL = 8


def _lru_body(xb_ref, gb_ref, cw_ref, cb_ref, wa_ref, ba_ref, wx_ref, bx_ref, lam_ref, y_ref,
              xp_ref, xs_ref, a_ref, u_ref, *, n_ctx, s_len):
    n_lat = s_len - n_ctx
    pad = SUBLANES
    lat_off = n_ctx + 2 * pad
    zeros = jnp.zeros((pad, LANES), F32)
    xp_ref[0:pad, :] = zeros
    xp_ref[pad + n_ctx:lat_off, :] = zeros
    xp_ref[lat_off + n_lat:lat_off + n_lat + pad, :] = zeros
    xp_ref[pad:pad + n_ctx, :] = xb_ref[0, n_lat:n_lat + n_ctx, :]
    cchunk = 512

    def copy_body(c, _):
        src = pl.multiple_of(c * cchunk, SUBLANES)
        dst = pl.multiple_of(lat_off + c * cchunk, SUBLANES)
        xp_ref[pl.ds(dst, cchunk), :] = xb_ref[0, pl.ds(src, cchunk), :]
        return 0

    lax.fori_loop(0, n_lat // cchunk, copy_body, 0)

    cw = cw_ref[...]
    cb = cb_ref[...]

    def conv(src0, dst0, n):
        acc = cb + xp_ref[src0 - CONV_LEFT:src0 - CONV_LEFT + n, :] * cw[0:1, :]
        for j in range(1, CONV_W):
            acc = acc + xp_ref[src0 - CONV_LEFT + j:src0 - CONV_LEFT + j + n, :] * cw[j:j + 1, :]
        xs_ref[dst0:dst0 + n, :] = acc

    conv(pad, n_lat, n_ctx)
    for c in range(n_lat // cchunk):
        conv(lat_off + c * cchunk, c * cchunk, cchunk)

    seg_lat = n_lat // SUBLANES
    seg_ctx = n_ctx // SUBLANES
    p_lat, p_ctx = seg_lat + LRU_PITCH_PAD, seg_ctx + LRU_PITCH_PAD
    ctx_base = SUBLANES * p_lat
    row_io = lax.broadcasted_iota(jnp.int32, (SUBLANES, LANES), 0)

    def gates(x, d):
        xb16 = x.astype(BF16)
        r = _sigmoid(_dot(xb16, wa_ref[d]) + ba_ref[d])
        i = _sigmoid(_dot(xb16, wx_ref[d]) + bx_ref[d])
        log_a = (-LRU_C) * r * jax.nn.softplus(-lam_ref[d])
        a = jnp.exp(log_a)
        return a, jnp.sqrt(jnp.tanh(-log_a) * (1.0 + a * a)) * (i * x)

    def fill(d):
        def lat_body(r, _):
            src = pl.multiple_of(r * seg_lat, SUBLANES)
            dst = pl.multiple_of(r * p_lat, SUBLANES)
            a, u = gates(xs_ref[pl.ds(src, seg_lat), :], d)
            a_ref[d, pl.ds(dst, seg_lat), :] = a
            u_ref[d, pl.ds(dst, seg_lat), :] = u
            return 0

        lax.fori_loop(0, SUBLANES, lat_body, 0)
        a, u = gates(xs_ref[n_lat:n_lat + n_ctx, :], d)
        for r in range(SUBLANES):
            a_ref[d, ctx_base + r * p_ctx:ctx_base + r * p_ctx + seg_ctx, :] = a[r * seg_ctx:(r + 1) * seg_ctx, :]
            u_ref[d, ctx_base + r * p_ctx:ctx_base + r * p_ctx + seg_ctx, :] = u[r * seg_ctx:(r + 1) * seg_ctx, :]

    def scan(base, n, pitch, h0s):
        def block(tb, carry):
            idx = [[pl.ds(base + (tb * LRU_UNROLL + k if d == 0 else n - 1 - tb * LRU_UNROLL - k),
                          SUBLANES, stride=pitch) for k in range(LRU_UNROLL)] for d in range(2)]
            av = [[a_ref[d, i, :] for i in idx[d]] for d in range(2)]
            uv = [[u_ref[d, i, :] for i in idx[d]] for d in range(2)]
            carry = list(carry)
            for k in range(LRU_UNROLL):
                for d in range(2):
                    h, acum = carry[d]
                    h = av[d][k] * h + uv[d][k]
                    acum = acum * av[d][k]
                    carry[d] = (h, acum)
                    uv[d][k], av[d][k] = h, acum
            for d in range(2):
                for k in range(LRU_UNROLL):
                    u_ref[d, idx[d][k], :] = uv[d][k]
                    a_ref[d, idx[d][k], :] = av[d][k]
            return tuple(carry)

        init = (jnp.zeros((SUBLANES, LANES), F32), jnp.ones((SUBLANES, LANES), F32))
        ends = lax.fori_loop(0, n // LRU_UNROLL, block, (init, init))
        result = []
        for d in range(2):
            h_end, a_end = ends[d]
            carry = h0s[d]
            cvec = jnp.zeros((SUBLANES, LANES), F32)
            for r in (range(SUBLANES) if d == 0 else range(SUBLANES - 1, -1, -1)):
                cvec = jnp.where(row_io == r, carry, cvec)
                carry = h_end[r:r + 1, :] + a_end[r:r + 1, :] * carry
            result.append((cvec, carry))
        return result

    fill(0)
    fill(1)
    zero_state = jnp.zeros((1, LANES), F32)
    ctx_res = scan(ctx_base, seg_ctx, p_ctx, (zero_state, zero_state))
    lat_res = scan(0, seg_lat, p_lat, (ctx_res[0][1], ctx_res[1][1]))
    carries = {(d, "ctx"): ctx_res[d][0] for d in range(2)}
    carries.update({(d, "lat"): lat_res[d][0] for d in range(2)})

    def emit(kind, r, dst0, src0, n):
        hsum = None
        for d in range(2):
            c_in = carries[d, kind][r:r + 1, :]
            part = u_ref[d, src0:src0 + n, :] + a_ref[d, src0:src0 + n, :] * c_in
            hsum = part if hsum is None else hsum + part
        y_ref[0, dst0:dst0 + n, :] = jax.nn.gelu(gb_ref[0, dst0:dst0 + n, :]) * hsum

    for r in range(SUBLANES):
        emit("lat", r, r * seg_lat, r * p_lat, seg_lat)
        emit("ctx", r, n_lat + r * seg_ctx, ctx_base + r * p_ctx, seg_ctx)


def _lru_call(r, conv_w, conv_b, wa_bd, b_a, wx_bd, b_x, lam, layer, n_ctx):
    bsz, s, _ = r.shape
    nh = LRU_W // LANES
    n_lat = s - n_ctx
    scan_rows = SUBLANES * (n_lat // SUBLANES + LRU_PITCH_PAD) + SUBLANES * (n_ctx // SUBLANES + LRU_PITCH_PAD)
    vec = lambda b, c: (layer, 0, 0, c)
    return pl.pallas_call(
        functools.partial(_lru_body, n_ctx=n_ctx, s_len=s),
        grid=(bsz, nh),
        in_specs=[pl.BlockSpec((1, s, LANES), lambda b, c: (b, 0, c)),
                  pl.BlockSpec((1, s, LANES), lambda b, c: (b, 0, nh + c)),
                  pl.BlockSpec((None, CONV_W, LANES), lambda b, c: (layer, 0, c)),
                  pl.BlockSpec((None, 1, LANES), lambda b, c: (layer, 0, c)),
                  pl.BlockSpec((None, None, 2, LANES, LANES), lambda b, c: (layer, c, 0, 0, 0)),
                  pl.BlockSpec((None, 2, 1, LANES), vec),
                  pl.BlockSpec((None, None, 2, LANES, LANES), lambda b, c: (layer, c, 0, 0, 0)),
                  pl.BlockSpec((None, 2, 1, LANES), vec),
                  pl.BlockSpec((None, 2, 1, LANES), vec)],
        out_specs=pl.BlockSpec((1, s, LANES), lambda b, c: (b, 0, c)),
        out_shape=jax.ShapeDtypeStruct((bsz, s, LRU_W), F32),
        scratch_shapes=[pltpu.VMEM((s + 3 * SUBLANES, LANES), F32),
                        pltpu.VMEM((s, LANES), F32),
                        pltpu.VMEM((2, scan_rows, LANES), F32),
                        pltpu.VMEM((2, scan_rows, LANES), F32)],
        compiler_params=_cparams(("parallel", "parallel"), 48),
        name="rglru",
    )(r, r, conv_w, conv_b, wa_bd, b_a, wx_bd, b_x, lam)


def _mix_mlp_body(x_ref, m_ref, ya_ref, hf_ref, hb_ref, og_ref, yc_ref, wo_ref, hn_ref, w1_ref, w2_ref, fg_ref,
                  o_ref, x1_ref, u_ref, acc_ref, *, tm, ctx_start, final):
    b, i, k = pl.program_id(0), pl.program_id(1), pl.program_id(2)

    @pl.when(k == 0)
    def _():
        hsum = hf_ref[0] + hb_ref[0]
        msq = jnp.dot(hsum * hsum, hn_ref[...], precision=lax.Precision.HIGHEST, preferred_element_type=F32)
        yb = (_sigmoid(og_ref[0]) * (hsum * lax.rsqrt(msq + EPS))).astype(BF16)
        na, nb = MLA_HEADS * MLA_V, MLA_HEADS * MLA_V + ML_W
        y = (_dot(ya_ref[0], wo_ref[0, 0:na, :]) + _dot(yb, wo_ref[0, na:nb, :])
             + _dot(yc_ref[0].astype(BF16), wo_ref[0, nb:, :]))
        x1 = x_ref[0] + _mod_rows(m_ref, b, i * tm, tm, ctx_start, 2) * y
        x1_ref[...] = x1
        shift = _mod_rows(m_ref, b, i * tm, tm, ctx_start, 3)
        scale = _mod_rows(m_ref, b, i * tm, tm, ctx_start, 4)
        u_ref[...] = (_rms(x1) * (1.0 + scale) + shift).astype(BF16)
        acc_ref[...] = jnp.zeros_like(acc_ref)

    hid = jnp.maximum(_dot(u_ref[...], w1_ref[0]), 0.0)
    acc_ref[...] += _dot((hid * hid).astype(BF16), w2_ref[0])

    @pl.when(k == pl.num_programs(2) - 1)
    def _():
        res = x1_ref[...] + _mod_rows(m_ref, b, i * tm, tm, ctx_start, 5) * acc_ref[...]
        if final:
            res = _rms(res) * fg_ref[...]
        o_ref[0] = res


MLP_FF_CHUNK = 2048


def _mix_mlp_call(x, mods, ya, hf, hb, mq, yc, w_out, head_avg, w1, w2, final_g, layer, n_ctx, final):
    bsz, s, d = x.shape
    rows_out = s - n_ctx if final else s
    tm = _pick_tile(rows_out, (544, 512, 256))
    fc = MLP_FF_CHUNK
    row = lambda b, i, k: (b, i, 0)
    lsel = lambda b, i, k: (layer, 0, 0)
    return pl.pallas_call(
        functools.partial(_mix_mlp_body, tm=tm, ctx_start=None if final else s - n_ctx, final=final),
        grid=(bsz, rows_out // tm, D_FF // fc),
        in_specs=[pl.BlockSpec((1, tm, d), row),
                  pl.BlockSpec((1, MOD_ROWS, 6 * d), lsel),
                  pl.BlockSpec((1, tm, MLA_HEADS * MLA_V), row),
                  pl.BlockSpec((1, tm, ML_W), row),
                  pl.BlockSpec((1, tm, ML_W), row),
                  pl.BlockSpec((1, tm, ML_W), lambda b, i, k: (b, i, 3)),
                  pl.BlockSpec((1, tm, LRU_W), row),
                  pl.BlockSpec((1, d, d), lsel),
                  pl.BlockSpec((ML_W, ML_W), lambda b, i, k: (0, 0)),
                  pl.BlockSpec((1, d, fc), lambda b, i, k: (layer, 0, k)),
                  pl.BlockSpec((1, fc, d), lambda b, i, k: (layer, k, 0)),
                  pl.BlockSpec((1, d), lambda b, i, k: (0, 0))],
        out_specs=pl.BlockSpec((1, tm, d), row),
        out_shape=jax.ShapeDtypeStruct((bsz, rows_out, d), F32),
        scratch_shapes=[pltpu.VMEM((tm, d), F32), pltpu.VMEM((tm, d), BF16), pltpu.VMEM((tm, d), F32)],
        compiler_params=_cparams(("parallel", "parallel", "arbitrary"), 48),
        name="mix_mlp",
    )(x, mods, ya, hf, hb, mq, yc, w_out, head_avg, w1, w2, final_g)


def _rope_tables(seq, n_ctx):
    half = MLA_ROPE // 2
    row = jnp.repeat(jnp.arange(seq // GRID_W), GRID_W).astype(F32)
    col = jnp.tile(jnp.arange(GRID_W), seq // GRID_W).astype(F32)
    freqs = 1.0 / (ROPE_BASE ** (jnp.arange(0, half, 2, dtype=F32) / half))
    ang = jnp.concatenate([row[:, None] * freqs, col[:, None] * freqs], axis=-1)
    cos = jnp.concatenate([jnp.cos(ang), jnp.ones((n_ctx, half), F32)], axis=0)
    sin = jnp.concatenate([jnp.sin(ang), jnp.zeros((n_ctx, half), F32)], axis=0)
    return jnp.tile(cos, (1, MLA_HEADS)), jnp.tile(sin, (1, MLA_HEADS))


def _block_diag_halves(w):
    depth = w.shape[0]
    per = LANES // LRU_BD
    wh = w.reshape(depth, 2, LRU_BLOCKS // per, per, LRU_BD, LRU_BD)
    eye = jnp.eye(per, dtype=w.dtype)
    bd = jnp.einsum("ldcpio,pq->ldcpiqo", wh, eye).reshape(depth, 2, LRU_BLOCKS // per, LANES, LANES)
    return bd.transpose(0, 2, 1, 3, 4).astype(BF16)


def kernel(x, c, ctx, c_ctx, w_mod, b_mod, w_in, mla_g_q, mla_w_uq, mla_g_kv, mla_w_ukv, ml_gate_bias,
           lru_conv_w, lru_conv_b, lru_w_a, lru_b_a, lru_w_x, lru_b_x, lru_lam, w_out, w_ff1, w_ff2, final_g):
    bsz, seq, d = x.shape
    n_ctx = ctx.shape[1]
    depth = w_in.shape[0]
    assert bsz <= CTX_MOD_ROW and d == D_MODEL and n_ctx % 256 == 0

    zc = lambda n: jnp.zeros((depth, d, n), w_in.dtype)
    ml0, ml1 = MLA_IN, MLA_IN + 4 * ML_W
    w_in_p = jnp.concatenate([w_in[:, :, :ml0], zc(A_W - MLA_IN), w_in[:, :, ml0:ml1],
                              w_in[:, :, ml1:ml1 + 4 * ML_HEADS], zc(MG_W - 4 * ML_HEADS),
                              w_in[:, :, ml1 + 4 * ML_HEADS:]], axis=-1).astype(BF16)
    half = MLA_ROPE // 2
    uq = mla_w_uq.reshape(depth, MLA_Q_RANK, MLA_HEADS, MLA_QK)
    w_uq_t = jnp.concatenate([uq[..., :MLA_NOPE].reshape(depth, MLA_Q_RANK, -1),
                              uq[..., MLA_NOPE:MLA_NOPE + half].reshape(depth, MLA_Q_RANK, -1),
                              uq[..., MLA_NOPE + half:].reshape(depth, MLA_Q_RANK, -1)],
                             axis=-1).astype(BF16).transpose(0, 2, 1)
    ukv = mla_w_ukv.reshape(depth, MLA_KV_RANK, MLA_HEADS, MLA_NOPE + MLA_V)
    w_ukv_k = ukv[..., :MLA_NOPE].reshape(depth, MLA_KV_RANK, -1).astype(BF16)
    w_ukv_vt = ukv[..., MLA_NOPE:].reshape(depth, MLA_KV_RANK, -1).astype(BF16).transpose(0, 2, 1)
    g_q = mla_g_q.reshape(depth, 1, MLA_Q_RANK)
    g_kv = mla_g_kv.reshape(depth, 1, MLA_KV_RANK)
    bias_p = jnp.pad(ml_gate_bias, ((0, 0), (0, MG_W - 4 * ML_HEADS)))
    wa_bd, wx_bd = _block_diag_halves(lru_w_a), _block_diag_halves(lru_w_x)
    vec4 = lambda v: v.reshape(depth, 2, 1, LRU_W)
    conv_b = lru_conv_b.reshape(depth, 1, LRU_W)
    w_out16, w1_16, w2_16 = w_out.astype(BF16), w_ff1.astype(BF16), w_ff2.astype(BF16)
    head_avg = jnp.kron(jnp.eye(ML_HEADS, dtype=F32), jnp.full((ML_DH, ML_DH), 1.0 / ML_DH, F32))
    cos4, sin4 = _rope_tables(seq, n_ctx)
    rope = (cos4, sin4, cos4.T, sin4.T)
    fg = final_g.reshape(1, d)

    cvec = jnp.concatenate([c, jnp.zeros((CTX_MOD_ROW - bsz, d), c.dtype), c_ctx[None, :],
                            jnp.zeros((MOD_ROWS - CTX_MOD_ROW - 1, d), c.dtype)], axis=0)
    mods = _mods_call(cvec, w_mod, b_mod)

    xs = jnp.concatenate([x, ctx], axis=1)
    for l in range(depth):
        last = l == depth - 1
        a, mq, mg, r = _in_call(xs, mods, w_in_p, l, n_ctx)
        qt, k, vt = _prep_call(a, g_q, w_uq_t, g_kv, w_ukv_k, w_ukv_vt, rope, l)
        ya = _attn_call(qt, k, vt, n_ctx, with_ctx=not last)
        hf, hb = _mlstm_call(mq, mg, bias_p[l:l + 1], n_ctx)
        yc = _lru_call(r, lru_conv_w, conv_b, wa_bd, vec4(lru_b_a), wx_bd, vec4(lru_b_x), vec4(lru_lam), l, n_ctx)
        xs = _mix_mlp_call(xs, mods, ya, hf, hb, mq, yc, w_out16, head_avg, w1_16, w2_16, fg, l, n_ctx,
                           final=last)
    return xs
```

```python
import functools

import jax
import jax.numpy as jnp
from jax import lax
from jax.experimental import pallas as pl
from jax.experimental.pallas import tpu as pltpu

F32 = jnp.float32
BF16 = jnp.bfloat16

D_MODEL = 1024
DEPTH = 4
GRID_W = 64
N_CTX = 256
MLA_HEADS = 4
MLA_Q_RANK = 256
MLA_KV_RANK = 128
MLA_NOPE = 128
MLA_ROPE = 64
MLA_V = 128
MLA_QK = MLA_NOPE + MLA_ROPE
MLA_SCALE = MLA_QK ** -0.5
ROPE_BASE = 10000.0
ML_HEADS = 4
ML_DH = 64
ML_W = ML_HEADS * ML_DH
ML_CHUNK = 128
LRU_W = 256
LRU_BLOCKS = 4
LRU_BD = LRU_W // LRU_BLOCKS
CONV_W = 4
CONV_LEFT = 2
LRU_C = 8.0
D_FF = 4 * D_MODEL
EPS = 1e-6
MLA_IN = MLA_Q_RANK + MLA_KV_RANK + MLA_ROPE
ML_IN = 4 * ML_W + 4 * ML_HEADS
LRU_IN = 2 * LRU_W

LANES = 128
SUBLANES = 8
MOD_ROWS = 8
CTX_MOD_ROW = 4

A_W = 512
MQ_W = 4 * ML_W
MG_W = LANES
R_W = 2 * LRU_W
IN_W = A_W + MQ_W + MG_W + R_W


def _cparams(sem, vmem_mb):
    return pltpu.CompilerParams(dimension_semantics=sem, vmem_limit_bytes=vmem_mb * 1024 * 1024)


def _mod_rows(m_ref, b, row0, tm, ctx_start, seg):
    lo, hi = seg * D_MODEL, (seg + 1) * D_MODEL
    lat = m_ref[0, pl.ds(b, 1), lo:hi]
    if ctx_start is None:
        return lat
    ctx = m_ref[0, CTX_MOD_ROW:CTX_MOD_ROW + 1, lo:hi]
    rows = row0 + lax.broadcasted_iota(jnp.int32, (tm, 1), 0)
    return jnp.where(rows >= ctx_start, ctx, lat)


def _pick_tile(n, candidates):
    return next(t for t in candidates if n % t == 0)


def _rms(x):
    return x * lax.rsqrt(jnp.mean(x * x, axis=-1, keepdims=True) + EPS)


def _sigmoid(x):
    return 0.5 * jnp.tanh(0.5 * x) + 0.5


def _dot(a, b):
    return jnp.dot(a, b, preferred_element_type=F32)


def _dot_nt(a, b):
    return lax.dot_general(a, b, (((1,), (1,)), ((), ())), preferred_element_type=F32)


def _mods_body(c_ref, w_ref, b_ref, o_ref):
    cv = c_ref[...]
    act = (cv * jax.nn.sigmoid(cv)).astype(BF16)
    o_ref[0] = _dot(act, w_ref[0].astype(BF16)) + b_ref[0]


def _mods_call(cvec, w_mod, b_mod):
    depth, d, n = w_mod.shape
    tn = 1536
    return pl.pallas_call(
        _mods_body,
        grid=(depth, n // tn),
        in_specs=[pl.BlockSpec((MOD_ROWS, d), lambda l, j: (0, 0)),
                  pl.BlockSpec((1, d, tn), lambda l, j: (l, 0, j)),
                  pl.BlockSpec((1, 1, tn), lambda l, j: (l, 0, j))],
        out_specs=pl.BlockSpec((1, MOD_ROWS, tn), lambda l, j: (l, 0, j)),
        out_shape=jax.ShapeDtypeStruct((depth, MOD_ROWS, n), F32),
        compiler_params=_cparams(("arbitrary", "arbitrary"), 40),
        name="mods",
    )(cvec, w_mod, b_mod.reshape(depth, 1, n))


def _in_body(x_ref, m_ref, w_ref, a_ref, q_ref, g_ref, r_ref, *, tm, ctx_start):
    b, i = pl.program_id(0), pl.program_id(1)
    xn = _rms(x_ref[0])
    shift = _mod_rows(m_ref, b, i * tm, tm, ctx_start, 0)
    scale = _mod_rows(m_ref, b, i * tm, tm, ctx_start, 1)
    u = (xn * (1.0 + scale) + shift).astype(BF16)
    a_ref[0] = _dot(u, w_ref[0, :, 0:A_W])
    q_ref[0] = _dot(u, w_ref[0, :, A_W:A_W + MQ_W])
    g_ref[0] = _dot(u, w_ref[0, :, A_W + MQ_W:A_W + MQ_W + MG_W])
    r_ref[0] = _dot(u, w_ref[0, :, A_W + MQ_W + MG_W:IN_W])


def _in_call(x, mods, w_in_p, layer, n_ctx):
    bsz, s, d = x.shape
    tm = _pick_tile(s, (1088, 256))
    row = lambda b, i: (b, i, 0)
    return pl.pallas_call(
        functools.partial(_in_body, tm=tm, ctx_start=s - n_ctx),
        grid=(bsz, s // tm),
        in_specs=[pl.BlockSpec((1, tm, d), row),
                  pl.BlockSpec((1, MOD_ROWS, 6 * d), lambda b, i: (layer, 0, 0)),
                  pl.BlockSpec((1, d, IN_W), lambda b, i: (layer, 0, 0))],
        out_specs=[pl.BlockSpec((1, tm, A_W), row), pl.BlockSpec((1, tm, MQ_W), row),
                   pl.BlockSpec((1, tm, MG_W), row), pl.BlockSpec((1, tm, R_W), row)],
        out_shape=[jax.ShapeDtypeStruct((bsz, s, w), F32) for w in (A_W, MQ_W, MG_W, R_W)],
        compiler_params=_cparams(("parallel", "parallel"), 52),
        name="in_proj",
    )(x, mods, w_in_p)


LOG2E = 1.4426950408889634


def _prep_body(a_ref, gq_ref, gkv_ref, wuqt_ref, wukvk_ref, wukvvt_ref, cos_ref, sin_ref, cost_ref, sint_ref,
               qt_ref, k_ref, vt_ref):
    for bb in range(a_ref.shape[0]):
        _prep_sample(bb, a_ref, gq_ref, gkv_ref, wuqt_ref, wukvk_ref, wukvvt_ref, cos_ref, sin_ref, cost_ref,
                     sint_ref, qt_ref, k_ref, vt_ref)


def _prep_sample(bb, a_ref, gq_ref, gkv_ref, wuqt_ref, wukvk_ref, wukvvt_ref, cos_ref, sin_ref, cost_ref, sint_ref,
                 qt_ref, k_ref, vt_ref):
    a = a_ref[bb]
    nq, nkv = MLA_Q_RANK, MLA_Q_RANK + MLA_KV_RANK
    half = MLA_ROPE // 2
    hn = MLA_HEADS * MLA_NOPE
    cq_t = (_rms(a[:, 0:nq]) * gq_ref[0]).T.astype(BF16)
    ckv = _rms(a[:, nq:nkv]) * gkv_ref[0]
    q_t = _dot(wuqt_ref[0], cq_t) * (MLA_SCALE * LOG2E)
    x1, x2 = q_t[hn:hn + LANES], q_t[hn + LANES:hn + 2 * LANES]
    cos_t, sin_t = cost_ref[...], sint_ref[...]
    r1 = (x1 * cos_t - x2 * sin_t).astype(BF16)
    r2 = (x1 * sin_t + x2 * cos_t).astype(BF16)
    k_nope = _dot(ckv.astype(BF16), wukvk_ref[0])
    v_t = _dot(wukvvt_ref[0], ckv.T.astype(BF16))
    k1, k2 = a[:, nkv:nkv + half], a[:, nkv + half:nkv + 2 * half]
    c32, s32 = cos_ref[:, 0:half], sin_ref[:, 0:half]
    kr1 = (k1 * c32 - k2 * s32).astype(BF16)
    kr2 = (k1 * s32 + k2 * c32).astype(BF16)
    for h in range(MLA_HEADS):
        qt_ref[bb, h, 0:MLA_NOPE, :] = q_t[h * MLA_NOPE:(h + 1) * MLA_NOPE].astype(BF16)
        qt_ref[bb, h, MLA_NOPE:MLA_NOPE + half, :] = r1[h * half:(h + 1) * half]
        qt_ref[bb, h, MLA_NOPE + half:MLA_QK, :] = r2[h * half:(h + 1) * half]
        k_ref[bb, h, :, 0:MLA_NOPE] = k_nope[:, h * MLA_NOPE:(h + 1) * MLA_NOPE].astype(BF16)
        k_ref[bb, h, :, MLA_NOPE:MLA_NOPE + half] = kr1
        k_ref[bb, h, :, MLA_NOPE + half:MLA_QK] = kr2
        vt_ref[bb, h] = v_t[h * MLA_V:(h + 1) * MLA_V].astype(BF16)


def _prep_call(a, g_q, w_uq_t, g_kv, w_ukv_k, w_ukv_vt, rope, layer):
    bsz, s, _ = a.shape
    tm = 256
    lsel = lambda i: (layer, 0, 0)
    cos4, sin4, cos4_t, sin4_t = rope
    return pl.pallas_call(
        _prep_body,
        grid=(s // tm,),
        in_specs=[pl.BlockSpec((bsz, tm, A_W), lambda i: (0, i, 0)),
                  pl.BlockSpec((1, 1, MLA_Q_RANK), lsel),
                  pl.BlockSpec((1, 1, MLA_KV_RANK), lsel),
                  pl.BlockSpec((1, MLA_HEADS * MLA_QK, MLA_Q_RANK), lsel),
                  pl.BlockSpec((1, MLA_KV_RANK, MLA_HEADS * MLA_NOPE), lsel),
                  pl.BlockSpec((1, MLA_HEADS * MLA_V, MLA_KV_RANK), lsel),
                  pl.BlockSpec((tm, LANES), lambda i: (i, 0)),
                  pl.BlockSpec((tm, LANES), lambda i: (i, 0)),
                  pl.BlockSpec((LANES, tm), lambda i: (0, i)),
                  pl.BlockSpec((LANES, tm), lambda i: (0, i))],
        out_specs=[pl.BlockSpec((bsz, MLA_HEADS, MLA_QK, tm), lambda i: (0, 0, 0, i)),
                   pl.BlockSpec((bsz, MLA_HEADS, tm, MLA_QK), lambda i: (0, 0, i, 0)),
                   pl.BlockSpec((bsz, MLA_HEADS, MLA_V, tm), lambda i: (0, 0, 0, i))],
        out_shape=[jax.ShapeDtypeStruct((bsz, MLA_HEADS, MLA_QK, s), BF16),
                   jax.ShapeDtypeStruct((bsz, MLA_HEADS, s, MLA_QK), BF16),
                   jax.ShapeDtypeStruct((bsz, MLA_HEADS, MLA_V, s), BF16)],
        compiler_params=_cparams(("parallel",), 40),
        name="mla_prep",
    )(a, g_q, g_kv, w_uq_t, w_ukv_k, w_ukv_vt, cos4, sin4, cos4_t, sin4_t)


def _attn_body(qt_ref, k_ref, vt_ref, o_ref, *, subs, zero_rows):
    work = [(q0, tq, c, c is chunks[0], c is chunks[-1]) for q0, tq, chunks in subs for c in chunks]
    score = lambda w: _dot(k_ref[0, 0, w[2][0]:w[2][0] + w[2][1], :],
                           qt_ref[0, 0, :, w[0]:w[0] + w[1]])
    pending = [score(w) for w in work[:ATTN_AHEAD]]
    m = l = acc = None
    for idx, (q0, tq, (start, size), first, last) in enumerate(work):
        if first:
            m = jnp.full((1, tq), -jnp.inf, F32)
            l = jnp.zeros((1, tq), F32)
            acc = jnp.zeros((MLA_V, tq), F32)
        st = pending.pop(0)
        if idx + ATTN_AHEAD < len(work):
            pending.append(score(work[idx + ATTN_AHEAD]))
        m_new = jnp.maximum(m, jnp.max(st, axis=0, keepdims=True))
        p = jnp.exp2(st - m_new)
        alpha = jnp.exp2(m - m_new)
        l = alpha * l + jnp.sum(p, axis=0, keepdims=True)
        acc = alpha * acc + _dot(vt_ref[0, 0, :, start:start + size], p.astype(BF16))
        m = m_new
        if last:
            o_ref[0, q0:q0 + tq, :] = (acc / l).T.astype(o_ref.dtype)
    if zero_rows is not None:
        o_ref[0, zero_rows[0]:zero_rows[1], :] = jnp.zeros((zero_rows[1] - zero_rows[0], MLA_V), o_ref.dtype)


ATTN_TQ = 512
ATTN_TK = 512
ATTN_AHEAD = 2


def _attn_call(qt, k, vt, n_ctx, with_ctx):
    bsz, nh, s, dk = k.shape
    n_lat = s - n_ctx
    assert n_lat % ATTN_TQ == 0 and n_lat % ATTN_TK == 0
    ctx_chunk = (n_lat, n_ctx)
    lat_chunks = (ctx_chunk,) + tuple((j * ATTN_TK, ATTN_TK) for j in range(n_lat // ATTN_TK))
    subs = [(j * ATTN_TQ, ATTN_TQ, lat_chunks) for j in range(n_lat // ATTN_TQ)]
    if with_ctx:
        subs.append((n_lat, n_ctx, (ctx_chunk,)))
    return pl.pallas_call(
        functools.partial(_attn_body, subs=tuple(subs), zero_rows=None if with_ctx else (n_lat, s)),
        grid=(bsz, nh),
        in_specs=[pl.BlockSpec((1, 1, dk, s), lambda b, h: (b, h, 0, 0)),
                  pl.BlockSpec((1, 1, s, dk), lambda b, h: (b, h, 0, 0)),
                  pl.BlockSpec((1, 1, MLA_V, s), lambda b, h: (b, h, 0, 0))],
        out_specs=pl.BlockSpec((1, s, MLA_V), lambda b, h: (b, 0, h)),
        out_shape=jax.ShapeDtypeStruct((bsz, s, nh * MLA_V), BF16),
        compiler_params=_cparams(("parallel", "parallel"), 48),
        name="mla_attn",
    )(qt, k, vt)


ML_BLOCK_CHUNKS = 2


def _mlstm_body(xf_ref, gf_ref, xb_ref, gb_ref, bias_ref, hf_ref, hb_ref, c_ref, m_ref):
    lc = ML_CHUNK
    assert lc == LANES

    @pl.when(pl.program_id(0) == 0)
    def _():
        c_ref[...] = jnp.zeros_like(c_ref)
        m_ref[...] = jnp.zeros_like(m_ref)

    nprob = xf_ref.shape[0] * 2 * ML_HEADS
    s_io = lax.broadcasted_iota(jnp.int32, (lc, lc), 0)
    t_io = lax.broadcasted_iota(jnp.int32, (lc, lc), 1)
    lane = lax.broadcasted_iota(jnp.int32, (lc, LANES), 1)
    row = lax.broadcasted_iota(jnp.int32, (LANES, lc), 0)
    rowp = lax.broadcasted_iota(jnp.int32, (nprob, lc), 0)
    bias = bias_ref[...]
    ones_sq = jnp.ones((lc, lc), BF16)
    nsub = xf_ref.shape[1] // lc
    for t in range(nsub):
        _mlstm_chunk((xf_ref, xb_ref), (gf_ref, gb_ref), (hf_ref, hb_ref), c_ref, m_ref,
                     (t * lc, (nsub - 1 - t) * lc), bias, ones_sq, (s_io, t_io, lane, row, rowp))


def _mlstm_chunk(x_refs, g_refs, o_refs, c_ref, m_ref, row0, bias, ones_sq, iotas):
    lc = ML_CHUNK
    ngate = 4 * ML_HEADS
    s_io, t_io, lane, row, rowp = iotas
    nprob = rowp.shape[0]
    probs = []
    c_rows = jnp.zeros((nprob, lc), F32)
    for bb, d in [(bb, d) for bb in range(x_refs[0].shape[0]) for d in range(2)]:
        x_ref, g_ref = x_refs[d], g_refs[d]
        rows = slice(row0[d], row0[d] + lc)
        mask = (s_io <= t_io) if d == 0 else (s_io >= t_io)
        gt = (g_ref[bb, rows, :] + bias).T[0:ngate]
        lf = jax.nn.log_sigmoid(gt)
        hi = lf.astype(BF16)
        r1 = lf - hi.astype(F32)
        mid = r1.astype(BF16)
        lo = (r1 - mid.astype(F32)).astype(BF16)
        sums = _dot(jnp.concatenate([hi, mid, lo], axis=0),
                    jnp.concatenate([mask.astype(BF16), ones_sq], axis=1))
        sums = sums[0:ngate] + sums[ngate:2 * ngate] + sums[2 * ngate:3 * ngate]
        b_run, b_tot = sums[:, 0:lc], sums[:, lc:2 * lc]
        x = x_ref[bb, rows, :]
        for pair in range(ML_HEADS // 2):
            qs = x[:, pair * LANES:(pair + 1) * LANES] * (ML_DH ** -0.5)
            ks = x[:, ML_W + pair * LANES:ML_W + (pair + 1) * LANES]
            vt = x[:, 2 * ML_W + pair * LANES:2 * ML_W + (pair + 1) * LANES].T
            for odd in range(2):
                h = 2 * pair + odd
                ci, cf = d * 2 * ML_HEADS + h, d * 2 * ML_HEADS + ML_HEADS + h
                own = (lane >= ML_DH) if odd else (lane < ML_DH)
                own_r = (row >= ML_DH) if odd else (row < ML_DH)
                den_row = 0 if odd else ML_DH
                j = (bb * 2 + d) * ML_HEADS + h
                brow, irow = b_run[cf:cf + 1], gt[ci:ci + 1]
                c_rows = jnp.where(rowp == j, brow - irow, c_rows)
                probs.append(dict(
                    j=j, bb=bb, d=d, pair=pair, odd=odd, mask=mask, den_row=den_row, brow=brow, irow=irow,
                    btot=b_tot[cf:cf + 1],
                    qm=jnp.where(own, qs, 0.0).astype(BF16), ks=ks.astype(BF16),
                    km=jnp.where(own, ks, 0.0).astype(BF16),
                    vaug=jnp.where(own_r, vt, jnp.where(row == den_row, 1.0, 0.0))))

    c_cols = jnp.concatenate([c_rows, jnp.zeros((LANES - nprob, lc), F32)], axis=0).T

    for p in probs:
        p["c_old"] = c_ref[p["j"]]
        p["kq"] = _dot_nt(p["ks"], p["qm"])
        p["inter"] = _dot_nt(p["c_old"].astype(BF16), p["qm"])
    for p in probs:
        j = p["j"]
        m_prev = m_ref[j:j + 1, :]
        dt = jnp.where(p["mask"], p["brow"] - c_cols[:, j:j + 1], -jnp.inf)
        inter_m = p["brow"] + m_prev
        m_row = jnp.maximum(inter_m, jnp.max(dt, axis=0, keepdims=True))
        p["st"] = (p["kq"] * jnp.exp(dt - m_row)).astype(BF16)
        p["w_inter"] = jnp.exp(inter_m - m_row)
        p["floor"] = jnp.exp(-m_row)
        grow = p["btot"] - p["brow"] + p["irow"]
        m_new = jnp.maximum(p["btot"] + m_prev, jnp.max(grow, axis=1, keepdims=True))
        p["w_old"] = jnp.exp(p["btot"] + m_prev - m_new)
        p["wv"] = (p["vaug"] * jnp.exp(grow - m_new)).astype(BF16)
        m_ref[j:j + 1, :] = m_new
    outs = {}
    for p in probs:
        ht = _dot(p["vaug"].astype(BF16), p["st"]) + p["w_inter"] * p["inter"]
        den = ht[p["den_row"]:p["den_row"] + 1]
        outs[(p["bb"], p["d"], p["pair"], p["odd"])] = ht / jnp.maximum(jnp.abs(den), p["floor"])
    for bb, d, pair in sorted({(p["bb"], p["d"], p["pair"]) for p in probs}):
        both = jnp.where(row < ML_DH, outs[(bb, d, pair, 0)], outs[(bb, d, pair, 1)])
        o_refs[d][bb, row0[d]:row0[d] + lc, pair * LANES:(pair + 1) * LANES] = both.T
    for p in probs:
        c_ref[p["j"]] = p["w_old"] * p["c_old"] + _dot(p["wv"], p["km"])


def _mlstm_call(mq, mg, bias_p, n_ctx):
    bsz, s, _ = mq.shape
    rows = ML_BLOCK_CHUNKS * ML_CHUNK
    assert n_ctx % rows == 0 and s % rows == 0
    nch, ncc = s // rows, n_ctx // rows
    nlc = nch - ncc
    fwd = lambda j: (0, jnp.where(j < ncc, nlc + j, j - ncc), 0)
    bwd = lambda j: (0, nch - 1 - j, 0)
    out = jax.ShapeDtypeStruct((bsz, s, ML_W), F32)
    nstate = bsz * 2 * ML_HEADS
    assert nstate <= LANES
    return pl.pallas_call(
        _mlstm_body,
        grid=(nch,),
        in_specs=[pl.BlockSpec((bsz, rows, MQ_W), fwd), pl.BlockSpec((bsz, rows, MG_W), fwd),
                  pl.BlockSpec((bsz, rows, MQ_W), bwd), pl.BlockSpec((bsz, rows, MG_W), bwd),
                  pl.BlockSpec((1, MG_W), lambda j: (0, 0))],
        out_specs=[pl.BlockSpec((bsz, rows, ML_W), fwd), pl.BlockSpec((bsz, rows, ML_W), bwd)],
        out_shape=[out, out],
        scratch_shapes=[pltpu.VMEM((nstate, LANES, LANES), F32),
                        pltpu.VMEM((nstate, LANES), F32)],
        compiler_params=_cparams(("arbitrary",), 48),
        name="mlstm",
    )(mq, mg, mq, mg, bias_p)


LRU_PITCH_PAD = 8
LRU_UNROLL = 8


def _lru_body(xb_ref, gb_ref, cw_ref, cb_ref, wa_ref, ba_ref, wx_ref, bx_ref, lam_ref, y_ref,
              xp_ref, xs_ref, a_ref, u_ref, *, n_ctx, s_len):
    n_lat = s_len - n_ctx
    pad = SUBLANES
    lat_off = n_ctx + 2 * pad
    zeros = jnp.zeros((pad, LANES), F32)
    xp_ref[0:pad, :] = zeros
    xp_ref[pad + n_ctx:lat_off, :] = zeros
    xp_ref[lat_off + n_lat:lat_off + n_lat + pad, :] = zeros
    xp_ref[pad:pad + n_ctx, :] = xb_ref[0, n_lat:n_lat + n_ctx, :]
    cchunk = 512

    def copy_body(c, _):
        src = pl.multiple_of(c * cchunk, SUBLANES)
        dst = pl.multiple_of(lat_off + c * cchunk, SUBLANES)
        xp_ref[pl.ds(dst, cchunk), :] = xb_ref[0, pl.ds(src, cchunk), :]
        return 0

    lax.fori_loop(0, n_lat // cchunk, copy_body, 0)

    cw = cw_ref[...]
    cb = cb_ref[...]

    def conv(src0, dst0, n):
        acc = cb + xp_ref[src0 - CONV_LEFT:src0 - CONV_LEFT + n, :] * cw[0:1, :]
        for j in range(1, CONV_W):
            acc = acc + xp_ref[src0 - CONV_LEFT + j:src0 - CONV_LEFT + j + n, :] * cw[j:j + 1, :]
        xs_ref[dst0:dst0 + n, :] = acc

    conv(pad, n_lat, n_ctx)
    for c in range(n_lat // cchunk):
        conv(lat_off + c * cchunk, c * cchunk, cchunk)

    seg_lat = n_lat // SUBLANES
    seg_ctx = n_ctx // SUBLANES
    p_lat, p_ctx = seg_lat + LRU_PITCH_PAD, seg_ctx + LRU_PITCH_PAD
    ctx_base = SUBLANES * p_lat
    row_io = lax.broadcasted_iota(jnp.int32, (SUBLANES, LANES), 0)

    def gates(x, d):
        xb16 = x.astype(BF16)
        r = _sigmoid(_dot(xb16, wa_ref[d]) + ba_ref[d])
        i = _sigmoid(_dot(xb16, wx_ref[d]) + bx_ref[d])
        log_a = (-LRU_C) * r * jax.nn.softplus(-lam_ref[d])
        a = jnp.exp(log_a)
        return a, jnp.sqrt(jnp.tanh(-log_a) * (1.0 + a * a)) * (i * x)

    def fill(d):
        def lat_body(r, _):
            src = pl.multiple_of(r * seg_lat, SUBLANES)
            dst = pl.multiple_of(r * p_lat, SUBLANES)
            a, u = gates(xs_ref[pl.ds(src, seg_lat), :], d)
            a_ref[d, pl.ds(dst, seg_lat), :] = a
            u_ref[d, pl.ds(dst, seg_lat), :] = u
            return 0

        lax.fori_loop(0, SUBLANES, lat_body, 0)
        a, u = gates(xs_ref[n_lat:n_lat + n_ctx, :], d)
        for r in range(SUBLANES):
            a_ref[d, ctx_base + r * p_ctx:ctx_base + r * p_ctx + seg_ctx, :] = a[r * seg_ctx:(r + 1) * seg_ctx, :]
            u_ref[d, ctx_base + r * p_ctx:ctx_base + r * p_ctx + seg_ctx, :] = u[r * seg_ctx:(r + 1) * seg_ctx, :]

    def scan(base, n, pitch, h0s):
        def block(tb, carry):
            idx = [[pl.ds(base + (tb * LRU_UNROLL + k if d == 0 else n - 1 - tb * LRU_UNROLL - k),
                          SUBLANES, stride=pitch) for k in range(LRU_UNROLL)] for d in range(2)]
            av = [[a_ref[d, i, :] for i in idx[d]] for d in range(2)]
            uv = [[u_ref[d, i, :] for i in idx[d]] for d in range(2)]
            carry = list(carry)
            for k in range(LRU_UNROLL):
                for d in range(2):
                    h, acum = carry[d]
                    h = av[d][k] * h + uv[d][k]
                    acum = acum * av[d][k]
                    carry[d] = (h, acum)
                    uv[d][k], av[d][k] = h, acum
            for d in range(2):
                for k in range(LRU_UNROLL):
                    u_ref[d, idx[d][k], :] = uv[d][k]
                    a_ref[d, idx[d][k], :] = av[d][k]
            return tuple(carry)

        init = (jnp.zeros((SUBLANES, LANES), F32), jnp.ones((SUBLANES, LANES), F32))
        ends = lax.fori_loop(0, n // LRU_UNROLL, block, (init, init))
        result = []
        for d in range(2):
            h_end, a_end = ends[d]
            carry = h0s[d]
            cvec = jnp.zeros((SUBLANES, LANES), F32)
            for r in (range(SUBLANES) if d == 0 else range(SUBLANES - 1, -1, -1)):
                cvec = jnp.where(row_io == r, carry, cvec)
                carry = h_end[r:r + 1, :] + a_end[r:r + 1, :] * carry
            result.append((cvec, carry))
        return result

    fill(0)
    fill(1)
    zero_state = jnp.zeros((1, LANES), F32)
    ctx_res = scan(ctx_base, seg_ctx, p_ctx, (zero_state, zero_state))
    lat_res = scan(0, seg_lat, p_lat, (ctx_res[0][1], ctx_res[1][1]))
    carries = {(d, "ctx"): ctx_res[d][0] for d in range(2)}
    carries.update({(d, "lat"): lat_res[d][0] for d in range(2)})

    def emit(kind, r, dst0, src0, n):
        hsum = None
        for d in range(2):
            c_in = carries[d, kind][r:r + 1, :]
            part = u_ref[d, src0:src0 + n, :] + a_ref[d, src0:src0 + n, :] * c_in
            hsum = part if hsum is None else hsum + part
        y_ref[0, dst0:dst0 + n, :] = jax.nn.gelu(gb_ref[0, dst0:dst0 + n, :]) * hsum

    for r in range(SUBLANES):
        emit("lat", r, r * seg_lat, r * p_lat, seg_lat)
        emit("ctx", r, n_lat + r * seg_ctx, ctx_base + r * p_ctx, seg_ctx)


def _lru_call(r, conv_w, conv_b, wa_bd, b_a, wx_bd, b_x, lam, layer, n_ctx):
    bsz, s, _ = r.shape
    nh = LRU_W // LANES
    n_lat = s - n_ctx
    scan_rows = SUBLANES * (n_lat // SUBLANES + LRU_PITCH_PAD) + SUBLANES * (n_ctx // SUBLANES + LRU_PITCH_PAD)
    vec = lambda b, c: (layer, 0, 0, c)
    return pl.pallas_call(
        functools.partial(_lru_body, n_ctx=n_ctx, s_len=s),
        grid=(bsz, nh),
        in_specs=[pl.BlockSpec((1, s, LANES), lambda b, c: (b, 0, c)),
                  pl.BlockSpec((1, s, LANES), lambda b, c: (b, 0, nh + c)),
                  pl.BlockSpec((None, CONV_W, LANES), lambda b, c: (layer, 0, c)),
                  pl.BlockSpec((None, 1, LANES), lambda b, c: (layer, 0, c)),
                  pl.BlockSpec((None, None, 2, LANES, LANES), lambda b, c: (layer, c, 0, 0, 0)),
                  pl.BlockSpec((None, 2, 1, LANES), vec),
                  pl.BlockSpec((None, None, 2, LANES, LANES), lambda b, c: (layer, c, 0, 0, 0)),
                  pl.BlockSpec((None, 2, 1, LANES), vec),
                  pl.BlockSpec((None, 2, 1, LANES), vec)],
        out_specs=pl.BlockSpec((1, s, LANES), lambda b, c: (b, 0, c)),
        out_shape=jax.ShapeDtypeStruct((bsz, s, LRU_W), F32),
        scratch_shapes=[pltpu.VMEM((s + 3 * SUBLANES, LANES), F32),
                        pltpu.VMEM((s, LANES), F32),
                        pltpu.VMEM((2, scan_rows, LANES), F32),
                        pltpu.VMEM((2, scan_rows, LANES), F32)],
        compiler_params=_cparams(("parallel", "parallel"), 48),
        name="rglru",
    )(r, r, conv_w, conv_b, wa_bd, b_a, wx_bd, b_x, lam)


MLP_FF_CHUNK = 1024


def _mix_mlp_body(x_ref, m_ref, ya_ref, hf_ref, hb_ref, og_ref, yc_ref, wo_ref, hn_ref, w1_ref, w2_ref, fg_ref,
                  o_ref, *, tm, ctx_start, final):
    b, i = pl.program_id(0), pl.program_id(1)
    hsum = hf_ref[0] + hb_ref[0]
    msq = jnp.dot(hsum * hsum, hn_ref[...], precision=lax.Precision.HIGHEST, preferred_element_type=F32)
    yb = (_sigmoid(og_ref[0]) * (hsum * lax.rsqrt(msq + EPS))).astype(BF16)
    na, nb = MLA_HEADS * MLA_V, MLA_HEADS * MLA_V + ML_W
    y = (_dot(ya_ref[0], wo_ref[0, 0:na, :]) + _dot(yb, wo_ref[0, na:nb, :])
         + _dot(yc_ref[0].astype(BF16), wo_ref[0, nb:, :]))
    x1 = x_ref[0] + _mod_rows(m_ref, b, i * tm, tm, ctx_start, 2) * y
    shift = _mod_rows(m_ref, b, i * tm, tm, ctx_start, 3)
    scale = _mod_rows(m_ref, b, i * tm, tm, ctx_start, 4)
    u = (_rms(x1) * (1.0 + scale) + shift).astype(BF16)
    fc = MLP_FF_CHUNK
    nchunk = D_FF // fc
    up = lambda c: _dot(u, w1_ref[0, :, c * fc:(c + 1) * fc])
    h_next = up(0)
    acc = None
    for c in range(nchunk):
        hid = jnp.maximum(h_next, 0.0)
        if c + 1 < nchunk:
            h_next = up(c + 1)
        part = _dot((hid * hid).astype(BF16), w2_ref[0, c * fc:(c + 1) * fc, :])
        acc = part if acc is None else acc + part
    res = x1 + _mod_rows(m_ref, b, i * tm, tm, ctx_start, 5) * acc
    if final:
        res = _rms(res) * fg_ref[...]
    o_ref[0] = res


def _mix_mlp_call(x, mods, ya, hf, hb, mq, yc, w_out, head_avg, w1, w2, final_g, layer, n_ctx, final):
    bsz, s, d = x.shape
    rows_out = s - n_ctx if final else s
    tm = _pick_tile(rows_out, (544, 512, 256))
    row = lambda b, i: (b, i, 0)
    lsel = lambda b, i: (layer, 0, 0)
    once = pl.Buffered(1)
    return pl.pallas_call(
        functools.partial(_mix_mlp_body, tm=tm, ctx_start=None if final else s - n_ctx, final=final),
        grid=(bsz, rows_out // tm),
        in_specs=[pl.BlockSpec((1, tm, d), row),
                  pl.BlockSpec((1, MOD_ROWS, 6 * d), lsel, pipeline_mode=once),
                  pl.BlockSpec((1, tm, MLA_HEADS * MLA_V), row),
                  pl.BlockSpec((1, tm, ML_W), row),
                  pl.BlockSpec((1, tm, ML_W), row),
                  pl.BlockSpec((1, tm, ML_W), lambda b, i: (b, i, 3)),
                  pl.BlockSpec((1, tm, LRU_W), row),
                  pl.BlockSpec((1, d, d), lsel, pipeline_mode=once),
                  pl.BlockSpec((ML_W, ML_W), lambda b, i: (0, 0), pipeline_mode=once),
                  pl.BlockSpec((1, d, D_FF), lsel, pipeline_mode=once),
                  pl.BlockSpec((1, D_FF, d), lsel, pipeline_mode=once),
                  pl.BlockSpec((1, d), lambda b, i: (0, 0), pipeline_mode=once)],
        out_specs=pl.BlockSpec((1, tm, d), row),
        out_shape=jax.ShapeDtypeStruct((bsz, rows_out, d), F32),
        compiler_params=_cparams(("parallel", "parallel"), 56),
        name="mix_mlp",
    )(x, mods, ya, hf, hb, mq, yc, w_out, head_avg, w1, w2, final_g)


def _rope_tables(seq, n_ctx):
    half = MLA_ROPE // 2
    row = jnp.repeat(jnp.arange(seq // GRID_W), GRID_W).astype(F32)
    col = jnp.tile(jnp.arange(GRID_W), seq // GRID_W).astype(F32)
    freqs = 1.0 / (ROPE_BASE ** (jnp.arange(0, half, 2, dtype=F32) / half))
    ang = jnp.concatenate([row[:, None] * freqs, col[:, None] * freqs], axis=-1)
    cos = jnp.concatenate([jnp.cos(ang), jnp.ones((n_ctx, half), F32)], axis=0)
    sin = jnp.concatenate([jnp.sin(ang), jnp.zeros((n_ctx, half), F32)], axis=0)
    return jnp.tile(cos, (1, MLA_HEADS)), jnp.tile(sin, (1, MLA_HEADS))


def _block_diag_halves(w):
    depth = w.shape[0]
    per = LANES // LRU_BD
    wh = w.reshape(depth, 2, LRU_BLOCKS // per, per, LRU_BD, LRU_BD)
    eye = jnp.eye(per, dtype=w.dtype)
    bd = jnp.einsum("ldcpio,pq->ldcpiqo", wh, eye).reshape(depth, 2, LRU_BLOCKS // per, LANES, LANES)
    return bd.transpose(0, 2, 1, 3, 4).astype(BF16)


def kernel(x, c, ctx, c_ctx, w_mod, b_mod, w_in, mla_g_q, mla_w_uq, mla_g_kv, mla_w_ukv, ml_gate_bias,
           lru_conv_w, lru_conv_b, lru_w_a, lru_b_a, lru_w_x, lru_b_x, lru_lam, w_out, w_ff1, w_ff2, final_g):
    bsz, seq, d = x.shape
    n_ctx = ctx.shape[1]
    depth = w_in.shape[0]
    assert bsz <= CTX_MOD_ROW and d == D_MODEL and n_ctx % 256 == 0

    zc = lambda n: jnp.zeros((depth, d, n), w_in.dtype)
    ml0, ml1 = MLA_IN, MLA_IN + 4 * ML_W
    w_in_p = jnp.concatenate([w_in[:, :, :ml0], zc(A_W - MLA_IN), w_in[:, :, ml0:ml1],
                              w_in[:, :, ml1:ml1 + 4 * ML_HEADS], zc(MG_W - 4 * ML_HEADS),
                              w_in[:, :, ml1 + 4 * ML_HEADS:]], axis=-1).astype(BF16)
    half = MLA_ROPE // 2
    uq = mla_w_uq.reshape(depth, MLA_Q_RANK, MLA_HEADS, MLA_QK)
    w_uq_t = jnp.concatenate([uq[..., :MLA_NOPE].reshape(depth, MLA_Q_RANK, -1),
                              uq[..., MLA_NOPE:MLA_NOPE + half].reshape(depth, MLA_Q_RANK, -1),
                              uq[..., MLA_NOPE + half:].reshape(depth, MLA_Q_RANK, -1)],
                             axis=-1).astype(BF16).transpose(0, 2, 1)
    ukv = mla_w_ukv.reshape(depth, MLA_KV_RANK, MLA_HEADS, MLA_NOPE + MLA_V)
    w_ukv_k = ukv[..., :MLA_NOPE].reshape(depth, MLA_KV_RANK, -1).astype(BF16)
    w_ukv_vt = ukv[..., MLA_NOPE:].reshape(depth, MLA_KV_RANK, -1).astype(BF16).transpose(0, 2, 1)
    g_q = mla_g_q.reshape(depth, 1, MLA_Q_RANK)
    g_kv = mla_g_kv.reshape(depth, 1, MLA_KV_RANK)
    bias_p = jnp.pad(ml_gate_bias, ((0, 0), (0, MG_W - 4 * ML_HEADS)))
    wa_bd, wx_bd = _block_diag_halves(lru_w_a), _block_diag_halves(lru_w_x)
    vec4 = lambda v: v.reshape(depth, 2, 1, LRU_W)
    conv_b = lru_conv_b.reshape(depth, 1, LRU_W)
    w_out16, w1_16, w2_16 = w_out.astype(BF16), w_ff1.astype(BF16), w_ff2.astype(BF16)
    head_avg = jnp.kron(jnp.eye(ML_HEADS, dtype=F32), jnp.full((ML_DH, ML_DH), 1.0 / ML_DH, F32))
    cos4, sin4 = _rope_tables(seq, n_ctx)
    rope = (cos4, sin4, cos4.T, sin4.T)
    fg = final_g.reshape(1, d)

    cvec = jnp.concatenate([c, jnp.zeros((CTX_MOD_ROW - bsz, d), c.dtype), c_ctx[None, :],
                            jnp.zeros((MOD_ROWS - CTX_MOD_ROW - 1, d), c.dtype)], axis=0)
    mods = _mods_call(cvec, w_mod, b_mod)

    xs = jnp.concatenate([x, ctx], axis=1)
    for l in range(depth):
        last = l == depth - 1
        a, mq, mg, r = _in_call(xs, mods, w_in_p, l, n_ctx)
        qt, k, vt = _prep_call(a, g_q, w_uq_t, g_kv, w_ukv_k, w_ukv_vt, rope, l)
        ya = _attn_call(qt, k, vt, n_ctx, with_ctx=not last)
        hf, hb = _mlstm_call(mq, mg, bias_p[l:l + 1], n_ctx)
        yc = _lru_call(r, lru_conv_w, conv_b, wa_bd, vec4(lru_b_a), wx_bd, vec4(lru_b_x), vec4(lru_lam), l, n_ctx)
        xs = _mix_mlp_call(xs, mods, ya, hf, hb, mq, yc, w_out16, head_avg, w1_16, w2_16, fg, l, n_ctx,
                           final=last)
    return xs
```

```python
import functools

import jax
import jax.numpy as jnp
from jax import lax
from jax.experimental import pallas as pl
from jax.experimental.pallas import tpu as pltpu

F32 = jnp.float32
BF16 = jnp.bfloat16

D_MODEL = 1024
DEPTH = 4
GRID_W = 64
N_CTX = 256
MLA_HEADS = 4
MLA_Q_RANK = 256
MLA_KV_RANK = 128
MLA_NOPE = 128
MLA_ROPE = 64
MLA_V = 128
MLA_QK = MLA_NOPE + MLA_ROPE
MLA_SCALE = MLA_QK ** -0.5
ROPE_BASE = 10000.0
ML_HEADS = 4
ML_DH = 64
ML_W = ML_HEADS * ML_DH
ML_CHUNK = 128
LRU_W = 256
LRU_BLOCKS = 4
LRU_BD = LRU_W // LRU_BLOCKS
CONV_W = 4
CONV_LEFT = 2
LRU_C = 8.0
D_FF = 4 * D_MODEL
EPS = 1e-6
MLA_IN = MLA_Q_RANK + MLA_KV_RANK + MLA_ROPE
ML_IN = 4 * ML_W + 4 * ML_HEADS
LRU_IN = 2 * LRU_W

LANES = 128
SUBLANES = 8
MOD_ROWS = 8
CTX_MOD_ROW = 4

A_W = 512
MQ_W = 4 * ML_W
MG_W = LANES
R_W = 2 * LRU_W
IN_W = A_W + MQ_W + MG_W + R_W


def _cparams(sem, vmem_mb):
    return pltpu.CompilerParams(dimension_semantics=sem, vmem_limit_bytes=vmem_mb * 1024 * 1024)


def _mod_rows(m_ref, b, row0, tm, ctx_start, seg):
    lo, hi = seg * D_MODEL, (seg + 1) * D_MODEL
    lat = m_ref[0, pl.ds(b, 1), lo:hi]
    if ctx_start is None:
        return lat
    ctx = m_ref[0, CTX_MOD_ROW:CTX_MOD_ROW + 1, lo:hi]
    rows = row0 + lax.broadcasted_iota(jnp.int32, (tm, 1), 0)
    return jnp.where(rows >= ctx_start, ctx, lat)


def _pick_tile(n, candidates):
    return next(t for t in candidates if n % t == 0)


def _rms(x):
    return x * lax.rsqrt(jnp.mean(x * x, axis=-1, keepdims=True) + EPS)


def _sigmoid(x):
    return 0.5 * jnp.tanh(0.5 * x) + 0.5


def _dot(a, b):
    return jnp.dot(a, b, preferred_element_type=F32)


def _dot_nt(a, b):
    return lax.dot_general(a, b, (((1,), (1,)), ((), ())), preferred_element_type=F32)


def _mods_body(c_ref, w_ref, b_ref, o_ref):
    cv = c_ref[...]
    act = (cv * jax.nn.sigmoid(cv)).astype(BF16)
    o_ref[0] = _dot(act, w_ref[0].astype(BF16)) + b_ref[0]


def _mods_call(cvec, w_mod, b_mod):
    depth, d, n = w_mod.shape
    tn = 1536
    return pl.pallas_call(
        _mods_body,
        grid=(depth, n // tn),
        in_specs=[pl.BlockSpec((MOD_ROWS, d), lambda l, j: (0, 0)),
                  pl.BlockSpec((1, d, tn), lambda l, j: (l, 0, j)),
                  pl.BlockSpec((1, 1, tn), lambda l, j: (l, 0, j))],
        out_specs=pl.BlockSpec((1, MOD_ROWS, tn), lambda l, j: (l, 0, j)),
        out_shape=jax.ShapeDtypeStruct((depth, MOD_ROWS, n), F32),
        compiler_params=_cparams(("arbitrary", "arbitrary"), 40),
        name="mods",
    )(cvec, w_mod, b_mod.reshape(depth, 1, n))


def _in_body(x_ref, m_ref, w_ref, a_ref, q_ref, g_ref, r_ref, *, tm, ctx_start):
    b, i = pl.program_id(0), pl.program_id(1)
    xn = _rms(x_ref[0])
    shift = _mod_rows(m_ref, b, i * tm, tm, ctx_start, 0)
    scale = _mod_rows(m_ref, b, i * tm, tm, ctx_start, 1)
    u = (xn * (1.0 + scale) + shift).astype(BF16)
    a_ref[0] = _dot(u, w_ref[0, :, 0:A_W])
    q_ref[0] = _dot(u, w_ref[0, :, A_W:A_W + MQ_W])
    g_ref[0] = _dot(u, w_ref[0, :, A_W + MQ_W:A_W + MQ_W + MG_W])
    r_ref[0] = _dot(u, w_ref[0, :, A_W + MQ_W + MG_W:IN_W])


def _in_call(x, mods, w_in_p, layer, n_ctx):
    bsz, s, d = x.shape
    tm = _pick_tile(s, (1088, 256))
    row = lambda b, i: (b, i, 0)
    return pl.pallas_call(
        functools.partial(_in_body, tm=tm, ctx_start=s - n_ctx),
        grid=(bsz, s // tm),
        in_specs=[pl.BlockSpec((1, tm, d), row),
                  pl.BlockSpec((1, MOD_ROWS, 6 * d), lambda b, i: (layer, 0, 0)),
                  pl.BlockSpec((1, d, IN_W), lambda b, i: (layer, 0, 0))],
        out_specs=[pl.BlockSpec((1, tm, A_W), row), pl.BlockSpec((1, tm, MQ_W), row),
                   pl.BlockSpec((1, tm, MG_W), row), pl.BlockSpec((1, tm, R_W), row)],
        out_shape=[jax.ShapeDtypeStruct((bsz, s, w), F32) for w in (A_W, MQ_W, MG_W, R_W)],
        compiler_params=_cparams(("parallel", "parallel"), 52),
        name="in_proj",
    )(x, mods, w_in_p)


LOG2E = 1.4426950408889634


def _prep_body(a_ref, gq_ref, gkv_ref, wuqt_ref, wukvk_ref, wukvvt_ref, cos_ref, sin_ref, cost_ref, sint_ref,
               qt_ref, k_ref, vt_ref):
    for bb in range(a_ref.shape[0]):
        _prep_sample(bb, a_ref, gq_ref, gkv_ref, wuqt_ref, wukvk_ref, wukvvt_ref, cos_ref, sin_ref, cost_ref,
                     sint_ref, qt_ref, k_ref, vt_ref)


def _prep_sample(bb, a_ref, gq_ref, gkv_ref, wuqt_ref, wukvk_ref, wukvvt_ref, cos_ref, sin_ref, cost_ref, sint_ref,
                 qt_ref, k_ref, vt_ref):
    a = a_ref[bb]
    nq, nkv = MLA_Q_RANK, MLA_Q_RANK + MLA_KV_RANK
    half = MLA_ROPE // 2
    hn = MLA_HEADS * MLA_NOPE
    cq_t = (_rms(a[:, 0:nq]) * gq_ref[0]).T.astype(BF16)
    ckv = _rms(a[:, nq:nkv]) * gkv_ref[0]
    q_t = _dot(wuqt_ref[0], cq_t) * (MLA_SCALE * LOG2E)
    x1, x2 = q_t[hn:hn + LANES], q_t[hn + LANES:hn + 2 * LANES]
    cos_t, sin_t = cost_ref[...], sint_ref[...]
    r1 = (x1 * cos_t - x2 * sin_t).astype(BF16)
    r2 = (x1 * sin_t + x2 * cos_t).astype(BF16)
    k_nope = _dot(ckv.astype(BF16), wukvk_ref[0])
    v_t = _dot(wukvvt_ref[0], ckv.T.astype(BF16))
    k1, k2 = a[:, nkv:nkv + half], a[:, nkv + half:nkv + 2 * half]
    c32, s32 = cos_ref[:, 0:half], sin_ref[:, 0:half]
    kr1 = (k1 * c32 - k2 * s32).astype(BF16)
    kr2 = (k1 * s32 + k2 * c32).astype(BF16)
    for h in range(MLA_HEADS):
        qt_ref[bb, h, 0:MLA_NOPE, :] = q_t[h * MLA_NOPE:(h + 1) * MLA_NOPE].astype(BF16)
        qt_ref[bb, h, MLA_NOPE:MLA_NOPE + half, :] = r1[h * half:(h + 1) * half]
        qt_ref[bb, h, MLA_NOPE + half:MLA_QK, :] = r2[h * half:(h + 1) * half]
        k_ref[bb, h, :, 0:MLA_NOPE] = k_nope[:, h * MLA_NOPE:(h + 1) * MLA_NOPE].astype(BF16)
        k_ref[bb, h, :, MLA_NOPE:MLA_NOPE + half] = kr1
        k_ref[bb, h, :, MLA_NOPE + half:MLA_QK] = kr2
        vt_ref[bb, h] = v_t[h * MLA_V:(h + 1) * MLA_V].astype(BF16)


def _prep_call(a, g_q, w_uq_t, g_kv, w_ukv_k, w_ukv_vt, rope, layer):
    bsz, s, _ = a.shape
    tm = 256
    lsel = lambda i: (layer, 0, 0)
    cos4, sin4, cos4_t, sin4_t = rope
    return pl.pallas_call(
        _prep_body,
        grid=(s // tm,),
        in_specs=[pl.BlockSpec((bsz, tm, A_W), lambda i: (0, i, 0)),
                  pl.BlockSpec((1, 1, MLA_Q_RANK), lsel),
                  pl.BlockSpec((1, 1, MLA_KV_RANK), lsel),
                  pl.BlockSpec((1, MLA_HEADS * MLA_QK, MLA_Q_RANK), lsel),
                  pl.BlockSpec((1, MLA_KV_RANK, MLA_HEADS * MLA_NOPE), lsel),
                  pl.BlockSpec((1, MLA_HEADS * MLA_V, MLA_KV_RANK), lsel),
                  pl.BlockSpec((tm, LANES), lambda i: (i, 0)),
                  pl.BlockSpec((tm, LANES), lambda i: (i, 0)),
                  pl.BlockSpec((LANES, tm), lambda i: (0, i)),
                  pl.BlockSpec((LANES, tm), lambda i: (0, i))],
        out_specs=[pl.BlockSpec((bsz, MLA_HEADS, MLA_QK, tm), lambda i: (0, 0, 0, i)),
                   pl.BlockSpec((bsz, MLA_HEADS, tm, MLA_QK), lambda i: (0, 0, i, 0)),
                   pl.BlockSpec((bsz, MLA_HEADS, MLA_V, tm), lambda i: (0, 0, 0, i))],
        out_shape=[jax.ShapeDtypeStruct((bsz, MLA_HEADS, MLA_QK, s), BF16),
                   jax.ShapeDtypeStruct((bsz, MLA_HEADS, s, MLA_QK), BF16),
                   jax.ShapeDtypeStruct((bsz, MLA_HEADS, MLA_V, s), BF16)],
        compiler_params=_cparams(("parallel",), 40),
        name="mla_prep",
    )(a, g_q, g_kv, w_uq_t, w_ukv_k, w_ukv_vt, cos4, sin4, cos4_t, sin4_t)


def _attn_body(qt_ref, k_ref, vt_ref, o_ref, *, subs, zero_rows):
    work = [(q0, tq, c, c is chunks[0], c is chunks[-1]) for q0, tq, chunks in subs for c in chunks]
    score = lambda w: _dot(k_ref[0, 0, w[2][0]:w[2][0] + w[2][1], :],
                           qt_ref[0, 0, :, w[0]:w[0] + w[1]])
    pending = [score(w) for w in work[:ATTN_AHEAD]]
    m = l = acc = None
    for idx, (q0, tq, (start, size), first, last) in enumerate(work):
        if first:
            m = jnp.full((1, tq), -jnp.inf, F32)
            l = jnp.zeros((1, tq), F32)
            acc = jnp.zeros((MLA_V, tq), F32)
        st = pending.pop(0)
        if idx + ATTN_AHEAD < len(work):
            pending.append(score(work[idx + ATTN_AHEAD]))
        m_new = jnp.maximum(m, jnp.max(st, axis=0, keepdims=True))
        p = jnp.exp2(st - m_new)
        alpha = jnp.exp2(m - m_new)
        l = alpha * l + jnp.sum(p, axis=0, keepdims=True)
        acc = alpha * acc + _dot(vt_ref[0, 0, :, start:start + size], p.astype(BF16))
        m = m_new
        if last:
            o_ref[0, q0:q0 + tq, :] = (acc / l).T.astype(o_ref.dtype)
    if zero_rows is not None:
        o_ref[0, zero_rows[0]:zero_rows[1], :] = jnp.zeros((zero_rows[1] - zero_rows[0], MLA_V), o_ref.dtype)


ATTN_TQ = 512
ATTN_TK = 512
ATTN_AHEAD = 2


def _attn_call(qt, k, vt, n_ctx, with_ctx):
    bsz, nh, s, dk = k.shape
    n_lat = s - n_ctx
    assert n_lat % ATTN_TQ == 0 and n_lat % ATTN_TK == 0
    ctx_chunk = (n_lat, n_ctx)
    lat_chunks = (ctx_chunk,) + tuple((j * ATTN_TK, ATTN_TK) for j in range(n_lat // ATTN_TK))
    subs = [(j * ATTN_TQ, ATTN_TQ, lat_chunks) for j in range(n_lat // ATTN_TQ)]
    if with_ctx:
        subs.append((n_lat, n_ctx, (ctx_chunk,)))
    return pl.pallas_call(
        functools.partial(_attn_body, subs=tuple(subs), zero_rows=None if with_ctx else (n_lat, s)),
        grid=(bsz, nh),
        in_specs=[pl.BlockSpec((1, 1, dk, s), lambda b, h: (b, h, 0, 0)),
                  pl.BlockSpec((1, 1, s, dk), lambda b, h: (b, h, 0, 0)),
                  pl.BlockSpec((1, 1, MLA_V, s), lambda b, h: (b, h, 0, 0))],
        out_specs=pl.BlockSpec((1, s, MLA_V), lambda b, h: (b, 0, h)),
        out_shape=jax.ShapeDtypeStruct((bsz, s, nh * MLA_V), BF16),
        compiler_params=_cparams(("parallel", "parallel"), 48),
        name="mla_attn",
    )(qt, k, vt)


ML_BLOCK_CHUNKS = 2


def _mlstm_body(xf_ref, gf_ref, xb_ref, gb_ref, bias_ref, hf_ref, hb_ref, c_ref, m_ref):
    lc = ML_CHUNK
    assert lc == LANES

    @pl.when(pl.program_id(0) == 0)
    def _():
        c_ref[...] = jnp.zeros_like(c_ref)
        m_ref[...] = jnp.zeros_like(m_ref)

    nprob = xf_ref.shape[0] * 2 * ML_HEADS
    s_io = lax.broadcasted_iota(jnp.int32, (lc, lc), 0)
    t_io = lax.broadcasted_iota(jnp.int32, (lc, lc), 1)
    lane = lax.broadcasted_iota(jnp.int32, (lc, LANES), 1)
    row = lax.broadcasted_iota(jnp.int32, (LANES, lc), 0)
    rowp = lax.broadcasted_iota(jnp.int32, (nprob, lc), 0)
    bias = bias_ref[...]
    ones_sq = jnp.ones((lc, lc), BF16)
    nsub = xf_ref.shape[1] // lc
    for t in range(nsub):
        _mlstm_chunk((xf_ref, xb_ref), (gf_ref, gb_ref), (hf_ref, hb_ref), c_ref, m_ref,
                     (t * lc, (nsub - 1 - t) * lc), bias, ones_sq, (s_io, t_io, lane, row, rowp))


def _mlstm_chunk(x_refs, g_refs, o_refs, c_ref, m_ref, row0, bias, ones_sq, iotas):
    lc = ML_CHUNK
    ngate = 4 * ML_HEADS
    s_io, t_io, lane, row, rowp = iotas
    nprob = rowp.shape[0]
    probs = []
    c_rows = jnp.zeros((nprob, lc), F32)
    for bb, d in [(bb, d) for bb in range(x_refs[0].shape[0]) for d in range(2)]:
        x_ref, g_ref = x_refs[d], g_refs[d]
        rows = slice(row0[d], row0[d] + lc)
        mask = (s_io <= t_io) if d == 0 else (s_io >= t_io)
        gt = (g_ref[bb, rows, :] + bias).T[0:ngate]
        lf = jax.nn.log_sigmoid(gt)
        hi = lf.astype(BF16)
        r1 = lf - hi.astype(F32)
        mid = r1.astype(BF16)
        lo = (r1 - mid.astype(F32)).astype(BF16)
        sums = _dot(jnp.concatenate([hi, mid, lo], axis=0),
                    jnp.concatenate([mask.astype(BF16), ones_sq], axis=1))
        sums = sums[0:ngate] + sums[ngate:2 * ngate] + sums[2 * ngate:3 * ngate]
        b_run, b_tot = sums[:, 0:lc], sums[:, lc:2 * lc]
        x = x_ref[bb, rows, :]
        for pair in range(ML_HEADS // 2):
            qs = x[:, pair * LANES:(pair + 1) * LANES] * (ML_DH ** -0.5)
            ks = x[:, ML_W + pair * LANES:ML_W + (pair + 1) * LANES]
            vt = x[:, 2 * ML_W + pair * LANES:2 * ML_W + (pair + 1) * LANES].T
            for odd in range(2):
                h = 2 * pair + odd
                ci, cf = d * 2 * ML_HEADS + h, d * 2 * ML_HEADS + ML_HEADS + h
                own = (lane >= ML_DH) if odd else (lane < ML_DH)
                own_r = (row >= ML_DH) if odd else (row < ML_DH)
                den_row = 0 if odd else ML_DH
                j = (bb * 2 + d) * ML_HEADS + h
                brow, irow = b_run[cf:cf + 1], gt[ci:ci + 1]
                c_rows = jnp.where(rowp == j, brow - irow, c_rows)
                probs.append(dict(
                    j=j, bb=bb, d=d, pair=pair, odd=odd, mask=mask, den_row=den_row, brow=brow, irow=irow,
                    btot=b_tot[cf:cf + 1],
                    qm=jnp.where(own, qs, 0.0).astype(BF16), ks=ks.astype(BF16),
                    km=jnp.where(own, ks, 0.0).astype(BF16),
                    vaug=jnp.where(own_r, vt, jnp.where(row == den_row, 1.0, 0.0))))

    c_cols = jnp.concatenate([c_rows, jnp.zeros((LANES - nprob, lc), F32)], axis=0).T

    for p in probs:
        p["c_old"] = c_ref[p["j"]]
        p["kq"] = _dot_nt(p["ks"], p["qm"])
        p["inter"] = _dot_nt(p["c_old"].astype(BF16), p["qm"])
    for p in probs:
        j = p["j"]
        m_prev = m_ref[j:j + 1, :]
        dt = jnp.where(p["mask"], p["brow"] - c_cols[:, j:j + 1], -jnp.inf)
        inter_m = p["brow"] + m_prev
        m_row = jnp.maximum(inter_m, jnp.max(dt, axis=0, keepdims=True))
        p["st"] = (p["kq"] * jnp.exp(dt - m_row)).astype(BF16)
        p["w_inter"] = jnp.exp(inter_m - m_row)
        p["floor"] = jnp.exp(-m_row)
        grow = p["btot"] - p["brow"] + p["irow"]
        m_new = jnp.maximum(p["btot"] + m_prev, jnp.max(grow, axis=1, keepdims=True))
        p["w_old"] = jnp.exp(p["btot"] + m_prev - m_new)
        p["wv"] = (p["vaug"] * jnp.exp(grow - m_new)).astype(BF16)
        m_ref[j:j + 1, :] = m_new
    outs = {}
    for p in probs:
        ht = _dot(p["vaug"].astype(BF16), p["st"]) + p["w_inter"] * p["inter"]
        den = ht[p["den_row"]:p["den_row"] + 1]
        outs[(p["bb"], p["d"], p["pair"], p["odd"])] = ht / jnp.maximum(jnp.abs(den), p["floor"])
    for bb, d, pair in sorted({(p["bb"], p["d"], p["pair"]) for p in probs}):
        both = jnp.where(row < ML_DH, outs[(bb, d, pair, 0)], outs[(bb, d, pair, 1)])
        o_refs[d][bb, row0[d]:row0[d] + lc, pair * LANES:(pair + 1) * LANES] = both.T
    for p in probs:
        c_ref[p["j"]] = p["w_old"] * p["c_old"] + _dot(p["wv"], p["km"])


def _mlstm_call(mq, mg, bias_p, n_ctx):
    bsz, s, _ = mq.shape
    rows = ML_BLOCK_CHUNKS * ML_CHUNK
    assert n_ctx % rows == 0 and s % rows == 0
    nch, ncc = s // rows, n_ctx // rows
    nlc = nch - ncc
    fwd = lambda j: (0, jnp.where(j < ncc, nlc + j, j - ncc), 0)
    bwd = lambda j: (0, nch - 1 - j, 0)
    out = jax.ShapeDtypeStruct((bsz, s, ML_W), F32)
    nstate = bsz * 2 * ML_HEADS
    assert nstate <= LANES
    return pl.pallas_call(
        _mlstm_body,
        grid=(nch,),
        in_specs=[pl.BlockSpec((bsz, rows, MQ_W), fwd), pl.BlockSpec((bsz, rows, MG_W), fwd),
                  pl.BlockSpec((bsz, rows, MQ_W), bwd), pl.BlockSpec((bsz, rows, MG_W), bwd),
                  pl.BlockSpec((1, MG_W), lambda j: (0, 0))],
        out_specs=[pl.BlockSpec((bsz, rows, ML_W), fwd), pl.BlockSpec((bsz, rows, ML_W), bwd)],
        out_shape=[out, out],
        scratch_shapes=[pltpu.VMEM((nstate, LANES, LANES), F32),
                        pltpu.VMEM((nstate, LANES), F32)],
        compiler_params=_cparams(("arbitrary",), 48),
        name="mlstm",
    )(mq, mg, mq, mg, bias_p)


LRU_PITCH_PAD = 8
LRU_UNROLL = 8


def _lru_body(xb_ref, gb_ref, cw_ref, cb_ref, wa_ref, ba_ref, wx_ref, bx_ref, lam_ref, y_ref,
              xp_ref, xs_ref, a_ref, u_ref, *, n_ctx, s_len):
    n_lat = s_len - n_ctx
    pad = SUBLANES
    lat_off = n_ctx + 2 * pad
    zeros = jnp.zeros((pad, LANES), F32)
    xp_ref[0:pad, :] = zeros
    xp_ref[pad + n_ctx:lat_off, :] = zeros
    xp_ref[lat_off + n_lat:lat_off + n_lat + pad, :] = zeros
    xp_ref[pad:pad + n_ctx, :] = xb_ref[0, n_lat:n_lat + n_ctx, :]
    cchunk = 512

    def copy_body(c, _):
        src = pl.multiple_of(c * cchunk, SUBLANES)
        dst = pl.multiple_of(lat_off + c * cchunk, SUBLANES)
        xp_ref[pl.ds(dst, cchunk), :] = xb_ref[0, pl.ds(src, cchunk), :]
        return 0

    lax.fori_loop(0, n_lat // cchunk, copy_body, 0)

    cw = cw_ref[...]
    cb = cb_ref[...]

    def conv(src0, dst0, n):
        acc = cb + xp_ref[src0 - CONV_LEFT:src0 - CONV_LEFT + n, :] * cw[0:1, :]
        for j in range(1, CONV_W):
            acc = acc + xp_ref[src0 - CONV_LEFT + j:src0 - CONV_LEFT + j + n, :] * cw[j:j + 1, :]
        xs_ref[dst0:dst0 + n, :] = acc

    conv(pad, n_lat, n_ctx)
    for c in range(n_lat // cchunk):
        conv(lat_off + c * cchunk, c * cchunk, cchunk)

    seg_lat = n_lat // SUBLANES
    seg_ctx = n_ctx // SUBLANES
    p_lat, p_ctx = seg_lat + LRU_PITCH_PAD, seg_ctx + LRU_PITCH_PAD
    ctx_base = SUBLANES * p_lat
    row_io = lax.broadcasted_iota(jnp.int32, (SUBLANES, LANES), 0)

    def gates(x, d):
        xb16 = x.astype(BF16)
        r = _sigmoid(_dot(xb16, wa_ref[d]) + ba_ref[d])
        i = _sigmoid(_dot(xb16, wx_ref[d]) + bx_ref[d])
        log_a = (-LRU_C) * r * jax.nn.softplus(-lam_ref[d])
        a = jnp.exp(log_a)
        return a, jnp.sqrt(jnp.tanh(-log_a) * (1.0 + a * a)) * (i * x)

    def fill(d):
        def lat_body(r, _):
            src = pl.multiple_of(r * seg_lat, SUBLANES)
            dst = pl.multiple_of(r * p_lat, SUBLANES)
            a, u = gates(xs_ref[pl.ds(src, seg_lat), :], d)
            a_ref[d, pl.ds(dst, seg_lat), :] = a
            u_ref[d, pl.ds(dst, seg_lat), :] = u
            return 0

        lax.fori_loop(0, SUBLANES, lat_body, 0)
        a, u = gates(xs_ref[n_lat:n_lat + n_ctx, :], d)
        for r in range(SUBLANES):
            a_ref[d, ctx_base + r * p_ctx:ctx_base + r * p_ctx + seg_ctx, :] = a[r * seg_ctx:(r + 1) * seg_ctx, :]
            u_ref[d, ctx_base + r * p_ctx:ctx_base + r * p_ctx + seg_ctx, :] = u[r * seg_ctx:(r + 1) * seg_ctx, :]

    def scan(base, n, pitch, h0s):
        def block(tb, carry):
            idx = [[pl.ds(base + (tb * LRU_UNROLL + k if d == 0 else n - 1 - tb * LRU_UNROLL - k),
                          SUBLANES, stride=pitch) for k in range(LRU_UNROLL)] for d in range(2)]
            av = [[a_ref[d, i, :] for i in idx[d]] for d in range(2)]
            uv = [[u_ref[d, i, :] for i in idx[d]] for d in range(2)]
            carry = list(carry)
            for k in range(LRU_UNROLL):
                for d in range(2):
                    h, acum = carry[d]
                    h = av[d][k] * h + uv[d][k]
                    acum = acum * av[d][k]
                    carry[d] = (h, acum)
                    uv[d][k], av[d][k] = h, acum
            for d in range(2):
                for k in range(LRU_UNROLL):
                    u_ref[d, idx[d][k], :] = uv[d][k]
                    a_ref[d, idx[d][k], :] = av[d][k]
            return tuple(carry)

        init = (jnp.zeros((SUBLANES, LANES), F32), jnp.ones((SUBLANES, LANES), F32))
        ends = lax.fori_loop(0, n // LRU_UNROLL, block, (init, init))
        result = []
        for d in range(2):
            h_end, a_end = ends[d]
            carry = h0s[d]
            cvec = jnp.zeros((SUBLANES, LANES), F32)
            for r in (range(SUBLANES) if d == 0 else range(SUBLANES - 1, -1, -1)):
                cvec = jnp.where(row_io == r, carry, cvec)
                carry = h_end[r:r + 1, :] + a_end[r:r + 1, :] * carry
            result.append((cvec, carry))
        return result

    fill(0)
    fill(1)
    zero_state = jnp.zeros((1, LANES), F32)
    ctx_res = scan(ctx_base, seg_ctx, p_ctx, (zero_state, zero_state))
    lat_res = scan(0, seg_lat, p_lat, (ctx_res[0][1], ctx_res[1][1]))
    carries = {(d, "ctx"): ctx_res[d][0] for d in range(2)}
    carries.update({(d, "lat"): lat_res[d][0] for d in range(2)})

    def emit(kind, r, dst0, src0, n):
        hsum = None
        for d in range(2):
            c_in = carries[d, kind][r:r + 1, :]
            part = u_ref[d, src0:src0 + n, :] + a_ref[d, src0:src0 + n, :] * c_in
            hsum = part if hsum is None else hsum + part
        y_ref[0, dst0:dst0 + n, :] = jax.nn.gelu(gb_ref[0, dst0:dst0 + n, :]) * hsum

    for r in range(SUBLANES):
        emit("lat", r, r * seg_lat, r * p_lat, seg_lat)
        emit("ctx", r, n_lat + r * seg_ctx, ctx_base + r * p_ctx, seg_ctx)


def _lru_call(r, conv_w, conv_b, wa_bd, b_a, wx_bd, b_x, lam, layer, n_ctx):
    bsz, s, _ = r.shape
    nh = LRU_W // LANES
    n_lat = s - n_ctx
    scan_rows = SUBLANES * (n_lat // SUBLANES + LRU_PITCH_PAD) + SUBLANES * (n_ctx // SUBLANES + LRU_PITCH_PAD)
    vec = lambda b, c: (layer, 0, 0, c)
    return pl.pallas_call(
        functools.partial(_lru_body, n_ctx=n_ctx, s_len=s),
        grid=(bsz, nh),
        in_specs=[pl.BlockSpec((1, s, LANES), lambda b, c: (b, 0, c)),
                  pl.BlockSpec((1, s, LANES), lambda b, c: (b, 0, nh + c)),
                  pl.BlockSpec((None, CONV_W, LANES), lambda b, c: (layer, 0, c)),
                  pl.BlockSpec((None, 1, LANES), lambda b, c: (layer, 0, c)),
                  pl.BlockSpec((None, None, 2, LANES, LANES), lambda b, c: (layer, c, 0, 0, 0)),
                  pl.BlockSpec((None, 2, 1, LANES), vec),
                  pl.BlockSpec((None, None, 2, LANES, LANES), lambda b, c: (layer, c, 0, 0, 0)),
                  pl.BlockSpec((None, 2, 1, LANES), vec),
                  pl.BlockSpec((None, 2, 1, LANES), vec)],
        out_specs=pl.BlockSpec((1, s, LANES), lambda b, c: (b, 0, c)),
        out_shape=jax.ShapeDtypeStruct((bsz, s, LRU_W), F32),
        scratch_shapes=[pltpu.VMEM((s + 3 * SUBLANES, LANES), F32),
                        pltpu.VMEM((s, LANES), F32),
                        pltpu.VMEM((2, scan_rows, LANES), F32),
                        pltpu.VMEM((2, scan_rows, LANES), F32)],
        compiler_params=_cparams(("parallel", "parallel"), 48),
        name="rglru",
    )(r, r, conv_w, conv_b, wa_bd, b_a, wx_bd, b_x, lam)


MLP_FF_CHUNK = 2048


def _mix_mlp_body(x_ref, m_ref, ya_ref, hf_ref, hb_ref, og_ref, yc_ref, wo_ref, hn_ref, w1_ref, w2_ref, fg_ref,
                  o_ref, x1_ref, u_ref, acc_ref, *, tm, ctx_start, final):
    b, i, k = pl.program_id(0), pl.program_id(1), pl.program_id(2)

    @pl.when(k == 0)
    def _():
        hsum = hf_ref[0] + hb_ref[0]
        sq = hsum * hsum
        hi = sq.astype(BF16)
        lo = (sq - hi.astype(F32)).astype(BF16)
        msq = _dot(hi, hn_ref[...]) + _dot(lo, hn_ref[...])
        yb = (_sigmoid(og_ref[0]) * (hsum * lax.rsqrt(msq + EPS))).astype(BF16)
        na, nb = MLA_HEADS * MLA_V, MLA_HEADS * MLA_V + ML_W
        y = (_dot(ya_ref[0], wo_ref[0, 0:na, :]) + _dot(yb, wo_ref[0, na:nb, :])
             + _dot(yc_ref[0].astype(BF16), wo_ref[0, nb:, :]))
        x1 = x_ref[0] + _mod_rows(m_ref, b, i * tm, tm, ctx_start, 2) * y
        x1_ref[...] = x1
        shift = _mod_rows(m_ref, b, i * tm, tm, ctx_start, 3)
        scale = _mod_rows(m_ref, b, i * tm, tm, ctx_start, 4)
        u_ref[...] = (_rms(x1) * (1.0 + scale) + shift).astype(BF16)
        acc_ref[...] = jnp.zeros_like(acc_ref)

    hid = jnp.maximum(_dot(u_ref[...], w1_ref[0]), 0.0)
    acc_ref[...] += _dot((hid * hid).astype(BF16), w2_ref[0])

    @pl.when(k == pl.num_programs(2) - 1)
    def _():
        res = x1_ref[...] + _mod_rows(m_ref, b, i * tm, tm, ctx_start, 5) * acc_ref[...]
        if final:
            res = _rms(res) * fg_ref[...]
        o_ref[0] = res


def _mix_mlp_call(x, mods, ya, hf, hb, mq, yc, w_out, head_avg, w1, w2, final_g, layer, n_ctx, final):
    bsz, s, d = x.shape
    rows_out = s - n_ctx if final else s
    tm = _pick_tile(rows_out, (544, 512, 256))
    fc = MLP_FF_CHUNK
    row = lambda b, i, k: (b, i, 0)
    lsel = lambda b, i, k: (layer, 0, 0)
    return pl.pallas_call(
        functools.partial(_mix_mlp_body, tm=tm, ctx_start=None if final else s - n_ctx, final=final),
        grid=(bsz, rows_out // tm, D_FF // fc),
        in_specs=[pl.BlockSpec((1, tm, d), row),
                  pl.BlockSpec((1, MOD_ROWS, 6 * d), lsel),
                  pl.BlockSpec((1, tm, MLA_HEADS * MLA_V), row),
                  pl.BlockSpec((1, tm, ML_W), row),
                  pl.BlockSpec((1, tm, ML_W), row),
                  pl.BlockSpec((1, tm, ML_W), lambda b, i, k: (b, i, 3)),
                  pl.BlockSpec((1, tm, LRU_W), row),
                  pl.BlockSpec((1, d, d), lsel),
                  pl.BlockSpec((ML_W, ML_W), lambda b, i, k: (0, 0)),
                  pl.BlockSpec((1, d, fc), lambda b, i, k: (layer, 0, k)),
                  pl.BlockSpec((1, fc, d), lambda b, i, k: (layer, k, 0)),
                  pl.BlockSpec((1, d), lambda b, i, k: (0, 0))],
        out_specs=pl.BlockSpec((1, tm, d), row),
        out_shape=jax.ShapeDtypeStruct((bsz, rows_out, d), F32),
        scratch_shapes=[pltpu.VMEM((tm, d), F32), pltpu.VMEM((tm, d), BF16), pltpu.VMEM((tm, d), F32)],
        compiler_params=_cparams(("parallel", "parallel", "arbitrary"), 48),
        name="mix_mlp",
    )(x, mods, ya, hf, hb, mq, yc, w_out, head_avg, w1, w2, final_g)


def _rope_tables(seq, n_ctx):
    half = MLA_ROPE // 2
    row = jnp.repeat(jnp.arange(seq // GRID_W), GRID_W).astype(F32)
    col = jnp.tile(jnp.arange(GRID_W), seq // GRID_W).astype(F32)
    freqs = 1.0 / (ROPE_BASE ** (jnp.arange(0, half, 2, dtype=F32) / half))
    ang = jnp.concatenate([row[:, None] * freqs, col[:, None] * freqs], axis=-1)
    cos = jnp.concatenate([jnp.cos(ang), jnp.ones((n_ctx, half), F32)], axis=0)
    sin = jnp.concatenate([jnp.sin(ang), jnp.zeros((n_ctx, half), F32)], axis=0)
    return jnp.tile(cos, (1, MLA_HEADS)), jnp.tile(sin, (1, MLA_HEADS))


def _block_diag_halves(w):
    depth = w.shape[0]
    per = LANES // LRU_BD
    wh = w.reshape(depth, 2, LRU_BLOCKS // per, per, LRU_BD, LRU_BD)
    eye = jnp.eye(per, dtype=w.dtype)
    bd = jnp.einsum("ldcpio,pq->ldcpiqo", wh, eye).reshape(depth, 2, LRU_BLOCKS // per, LANES, LANES)
    return bd.transpose(0, 2, 1, 3, 4).astype(BF16)


def kernel(x, c, ctx, c_ctx, w_mod, b_mod, w_in, mla_g_q, mla_w_uq, mla_g_kv, mla_w_ukv, ml_gate_bias,
           lru_conv_w, lru_conv_b, lru_w_a, lru_b_a, lru_w_x, lru_b_x, lru_lam, w_out, w_ff1, w_ff2, final_g):
    bsz, seq, d = x.shape
    n_ctx = ctx.shape[1]
    depth = w_in.shape[0]
    assert bsz <= CTX_MOD_ROW and d == D_MODEL and n_ctx % 256 == 0

    zc = lambda n: jnp.zeros((depth, d, n), w_in.dtype)
    ml0, ml1 = MLA_IN, MLA_IN + 4 * ML_W
    w_in_p = jnp.concatenate([w_in[:, :, :ml0], zc(A_W - MLA_IN), w_in[:, :, ml0:ml1],
                              w_in[:, :, ml1:ml1 + 4 * ML_HEADS], zc(MG_W - 4 * ML_HEADS),
                              w_in[:, :, ml1 + 4 * ML_HEADS:]], axis=-1).astype(BF16)
    half = MLA_ROPE // 2
    uq = mla_w_uq.reshape(depth, MLA_Q_RANK, MLA_HEADS, MLA_QK)
    w_uq_t = jnp.concatenate([uq[..., :MLA_NOPE].reshape(depth, MLA_Q_RANK, -1),
                              uq[..., MLA_NOPE:MLA_NOPE + half].reshape(depth, MLA_Q_RANK, -1),
                              uq[..., MLA_NOPE + half:].reshape(depth, MLA_Q_RANK, -1)],
                             axis=-1).astype(BF16).transpose(0, 2, 1)
    ukv = mla_w_ukv.reshape(depth, MLA_KV_RANK, MLA_HEADS, MLA_NOPE + MLA_V)
    w_ukv_k = ukv[..., :MLA_NOPE].reshape(depth, MLA_KV_RANK, -1).astype(BF16)
    w_ukv_vt = ukv[..., MLA_NOPE:].reshape(depth, MLA_KV_RANK, -1).astype(BF16).transpose(0, 2, 1)
    g_q = mla_g_q.reshape(depth, 1, MLA_Q_RANK)
    g_kv = mla_g_kv.reshape(depth, 1, MLA_KV_RANK)
    bias_p = jnp.pad(ml_gate_bias, ((0, 0), (0, MG_W - 4 * ML_HEADS)))
    wa_bd, wx_bd = _block_diag_halves(lru_w_a), _block_diag_halves(lru_w_x)
    vec4 = lambda v: v.reshape(depth, 2, 1, LRU_W)
    conv_b = lru_conv_b.reshape(depth, 1, LRU_W)
    w_out16, w1_16, w2_16 = w_out.astype(BF16), w_ff1.astype(BF16), w_ff2.astype(BF16)
    head_avg = jnp.kron(jnp.eye(ML_HEADS, dtype=F32), jnp.full((ML_DH, ML_DH), 1.0 / ML_DH, F32)).astype(BF16)
    cos4, sin4 = _rope_tables(seq, n_ctx)
    rope = (cos4, sin4, cos4.T, sin4.T)
    fg = final_g.reshape(1, d)

    cvec = jnp.concatenate([c, jnp.zeros((CTX_MOD_ROW - bsz, d), c.dtype), c_ctx[None, :],
                            jnp.zeros((MOD_ROWS - CTX_MOD_ROW - 1, d), c.dtype)], axis=0)
    mods = _mods_call(cvec, w_mod, b_mod)

    xs = jnp.concatenate([x, ctx], axis=1)
    for l in range(depth):
        last = l == depth - 1
        a, mq, mg, r = _in_call(xs, mods, w_in_p, l, n_ctx)
        qt, k, vt = _prep_call(a, g_q, w_uq_t, g_kv, w_ukv_k, w_ukv_vt, rope, l)
        ya = _attn_call(qt, k, vt, n_ctx, with_ctx=not last)
        hf, hb = _mlstm_call(mq, mg, bias_p[l:l + 1], n_ctx)
        yc = _lru_call(r, lru_conv_w, conv_b, wa_bd, vec4(lru_b_a), wx_bd, vec4(lru_b_x), vec4(lru_lam), l, n_ctx)
        xs = _mix_mlp_call(xs, mods, ya, hf, hb, mq, yc, w_out16, head_avg, w1_16, w2_16, fg, l, n_ctx,
                           final=last)
    return xs
```

```python
import functools

import jax
import jax.numpy as jnp
from jax import lax
from jax.experimental import pallas as pl
from jax.experimental.pallas import tpu as pltpu

F32 = jnp.float32
BF16 = jnp.bfloat16

D_MODEL = 1024
DEPTH = 4
GRID_W = 64
N_CTX = 256
MLA_HEADS = 4
MLA_Q_RANK = 256
MLA_KV_RANK = 128
MLA_NOPE = 128
MLA_ROPE = 64
MLA_V = 128
MLA_QK = MLA_NOPE + MLA_ROPE
MLA_SCALE = MLA_QK ** -0.5
ROPE_BASE = 10000.0
ML_HEADS = 4
ML_DH = 64
ML_W = ML_HEADS * ML_DH
ML_CHUNK = 128
LRU_W = 256
LRU_BLOCKS = 4
LRU_BD = LRU_W // LRU_BLOCKS
CONV_W = 4
CONV_LEFT = 2
LRU_C = 8.0
D_FF = 4 * D_MODEL
EPS = 1e-6
MLA_IN = MLA_Q_RANK + MLA_KV_RANK + MLA_ROPE
ML_IN = 4 * ML_W + 4 * ML_HEADS
LRU_IN = 2 * LRU_W

LANES = 128
SUBLANES = 8
MOD_ROWS = 8
CTX_MOD_ROW = 4

A_W = 512
MQ_W = 4 * ML_W
MG_W = LANES
R_W = 2 * LRU_W


def _cparams(sem, vmem_mb):
    return pltpu.CompilerParams(dimension_semantics=sem, vmem_limit_bytes=vmem_mb * 1024 * 1024)


def _mod_rows(m_ref, b, row0, tm, ctx_start, seg):
    lo, hi = seg * D_MODEL, (seg + 1) * D_MODEL
    lat = m_ref[0, pl.ds(b, 1), lo:hi]
    if ctx_start is None:
        return lat
    ctx = m_ref[0, CTX_MOD_ROW:CTX_MOD_ROW + 1, lo:hi]
    rows = row0 + lax.broadcasted_iota(jnp.int32, (tm, 1), 0)
    return jnp.where(rows >= ctx_start, ctx, lat)


def _pick_tile(n, candidates):
    return next(t for t in candidates if n % t == 0)


def _rms(x):
    return x * lax.rsqrt(jnp.mean(x * x, axis=-1, keepdims=True) + EPS)


def _sigmoid(x):
    return 0.5 * jnp.tanh(0.5 * x) + 0.5


def _dot(a, b):
    return jnp.dot(a, b, preferred_element_type=F32)


def _dot_nt(a, b):
    return lax.dot_general(a, b, (((1,), (1,)), ((), ())), preferred_element_type=F32)


def _mods_body(c_ref, w_ref, b_ref, o_ref):
    cv = c_ref[...]
    act = (cv * jax.nn.sigmoid(cv)).astype(BF16)
    o_ref[0] = _dot(act, w_ref[0].astype(BF16)) + b_ref[0]


def _mods_call(cvec, w_mod, b_mod):
    depth, d, n = w_mod.shape
    tn = 1536
    return pl.pallas_call(
        _mods_body,
        grid=(depth, n // tn),
        in_specs=[pl.BlockSpec((MOD_ROWS, d), lambda l, j: (0, 0)),
                  pl.BlockSpec((1, d, tn), lambda l, j: (l, 0, j)),
                  pl.BlockSpec((1, 1, tn), lambda l, j: (l, 0, j))],
        out_specs=pl.BlockSpec((1, MOD_ROWS, tn), lambda l, j: (l, 0, j)),
        out_shape=jax.ShapeDtypeStruct((depth, MOD_ROWS, n), F32),
        compiler_params=_cparams(("arbitrary", "arbitrary"), 40),
        name="mods",
    )(cvec, w_mod, b_mod.reshape(depth, 1, n))


def _in_body(x_ref, m_ref, wa_ref, wq_ref, wg_ref, wr_ref, a_ref, q_ref, g_ref, r_ref, *, tm, ctx_start):
    b, i = pl.program_id(0), pl.program_id(1)
    xn = _rms(x_ref[0])
    shift = _mod_rows(m_ref, b, i * tm, tm, ctx_start, 0)
    scale = _mod_rows(m_ref, b, i * tm, tm, ctx_start, 1)
    u = (xn * (1.0 + scale) + shift).astype(BF16)
    a_ref[0] = _dot(u, wa_ref[0])
    q_ref[0] = _dot(u, wq_ref[0])
    g_ref[0] = _dot(u, wg_ref[0])
    r_ref[0] = _dot(u, wr_ref[0])


def _in_call(x, mods, w_groups, layer, n_ctx):
    bsz, s, d = x.shape
    tm = _pick_tile(s, (1088, 256))
    row = lambda b, i: (b, i, 0)
    return pl.pallas_call(
        functools.partial(_in_body, tm=tm, ctx_start=s - n_ctx),
        grid=(bsz, s // tm),
        in_specs=[pl.BlockSpec((1, tm, d), row),
                  pl.BlockSpec((1, MOD_ROWS, 6 * d), lambda b, i: (layer, 0, 0)),
                  *[pl.BlockSpec((1, d, w), lambda b, i: (layer, 0, 0)) for w in (A_W, MQ_W, MG_W, R_W)]],
        out_specs=[pl.BlockSpec((1, tm, A_W), row), pl.BlockSpec((1, tm, MQ_W), row),
                   pl.BlockSpec((1, tm, MG_W), row), pl.BlockSpec((1, tm, R_W), row)],
        out_shape=[jax.ShapeDtypeStruct((bsz, s, w), F32) for w in (A_W, MQ_W, MG_W, R_W)],
        compiler_params=_cparams(("parallel", "parallel"), 52),
        name="in_proj",
    )(x, mods, *w_groups)


LOG2E = 1.4426950408889634


def _prep_body(a_ref, gq_ref, gkv_ref, wuqt_ref, wukvk_ref, wukvvt_ref, cos_ref, sin_ref, cost_ref, sint_ref,
               qt_ref, k_ref, vt_ref):
    for bb in range(a_ref.shape[0]):
        _prep_sample(bb, a_ref, gq_ref, gkv_ref, wuqt_ref, wukvk_ref, wukvvt_ref, cos_ref, sin_ref, cost_ref,
                     sint_ref, qt_ref, k_ref, vt_ref)


def _prep_sample(bb, a_ref, gq_ref, gkv_ref, wuqt_ref, wukvk_ref, wukvvt_ref, cos_ref, sin_ref, cost_ref, sint_ref,
                 qt_ref, k_ref, vt_ref):
    a = a_ref[bb]
    nq, nkv = MLA_Q_RANK, MLA_Q_RANK + MLA_KV_RANK
    half = MLA_ROPE // 2
    cq_t = (_rms(a[:, 0:nq]) * gq_ref[0]).T.astype(BF16)
    ckv = _rms(a[:, nq:nkv]) * gkv_ref[0]
    q_t = _dot(wuqt_ref[0], cq_t) * (MLA_SCALE * LOG2E)
    cos_t, sin_t = cost_ref[...], sint_ref[...]
    k_nope = _dot(ckv.astype(BF16), wukvk_ref[0])
    v_t = _dot(wukvvt_ref[0], ckv.T.astype(BF16))
    k1, k2 = a[:, nkv:nkv + half], a[:, nkv + half:nkv + 2 * half]
    c32, s32 = cos_ref[...], sin_ref[...]
    kr1 = (k1 * c32 - k2 * s32).astype(BF16)
    kr2 = (k1 * s32 + k2 * c32).astype(BF16)
    for h in range(MLA_HEADS):
        q_h = q_t[h * MLA_QK:(h + 1) * MLA_QK]
        x1, x2 = q_h[MLA_NOPE:MLA_NOPE + half], q_h[MLA_NOPE + half:MLA_QK]
        qt_ref[bb, h, 0:MLA_NOPE, :] = q_h[0:MLA_NOPE].astype(BF16)
        qt_ref[bb, h, MLA_NOPE:MLA_NOPE + half, :] = (x1 * cos_t - x2 * sin_t).astype(BF16)
        qt_ref[bb, h, MLA_NOPE + half:MLA_QK, :] = (x1 * sin_t + x2 * cos_t).astype(BF16)
        k_ref[bb, h, :, 0:MLA_NOPE] = k_nope[:, h * MLA_NOPE:(h + 1) * MLA_NOPE].astype(BF16)
        k_ref[bb, h, :, MLA_NOPE:MLA_NOPE + half] = kr1
        k_ref[bb, h, :, MLA_NOPE + half:MLA_QK] = kr2
        vt_ref[bb, h] = v_t[h * MLA_V:(h + 1) * MLA_V].astype(BF16)


def _prep_call(a, g_q, w_uq_t, g_kv, w_ukv_k, w_ukv_vt, rope, layer):
    bsz, s, _ = a.shape
    tm = 256
    half = MLA_ROPE // 2
    lsel = lambda i: (layer, 0, 0)
    cos, sin, cos_t, sin_t = rope
    return pl.pallas_call(
        _prep_body,
        grid=(s // tm,),
        in_specs=[pl.BlockSpec((bsz, tm, A_W), lambda i: (0, i, 0)),
                  pl.BlockSpec((1, 1, MLA_Q_RANK), lsel),
                  pl.BlockSpec((1, 1, MLA_KV_RANK), lsel),
                  pl.BlockSpec((1, MLA_HEADS * MLA_QK, MLA_Q_RANK), lsel),
                  pl.BlockSpec((1, MLA_KV_RANK, MLA_HEADS * MLA_NOPE), lsel),
                  pl.BlockSpec((1, MLA_HEADS * MLA_V, MLA_KV_RANK), lsel),
                  pl.BlockSpec((tm, half), lambda i: (i, 0)),
                  pl.BlockSpec((tm, half), lambda i: (i, 0)),
                  pl.BlockSpec((half, tm), lambda i: (0, i)),
                  pl.BlockSpec((half, tm), lambda i: (0, i))],
        out_specs=[pl.BlockSpec((bsz, MLA_HEADS, MLA_QK, tm), lambda i: (0, 0, 0, i)),
                   pl.BlockSpec((bsz, MLA_HEADS, tm, MLA_QK), lambda i: (0, 0, i, 0)),
                   pl.BlockSpec((bsz, MLA_HEADS, MLA_V, tm), lambda i: (0, 0, 0, i))],
        out_shape=[jax.ShapeDtypeStruct((bsz, MLA_HEADS, MLA_QK, s), BF16),
                   jax.ShapeDtypeStruct((bsz, MLA_HEADS, s, MLA_QK), BF16),
                   jax.ShapeDtypeStruct((bsz, MLA_HEADS, MLA_V, s), BF16)],
        compiler_params=_cparams(("parallel",), 40),
        name="mla_prep",
    )(a, g_q, g_kv, w_uq_t, w_ukv_k, w_ukv_vt, cos, sin, cos_t, sin_t)


def _attn_body(qt_ref, k_ref, vt_ref, o_ref, *, subs, zero_rows):
    work = [(q0, tq, c, c is chunks[0], c is chunks[-1]) for q0, tq, chunks in subs for c in chunks]
    score = lambda w: _dot(k_ref[0, 0, w[2][0]:w[2][0] + w[2][1], :],
                           qt_ref[0, 0, :, w[0]:w[0] + w[1]])
    pending = [score(w) for w in work[:ATTN_AHEAD]]
    m = l = acc = None
    for idx, (q0, tq, (start, size), first, last) in enumerate(work):
        if first:
            m = jnp.full((1, tq), -jnp.inf, F32)
            l = jnp.zeros((1, tq), F32)
            acc = jnp.zeros((MLA_V, tq), F32)
        st = pending.pop(0)
        if idx + ATTN_AHEAD < len(work):
            pending.append(score(work[idx + ATTN_AHEAD]))
        m_new = jnp.maximum(m, jnp.max(st, axis=0, keepdims=True))
        p = jnp.exp2(st - m_new)
        alpha = jnp.exp2(m - m_new)
        l = alpha * l + jnp.sum(p, axis=0, keepdims=True)
        acc = alpha * acc + _dot(vt_ref[0, 0, :, start:start + size], p.astype(BF16))
        m = m_new
        if last:
            o_ref[0, q0:q0 + tq, :] = (acc / l).T.astype(o_ref.dtype)
    if zero_rows is not None:
        o_ref[0, zero_rows[0]:zero_rows[1], :] = jnp.zeros((zero_rows[1] - zero_rows[0], MLA_V), o_ref.dtype)


ATTN_TQ = 512
ATTN_TK = 512
ATTN_AHEAD = 2


def _attn_call(qt, k, vt, n_ctx, with_ctx):
    bsz, nh, s, dk = k.shape
    n_lat = s - n_ctx
    assert n_lat % ATTN_TQ == 0 and n_lat % ATTN_TK == 0
    ctx_chunk = (n_lat, n_ctx)
    lat_chunks = (ctx_chunk,) + tuple((j * ATTN_TK, ATTN_TK) for j in range(n_lat // ATTN_TK))
    subs = [(j * ATTN_TQ, ATTN_TQ, lat_chunks) for j in range(n_lat // ATTN_TQ)]
    if with_ctx:
        subs.append((n_lat, n_ctx, (ctx_chunk,)))
    return pl.pallas_call(
        functools.partial(_attn_body, subs=tuple(subs), zero_rows=None if with_ctx else (n_lat, s)),
        grid=(bsz, nh),
        in_specs=[pl.BlockSpec((1, 1, dk, s), lambda b, h: (b, h, 0, 0)),
                  pl.BlockSpec((1, 1, s, dk), lambda b, h: (b, h, 0, 0)),
                  pl.BlockSpec((1, 1, MLA_V, s), lambda b, h: (b, h, 0, 0))],
        out_specs=pl.BlockSpec((1, s, MLA_V), lambda b, h: (b, 0, h)),
        out_shape=jax.ShapeDtypeStruct((bsz, s, nh * MLA_V), BF16),
        compiler_params=_cparams(("parallel", "parallel"), 48),
        name="mla_attn",
    )(qt, k, vt)


ML_BLOCK_CHUNKS = 2


def _mlstm_body(xf_ref, gf_ref, xb_ref, gb_ref, bias_ref, hf_ref, hb_ref, c_ref, m_ref):
    lc = ML_CHUNK
    assert lc == LANES

    @pl.when(pl.program_id(0) == 0)
    def _():
        c_ref[...] = jnp.zeros_like(c_ref)
        m_ref[...] = jnp.zeros_like(m_ref)

    nprob = xf_ref.shape[0] * 2 * ML_HEADS
    s_io = lax.broadcasted_iota(jnp.int32, (lc, lc), 0)
    t_io = lax.broadcasted_iota(jnp.int32, (lc, lc), 1)
    lane = lax.broadcasted_iota(jnp.int32, (lc, LANES), 1)
    row = lax.broadcasted_iota(jnp.int32, (LANES, lc), 0)
    rowp = lax.broadcasted_iota(jnp.int32, (nprob, lc), 0)
    bias = bias_ref[...]
    ones_sq = jnp.ones((lc, lc), BF16)
    nsub = xf_ref.shape[1] // lc
    for t in range(nsub):
        _mlstm_chunk((xf_ref, xb_ref), (gf_ref, gb_ref), (hf_ref, hb_ref), c_ref, m_ref,
                     (t * lc, (nsub - 1 - t) * lc), bias, ones_sq, (s_io, t_io, lane, row, rowp))


def _mlstm_chunk(x_refs, g_refs, o_refs, c_ref, m_ref, row0, bias, ones_sq, iotas):
    lc = ML_CHUNK
    ngate = 4 * ML_HEADS
    s_io, t_io, lane, row, rowp = iotas
    nprob = rowp.shape[0]
    probs = []
    c_rows = jnp.zeros((nprob, lc), F32)
    for bb, d in [(bb, d) for bb in range(x_refs[0].shape[0]) for d in range(2)]:
        x_ref, g_ref = x_refs[d], g_refs[d]
        rows = slice(row0[d], row0[d] + lc)
        mask = (s_io <= t_io) if d == 0 else (s_io >= t_io)
        gt = (g_ref[bb, rows, :] + bias).T[0:ngate]
        lf = jax.nn.log_sigmoid(gt)
        hi = lf.astype(BF16)
        r1 = lf - hi.astype(F32)
        mid = r1.astype(BF16)
        lo = (r1 - mid.astype(F32)).astype(BF16)
        sums = _dot(jnp.concatenate([hi, mid, lo], axis=0),
                    jnp.concatenate([mask.astype(BF16), ones_sq], axis=1))
        sums = sums[0:ngate] + sums[ngate:2 * ngate] + sums[2 * ngate:3 * ngate]
        b_run, b_tot = sums[:, 0:lc], sums[:, lc:2 * lc]
        x = x_ref[bb, rows, :]
        for pair in range(ML_HEADS // 2):
            qs = x[:, pair * LANES:(pair + 1) * LANES] * (ML_DH ** -0.5)
            ks = x[:, ML_W + pair * LANES:ML_W + (pair + 1) * LANES]
            vt = x[:, 2 * ML_W + pair * LANES:2 * ML_W + (pair + 1) * LANES].T
            for odd in range(2):
                h = 2 * pair + odd
                ci, cf = d * 2 * ML_HEADS + h, d * 2 * ML_HEADS + ML_HEADS + h
                own = (lane >= ML_DH) if odd else (lane < ML_DH)
                own_r = (row >= ML_DH) if odd else (row < ML_DH)
                den_row = 0 if odd else ML_DH
                j = (bb * 2 + d) * ML_HEADS + h
                brow, irow = b_run[cf:cf + 1], gt[ci:ci + 1]
                c_rows = jnp.where(rowp == j, brow - irow, c_rows)
                probs.append(dict(
                    j=j, bb=bb, d=d, pair=pair, odd=odd, mask=mask, den_row=den_row, brow=brow, irow=irow,
                    btot=b_tot[cf:cf + 1],
                    qm=jnp.where(own, qs, 0.0).astype(BF16), ks=ks.astype(BF16),
                    km=jnp.where(own, ks, 0.0).astype(BF16),
                    vaug=jnp.where(own_r, vt, jnp.where(row == den_row, 1.0, 0.0))))

    c_cols = jnp.concatenate([c_rows, jnp.zeros((LANES - nprob, lc), F32)], axis=0).T

    for p in probs:
        p["c_old"] = c_ref[p["j"]]
        p["kq"] = _dot_nt(p["ks"], p["qm"])
        p["inter"] = _dot_nt(p["c_old"].astype(BF16), p["qm"])
    for p in probs:
        j = p["j"]
        m_prev = m_ref[j:j + 1, :]
        dt = jnp.where(p["mask"], p["brow"] - c_cols[:, j:j + 1], -jnp.inf)
        inter_m = p["brow"] + m_prev
        m_row = jnp.maximum(inter_m, jnp.max(dt, axis=0, keepdims=True))
        p["st"] = (p["kq"] * jnp.exp(dt - m_row)).astype(BF16)
        p["w_inter"] = jnp.exp(inter_m - m_row)
        p["floor"] = jnp.exp(-m_row)
        grow = p["btot"] - p["brow"] + p["irow"]
        m_new = jnp.maximum(p["btot"] + m_prev, jnp.max(grow, axis=1, keepdims=True))
        p["w_old"] = jnp.exp(p["btot"] + m_prev - m_new)
        p["wv"] = (p["vaug"] * jnp.exp(grow - m_new)).astype(BF16)
        m_ref[j:j + 1, :] = m_new
    outs = {}
    for p in probs:
        ht = _dot(p["vaug"].astype(BF16), p["st"]) + p["w_inter"] * p["inter"]
        den = ht[p["den_row"]:p["den_row"] + 1]
        outs[(p["bb"], p["d"], p["pair"], p["odd"])] = ht / jnp.maximum(jnp.abs(den), p["floor"])
    for bb, d, pair in sorted({(p["bb"], p["d"], p["pair"]) for p in probs}):
        both = jnp.where(row < ML_DH, outs[(bb, d, pair, 0)], outs[(bb, d, pair, 1)])
        o_refs[d][bb, row0[d]:row0[d] + lc, pair * LANES:(pair + 1) * LANES] = both.T
    for p in probs:
        c_ref[p["j"]] = p["w_old"] * p["c_old"] + _dot(p["wv"], p["km"])


def _mlstm_call(mq, mg, bias_p, n_ctx):
    bsz, s, _ = mq.shape
    rows = ML_BLOCK_CHUNKS * ML_CHUNK
    assert n_ctx % rows == 0 and s % rows == 0
    nch, ncc = s // rows, n_ctx // rows
    nlc = nch - ncc
    fwd = lambda j: (0, jnp.where(j < ncc, nlc + j, j - ncc), 0)
    bwd = lambda j: (0, nch - 1 - j, 0)
    out = jax.ShapeDtypeStruct((bsz, s, ML_W), F32)
    nstate = bsz * 2 * ML_HEADS
    assert nstate <= LANES
    return pl.pallas_call(
        _mlstm_body,
        grid=(nch,),
        in_specs=[pl.BlockSpec((bsz, rows, MQ_W), fwd), pl.BlockSpec((bsz, rows, MG_W), fwd),
                  pl.BlockSpec((bsz, rows, MQ_W), bwd), pl.BlockSpec((bsz, rows, MG_W), bwd),
                  pl.BlockSpec((1, MG_W), lambda j: (0, 0))],
        out_specs=[pl.BlockSpec((bsz, rows, ML_W), fwd), pl.BlockSpec((bsz, rows, ML_W), bwd)],
        out_shape=[out, out],
        scratch_shapes=[pltpu.VMEM((nstate, LANES, LANES), F32),
                        pltpu.VMEM((nstate, LANES), F32)],
        compiler_params=_cparams(("arbitrary",), 48),
        name="mlstm",
    )(mq, mg, mq, mg, bias_p)


LRU_PITCH_PAD = 8
LRU_UNROLL = 8


def _lru_body(xb_ref, gb_ref, cw_ref, cb_ref, wa_ref, ba_ref, wx_ref, bx_ref, lam_ref, y_ref,
              xp_ref, xs_ref, a_ref, u_ref, *, n_ctx, s_len):
    n_lat = s_len - n_ctx
    pad = SUBLANES
    lat_off = n_ctx + 2 * pad
    zeros = jnp.zeros((pad, LANES), F32)
    xp_ref[0:pad, :] = zeros
    xp_ref[pad + n_ctx:lat_off, :] = zeros
    xp_ref[lat_off + n_lat:lat_off + n_lat + pad, :] = zeros
    xp_ref[pad:pad + n_ctx, :] = xb_ref[0, n_lat:n_lat + n_ctx, :]
    cchunk = 512

    def copy_body(c, _):
        src = pl.multiple_of(c * cchunk, SUBLANES)
        dst = pl.multiple_of(lat_off + c * cchunk, SUBLANES)
        xp_ref[pl.ds(dst, cchunk), :] = xb_ref[0, pl.ds(src, cchunk), :]
        return 0

    lax.fori_loop(0, n_lat // cchunk, copy_body, 0)

    cw = cw_ref[...]
    cb = cb_ref[...]

    def conv(src0, dst0, n):
        acc = cb + xp_ref[src0 - CONV_LEFT:src0 - CONV_LEFT + n, :] * cw[0:1, :]
        for j in range(1, CONV_W):
            acc = acc + xp_ref[src0 - CONV_LEFT + j:src0 - CONV_LEFT + j + n, :] * cw[j:j + 1, :]
        xs_ref[dst0:dst0 + n, :] = acc

    conv(pad, n_lat, n_ctx)
    for c in range(n_lat // cchunk):
        conv(lat_off + c * cchunk, c * cchunk, cchunk)

    seg_lat = n_lat // SUBLANES
    seg_ctx = n_ctx // SUBLANES
    p_lat, p_ctx = seg_lat + LRU_PITCH_PAD, seg_ctx + LRU_PITCH_PAD
    ctx_base = SUBLANES * p_lat
    row_io = lax.broadcasted_iota(jnp.int32, (SUBLANES, LANES), 0)

    def gates(x, d):
        xb16 = x.astype(BF16)
        r = _sigmoid(_dot(xb16, wa_ref[d]) + ba_ref[d])
        i = _sigmoid(_dot(xb16, wx_ref[d]) + bx_ref[d])
        log_a = (-LRU_C) * r * jax.nn.softplus(-lam_ref[d])
        a = jnp.exp(log_a)
        return a, jnp.sqrt(jnp.tanh(-log_a) * (1.0 + a * a)) * (i * x)

    def fill(d):
        def lat_body(r, _):
            src = pl.multiple_of(r * seg_lat, SUBLANES)
            dst = pl.multiple_of(r * p_lat, SUBLANES)
            a, u = gates(xs_ref[pl.ds(src, seg_lat), :], d)
            a_ref[d, pl.ds(dst, seg_lat), :] = a
            u_ref[d, pl.ds(dst, seg_lat), :] = u
            return 0

        lax.fori_loop(0, SUBLANES, lat_body, 0)
        a, u = gates(xs_ref[n_lat:n_lat + n_ctx, :], d)
        for r in range(SUBLANES):
            a_ref[d, ctx_base + r * p_ctx:ctx_base + r * p_ctx + seg_ctx, :] = a[r * seg_ctx:(r + 1) * seg_ctx, :]
            u_ref[d, ctx_base + r * p_ctx:ctx_base + r * p_ctx + seg_ctx, :] = u[r * seg_ctx:(r + 1) * seg_ctx, :]

    def scan(base, n, pitch, h0s):
        def block(tb, carry):
            idx = [[pl.ds(base + (tb * LRU_UNROLL + k if d == 0 else n - 1 - tb * LRU_UNROLL - k),
                          SUBLANES, stride=pitch) for k in range(LRU_UNROLL)] for d in range(2)]
            av = [[a_ref[d, i, :] for i in idx[d]] for d in range(2)]
            uv = [[u_ref[d, i, :] for i in idx[d]] for d in range(2)]
            carry = list(carry)
            for k in range(LRU_UNROLL):
                for d in range(2):
                    h, acum = carry[d]
                    h = av[d][k] * h + uv[d][k]
                    acum = acum * av[d][k]
                    carry[d] = (h, acum)
                    uv[d][k], av[d][k] = h, acum
            for d in range(2):
                for k in range(LRU_UNROLL):
                    u_ref[d, idx[d][k], :] = uv[d][k]
                    a_ref[d, idx[d][k], :] = av[d][k]
            return tuple(carry)

        init = (jnp.zeros((SUBLANES, LANES), F32), jnp.ones((SUBLANES, LANES), F32))
        ends = lax.fori_loop(0, n // LRU_UNROLL, block, (init, init))
        result = []
        for d in range(2):
            h_end, a_end = ends[d]
            carry = h0s[d]
            cvec = jnp.zeros((SUBLANES, LANES), F32)
            for r in (range(SUBLANES) if d == 0 else range(SUBLANES - 1, -1, -1)):
                cvec = jnp.where(row_io == r, carry, cvec)
                carry = h_end[r:r + 1, :] + a_end[r:r + 1, :] * carry
            result.append((cvec, carry))
        return result

    fill(0)
    fill(1)
    zero_state = jnp.zeros((1, LANES), F32)
    ctx_res = scan(ctx_base, seg_ctx, p_ctx, (zero_state, zero_state))
    lat_res = scan(0, seg_lat, p_lat, (ctx_res[0][1], ctx_res[1][1]))
    carries = {(d, "ctx"): ctx_res[d][0] for d in range(2)}
    carries.update({(d, "lat"): lat_res[d][0] for d in range(2)})

    def emit(kind, r, dst0, src0, n):
        hsum = None
        for d in range(2):
            c_in = carries[d, kind][r:r + 1, :]
            part = u_ref[d, src0:src0 + n, :] + a_ref[d, src0:src0 + n, :] * c_in
            hsum = part if hsum is None else hsum + part
        y_ref[0, dst0:dst0 + n, :] = jax.nn.gelu(gb_ref[0, dst0:dst0 + n, :]) * hsum

    for r in range(SUBLANES):
        emit("lat", r, r * seg_lat, r * p_lat, seg_lat)
        emit("ctx", r, n_lat + r * seg_ctx, ctx_base + r * p_ctx, seg_ctx)


def _lru_call(r, conv_w, conv_b, wa_bd, b_a, wx_bd, b_x, lam, layer, n_ctx):
    bsz, s, _ = r.shape
    nh = LRU_W // LANES
    n_lat = s - n_ctx
    scan_rows = SUBLANES * (n_lat // SUBLANES + LRU_PITCH_PAD) + SUBLANES * (n_ctx // SUBLANES + LRU_PITCH_PAD)
    vec = lambda b, c: (layer, 0, 0, c)
    return pl.pallas_call(
        functools.partial(_lru_body, n_ctx=n_ctx, s_len=s),
        grid=(bsz, nh),
        in_specs=[pl.BlockSpec((1, s, LANES), lambda b, c: (b, 0, c)),
                  pl.BlockSpec((1, s, LANES), lambda b, c: (b, 0, nh + c)),
                  pl.BlockSpec((None, CONV_W, LANES), lambda b, c: (layer, 0, c)),
                  pl.BlockSpec((None, 1, LANES), lambda b, c: (layer, 0, c)),
                  pl.BlockSpec((None, None, 2, LANES, LANES), lambda b, c: (layer, c, 0, 0, 0)),
                  pl.BlockSpec((None, 2, 1, LANES), vec),
                  pl.BlockSpec((None, None, 2, LANES, LANES), lambda b, c: (layer, c, 0, 0, 0)),
                  pl.BlockSpec((None, 2, 1, LANES), vec),
                  pl.BlockSpec((None, 2, 1, LANES), vec)],
        out_specs=pl.BlockSpec((1, s, LANES), lambda b, c: (b, 0, c)),
        out_shape=jax.ShapeDtypeStruct((bsz, s, LRU_W), F32),
        scratch_shapes=[pltpu.VMEM((s + 3 * SUBLANES, LANES), F32),
                        pltpu.VMEM((s, LANES), F32),
                        pltpu.VMEM((2, scan_rows, LANES), F32),
                        pltpu.VMEM((2, scan_rows, LANES), F32)],
        compiler_params=_cparams(("parallel", "parallel"), 48),
        name="rglru",
    )(r, r, conv_w, conv_b, wa_bd, b_a, wx_bd, b_x, lam)


MLP_FF_CHUNK = 2048


def _mix_mlp_body(x_ref, m_ref, ya_ref, hf_ref, hb_ref, og_ref, yc_ref, wo_ref, hn_ref, w1_ref, w2_ref, fg_ref,
                  o_ref, x1_ref, u_ref, acc_ref, *, tm, ctx_start, final):
    b, i, k = pl.program_id(0), pl.program_id(1), pl.program_id(2)

    @pl.when(k == 0)
    def _():
        hsum = hf_ref[0] + hb_ref[0]
        sq = hsum * hsum
        hi = sq.astype(BF16)
        lo = (sq - hi.astype(F32)).astype(BF16)
        msq = _dot(hi, hn_ref[...]) + _dot(lo, hn_ref[...])
        yb = (_sigmoid(og_ref[0]) * (hsum * lax.rsqrt(msq + EPS))).astype(BF16)
        na, nb = MLA_HEADS * MLA_V, MLA_HEADS * MLA_V + ML_W
        y = (_dot(ya_ref[0], wo_ref[0, 0:na, :]) + _dot(yb, wo_ref[0, na:nb, :])
             + _dot(yc_ref[0].astype(BF16), wo_ref[0, nb:, :]))
        x1 = x_ref[0] + _mod_rows(m_ref, b, i * tm, tm, ctx_start, 2) * y
        x1_ref[...] = x1
        shift = _mod_rows(m_ref, b, i * tm, tm, ctx_start, 3)
        scale = _mod_rows(m_ref, b, i * tm, tm, ctx_start, 4)
        u_ref[...] = (_rms(x1) * (1.0 + scale) + shift).astype(BF16)
        acc_ref[...] = jnp.zeros_like(acc_ref)

    hid = jnp.maximum(_dot(u_ref[...], w1_ref[0]), 0.0)
    acc_ref[...] += _dot((hid * hid).astype(BF16), w2_ref[0])

    @pl.when(k == pl.num_programs(2) - 1)
    def _():
        res = x1_ref[...] + _mod_rows(m_ref, b, i * tm, tm, ctx_start, 5) * acc_ref[...]
        if final:
            res = _rms(res) * fg_ref[...]
        o_ref[0] = res


def _mix_mlp_call(x, mods, ya, hf, hb, mq, yc, w_out, head_avg, w1, w2, final_g, layer, n_ctx, final):
    bsz, s, d = x.shape
    rows_out = s - n_ctx if final else s
    tm = _pick_tile(rows_out, (544, 512, 256))
    fc = MLP_FF_CHUNK
    row = lambda b, i, k: (b, i, 0)
    lsel = lambda b, i, k: (layer, 0, 0)
    return pl.pallas_call(
        functools.partial(_mix_mlp_body, tm=tm, ctx_start=None if final else s - n_ctx, final=final),
        grid=(bsz, rows_out // tm, D_FF // fc),
        in_specs=[pl.BlockSpec((1, tm, d), row),
                  pl.BlockSpec((1, MOD_ROWS, 6 * d), lsel),
                  pl.BlockSpec((1, tm, MLA_HEADS * MLA_V), row),
                  pl.BlockSpec((1, tm, ML_W), row),
                  pl.BlockSpec((1, tm, ML_W), row),
                  pl.BlockSpec((1, tm, ML_W), lambda b, i, k: (b, i, 3)),
                  pl.BlockSpec((1, tm, LRU_W), row),
                  pl.BlockSpec((1, d, d), lsel),
                  pl.BlockSpec((ML_W, ML_W), lambda b, i, k: (0, 0)),
                  pl.BlockSpec((1, d, fc), lambda b, i, k: (layer, 0, k)),
                  pl.BlockSpec((1, fc, d), lambda b, i, k: (layer, k, 0)),
                  pl.BlockSpec((1, d), lambda b, i, k: (0, 0))],
        out_specs=pl.BlockSpec((1, tm, d), row),
        out_shape=jax.ShapeDtypeStruct((bsz, rows_out, d), F32),
        scratch_shapes=[pltpu.VMEM((tm, d), F32), pltpu.VMEM((tm, d), BF16), pltpu.VMEM((tm, d), F32)],
        compiler_params=_cparams(("parallel", "parallel", "arbitrary"), 48),
        name="mix_mlp",
    )(x, mods, ya, hf, hb, mq, yc, w_out, head_avg, w1, w2, final_g)


def _rope_tables(seq, n_ctx):
    half = MLA_ROPE // 2
    row = jnp.repeat(jnp.arange(seq // GRID_W), GRID_W).astype(F32)
    col = jnp.tile(jnp.arange(GRID_W), seq // GRID_W).astype(F32)
    freqs = 1.0 / (ROPE_BASE ** (jnp.arange(0, half, 2, dtype=F32) / half))
    ang = jnp.concatenate([row[:, None] * freqs, col[:, None] * freqs], axis=-1)
    cos = jnp.concatenate([jnp.cos(ang), jnp.ones((n_ctx, half), F32)], axis=0)
    sin = jnp.concatenate([jnp.sin(ang), jnp.zeros((n_ctx, half), F32)], axis=0)
    return cos, sin


def _block_diag_halves(w):
    depth = w.shape[0]
    per = LANES // LRU_BD
    wh = w.reshape(depth, 2, LRU_BLOCKS // per, per, LRU_BD, LRU_BD)
    eye = jnp.eye(per, dtype=w.dtype)
    bd = jnp.einsum("ldcpio,pq->ldcpiqo", wh, eye).reshape(depth, 2, LRU_BLOCKS // per, LANES, LANES)
    return bd.transpose(0, 2, 1, 3, 4).astype(BF16)


def kernel(x, c, ctx, c_ctx, w_mod, b_mod, w_in, mla_g_q, mla_w_uq, mla_g_kv, mla_w_ukv, ml_gate_bias,
           lru_conv_w, lru_conv_b, lru_w_a, lru_b_a, lru_w_x, lru_b_x, lru_lam, w_out, w_ff1, w_ff2, final_g):
    bsz, seq, d = x.shape
    n_ctx = ctx.shape[1]
    depth = w_in.shape[0]
    assert bsz <= CTX_MOD_ROW and d == D_MODEL and n_ctx % 256 == 0

    w_in16 = w_in.astype(BF16)
    zc = lambda n: jnp.zeros((depth, d, n), BF16)
    ml0, ml1 = MLA_IN, MLA_IN + 4 * ML_W
    w_groups = (jnp.concatenate([w_in16[:, :, :ml0], zc(A_W - MLA_IN)], axis=-1),
                w_in16[:, :, ml0:ml1],
                jnp.concatenate([w_in16[:, :, ml1:ml1 + 4 * ML_HEADS], zc(MG_W - 4 * ML_HEADS)], axis=-1),
                w_in16[:, :, ml1 + 4 * ML_HEADS:])
    w_uq_t = mla_w_uq.astype(BF16).transpose(0, 2, 1)
    ukv = mla_w_ukv.reshape(depth, MLA_KV_RANK, MLA_HEADS, MLA_NOPE + MLA_V)
    w_ukv_k = ukv[..., :MLA_NOPE].reshape(depth, MLA_KV_RANK, -1).astype(BF16)
    w_ukv_vt = ukv[..., MLA_NOPE:].reshape(depth, MLA_KV_RANK, -1).astype(BF16).transpose(0, 2, 1)
    g_q = mla_g_q.reshape(depth, 1, MLA_Q_RANK)
    g_kv = mla_g_kv.reshape(depth, 1, MLA_KV_RANK)
    bias_p = jnp.pad(ml_gate_bias, ((0, 0), (0, MG_W - 4 * ML_HEADS)))
    wa_bd, wx_bd = _block_diag_halves(lru_w_a), _block_diag_halves(lru_w_x)
    vec4 = lambda v: v.reshape(depth, 2, 1, LRU_W)
    conv_b = lru_conv_b.reshape(depth, 1, LRU_W)
    w_out16, w1_16, w2_16 = w_out.astype(BF16), w_ff1.astype(BF16), w_ff2.astype(BF16)
    head_avg = jnp.kron(jnp.eye(ML_HEADS, dtype=F32), jnp.full((ML_DH, ML_DH), 1.0 / ML_DH, F32)).astype(BF16)
    cos, sin = _rope_tables(seq, n_ctx)
    rope = (cos, sin, cos.T, sin.T)
    fg = final_g.reshape(1, d)

    cvec = jnp.concatenate([c, jnp.zeros((CTX_MOD_ROW - bsz, d), c.dtype), c_ctx[None, :],
                            jnp.zeros((MOD_ROWS - CTX_MOD_ROW - 1, d), c.dtype)], axis=0)
    mods = _mods_call(cvec, w_mod, b_mod)

    xs = jnp.concatenate([x, ctx], axis=1)
    for l in range(depth):
        last = l == depth - 1
        a, mq, mg, r = _in_call(xs, mods, w_groups, l, n_ctx)
        qt, k, vt = _prep_call(a, g_q, w_uq_t, g_kv, w_ukv_k, w_ukv_vt, rope, l)
        ya = _attn_call(qt, k, vt, n_ctx, with_ctx=not last)
        hf, hb = _mlstm_call(mq, mg, bias_p[l:l + 1], n_ctx)
        yc = _lru_call(r, lru_conv_w, conv_b, wa_bd, vec4(lru_b_a), wx_bd, vec4(lru_b_x), vec4(lru_lam), l, n_ctx)
        xs = _mix_mlp_call(xs, mods, ya, hf, hb, mq, yc, w_out16, head_avg, w1_16, w2_16, fg, l, n_ctx,
                           final=last)
    return xs
```

```python
import functools

import jax
import jax.numpy as jnp
from jax import lax
from jax.experimental import pallas as pl
from jax.experimental.pallas import tpu as pltpu

F32 = jnp.float32
BF16 = jnp.bfloat16

D_MODEL = 1024
DEPTH = 4
GRID_W = 64
N_CTX = 256
MLA_HEADS = 4
MLA_Q_RANK = 256
MLA_KV_RANK = 128
MLA_NOPE = 128
MLA_ROPE = 64
MLA_V = 128
MLA_QK = MLA_NOPE + MLA_ROPE
MLA_SCALE = MLA_QK ** -0.5
ROPE_BASE = 10000.0
ML_HEADS = 4
ML_DH = 64
ML_W = ML_HEADS * ML_DH
ML_CHUNK = 128
LRU_W = 256
LRU_BLOCKS = 4
LRU_BD = LRU_W // LRU_BLOCKS
CONV_W = 4
CONV_LEFT = 2
LRU_C = 8.0
D_FF = 4 * D_MODEL
EPS = 1e-6
MLA_IN = MLA_Q_RANK + MLA_KV_RANK + MLA_ROPE
ML_IN = 4 * ML_W + 4 * ML_HEADS
LRU_IN = 2 * LRU_W

LANES = 128
SUBLANES = 8
MOD_ROWS = 8
CTX_MOD_ROW = 4

A_W = 512
MQ_W = 4 * ML_W
MG_W = LANES
R_W = 2 * LRU_W


def _cparams(sem, vmem_mb):
    return pltpu.CompilerParams(dimension_semantics=sem, vmem_limit_bytes=vmem_mb * 1024 * 1024)


def _mod_rows(m_ref, b, row0, tm, ctx_start, seg):
    lo, hi = seg * D_MODEL, (seg + 1) * D_MODEL
    lat = m_ref[0, pl.ds(b, 1), lo:hi]
    if ctx_start is None:
        return lat
    ctx = m_ref[0, CTX_MOD_ROW:CTX_MOD_ROW + 1, lo:hi]
    rows = row0 + lax.broadcasted_iota(jnp.int32, (tm, 1), 0)
    return jnp.where(rows >= ctx_start, ctx, lat)


def _pick_tile(n, candidates):
    return next(t for t in candidates if n % t == 0)


def _rms(x):
    return x * lax.rsqrt(jnp.mean(x * x, axis=-1, keepdims=True) + EPS)


def _sigmoid(x):
    return 0.5 * jnp.tanh(0.5 * x) + 0.5


def _dot(a, b):
    return jnp.dot(a, b, preferred_element_type=F32)


def _dot_nt(a, b):
    return lax.dot_general(a, b, (((1,), (1,)), ((), ())), preferred_element_type=F32)


def _mods_body(c_ref, w_ref, b_ref, o_ref):
    cv = c_ref[...]
    act = (cv * jax.nn.sigmoid(cv)).astype(BF16)
    o_ref[0] = _dot(act, w_ref[0].astype(BF16)) + b_ref[0]


def _mods_call(cvec, w_mod, b_mod):
    depth, d, n = w_mod.shape
    tn = 1536
    return pl.pallas_call(
        _mods_body,
        grid=(depth, n // tn),
        in_specs=[pl.BlockSpec((MOD_ROWS, d), lambda l, j: (0, 0)),
                  pl.BlockSpec((1, d, tn), lambda l, j: (l, 0, j)),
                  pl.BlockSpec((1, 1, tn), lambda l, j: (l, 0, j))],
        out_specs=pl.BlockSpec((1, MOD_ROWS, tn), lambda l, j: (l, 0, j)),
        out_shape=jax.ShapeDtypeStruct((depth, MOD_ROWS, n), F32),
        compiler_params=_cparams(("arbitrary", "arbitrary"), 40),
        name="mods",
    )(cvec, w_mod, b_mod.reshape(depth, 1, n))


def _in_body(x_ref, m_ref, wa_ref, wq_ref, wg_ref, wr_ref, a_ref, q_ref, g_ref, r_ref, *, tm, ctx_start):
    b, i = pl.program_id(0), pl.program_id(1)
    xn = _rms(x_ref[0])
    shift = _mod_rows(m_ref, b, i * tm, tm, ctx_start, 0)
    scale = _mod_rows(m_ref, b, i * tm, tm, ctx_start, 1)
    u = (xn * (1.0 + scale) + shift).astype(BF16)
    a_ref[0] = _dot(u, wa_ref[0])
    q_ref[0] = _dot(u, wq_ref[0])
    g_ref[0] = _dot(u, wg_ref[0])
    r_ref[0] = _dot(u, wr_ref[0])


def _in_call(x, mods, w_groups, layer, n_ctx):
    bsz, s, d = x.shape
    tm = _pick_tile(s, (1088, 256))
    row = lambda b, i: (b, i, 0)
    return pl.pallas_call(
        functools.partial(_in_body, tm=tm, ctx_start=s - n_ctx),
        grid=(bsz, s // tm),
        in_specs=[pl.BlockSpec((1, tm, d), row),
                  pl.BlockSpec((1, MOD_ROWS, 6 * d), lambda b, i: (layer, 0, 0)),
                  *[pl.BlockSpec((1, d, w), lambda b, i: (layer, 0, 0)) for w in (A_W, MQ_W, MG_W, R_W)]],
        out_specs=[pl.BlockSpec((1, tm, A_W), row), pl.BlockSpec((1, tm, MQ_W), row),
                   pl.BlockSpec((1, tm, MG_W), row), pl.BlockSpec((1, tm, R_W), row)],
        out_shape=[jax.ShapeDtypeStruct((bsz, s, w), F32) for w in (A_W, MQ_W, MG_W, R_W)],
        compiler_params=_cparams(("parallel", "parallel"), 52),
        name="in_proj",
    )(x, mods, *w_groups)


LOG2E = 1.4426950408889634


def _prep_body(a_ref, gq_ref, gkv_ref, wuqt_ref, wukvk_ref, wukvvt_ref, cos_ref, sin_ref, cost_ref, sint_ref,
               qt_ref, k_ref, vt_ref):
    for bb in range(a_ref.shape[0]):
        _prep_sample(bb, a_ref, gq_ref, gkv_ref, wuqt_ref, wukvk_ref, wukvvt_ref, cos_ref, sin_ref, cost_ref,
                     sint_ref, qt_ref, k_ref, vt_ref)


def _prep_sample(bb, a_ref, gq_ref, gkv_ref, wuqt_ref, wukvk_ref, wukvvt_ref, cos_ref, sin_ref, cost_ref, sint_ref,
                 qt_ref, k_ref, vt_ref):
    a = a_ref[bb]
    nq, nkv = MLA_Q_RANK, MLA_Q_RANK + MLA_KV_RANK
    half = MLA_ROPE // 2
    cq_t = (_rms(a[:, 0:nq]) * gq_ref[0]).T.astype(BF16)
    ckv = _rms(a[:, nq:nkv]) * gkv_ref[0]
    q_t = _dot(wuqt_ref[0], cq_t) * (MLA_SCALE * LOG2E)
    cos_t, sin_t = cost_ref[...], sint_ref[...]
    k_nope = _dot(ckv.astype(BF16), wukvk_ref[0])
    v_t = _dot(wukvvt_ref[0], ckv.T.astype(BF16))
    k1, k2 = a[:, nkv:nkv + half], a[:, nkv + half:nkv + 2 * half]
    c32, s32 = cos_ref[...], sin_ref[...]
    kr1 = (k1 * c32 - k2 * s32).astype(BF16)
    kr2 = (k1 * s32 + k2 * c32).astype(BF16)
    for h in range(MLA_HEADS):
        q_h = q_t[h * MLA_QK:(h + 1) * MLA_QK]
        x1, x2 = q_h[MLA_NOPE:MLA_NOPE + half], q_h[MLA_NOPE + half:MLA_QK]
        qt_ref[bb, h, 0:MLA_NOPE, :] = q_h[0:MLA_NOPE].astype(BF16)
        qt_ref[bb, h, MLA_NOPE:MLA_NOPE + half, :] = (x1 * cos_t - x2 * sin_t).astype(BF16)
        qt_ref[bb, h, MLA_NOPE + half:MLA_QK, :] = (x1 * sin_t + x2 * cos_t).astype(BF16)
        k_ref[bb, h, :, 0:MLA_NOPE] = k_nope[:, h * MLA_NOPE:(h + 1) * MLA_NOPE].astype(BF16)
        k_ref[bb, h, :, MLA_NOPE:MLA_NOPE + half] = kr1
        k_ref[bb, h, :, MLA_NOPE + half:MLA_QK] = kr2
        vt_ref[bb, h] = v_t[h * MLA_V:(h + 1) * MLA_V].astype(BF16)


def _prep_call(a, g_q, w_uq_t, g_kv, w_ukv_k, w_ukv_vt, rope, layer):
    bsz, s, _ = a.shape
    tm = 256
    half = MLA_ROPE // 2
    lsel = lambda i: (layer, 0, 0)
    cos, sin, cos_t, sin_t = rope
    return pl.pallas_call(
        _prep_body,
        grid=(s // tm,),
        in_specs=[pl.BlockSpec((bsz, tm, A_W), lambda i: (0, i, 0)),
                  pl.BlockSpec((1, 1, MLA_Q_RANK), lsel),
                  pl.BlockSpec((1, 1, MLA_KV_RANK), lsel),
                  pl.BlockSpec((1, MLA_HEADS * MLA_QK, MLA_Q_RANK), lsel),
                  pl.BlockSpec((1, MLA_KV_RANK, MLA_HEADS * MLA_NOPE), lsel),
                  pl.BlockSpec((1, MLA_HEADS * MLA_V, MLA_KV_RANK), lsel),
                  pl.BlockSpec((tm, half), lambda i: (i, 0)),
                  pl.BlockSpec((tm, half), lambda i: (i, 0)),
                  pl.BlockSpec((half, tm), lambda i: (0, i)),
                  pl.BlockSpec((half, tm), lambda i: (0, i))],
        out_specs=[pl.BlockSpec((bsz, MLA_HEADS, MLA_QK, tm), lambda i: (0, 0, 0, i)),
                   pl.BlockSpec((bsz, MLA_HEADS, tm, MLA_QK), lambda i: (0, 0, i, 0)),
                   pl.BlockSpec((bsz, MLA_HEADS, MLA_V, tm), lambda i: (0, 0, 0, i))],
        out_shape=[jax.ShapeDtypeStruct((bsz, MLA_HEADS, MLA_QK, s), BF16),
                   jax.ShapeDtypeStruct((bsz, MLA_HEADS, s, MLA_QK), BF16),
                   jax.ShapeDtypeStruct((bsz, MLA_HEADS, MLA_V, s), BF16)],
        compiler_params=_cparams(("parallel",), 40),
        name="mla_prep",
    )(a, g_q, g_kv, w_uq_t, w_ukv_k, w_ukv_vt, cos, sin, cos_t, sin_t)


def _attn_body(qt_ref, k_ref, vt_ref, o_ref, *, subs, zero_rows):
    work = [(q0, tq, c, c is chunks[0], c is chunks[-1]) for q0, tq, chunks in subs for c in chunks]
    score = lambda w: _dot(k_ref[0, 0, w[2][0]:w[2][0] + w[2][1], :],
                           qt_ref[0, 0, :, w[0]:w[0] + w[1]])
    pending = [score(w) for w in work[:ATTN_AHEAD]]
    m = l = acc = None
    for idx, (q0, tq, (start, size), first, last) in enumerate(work):
        if first:
            m = jnp.full((1, tq), -jnp.inf, F32)
            l = jnp.zeros((1, tq), F32)
            acc = jnp.zeros((MLA_V, tq), F32)
        st = pending.pop(0)
        if idx + ATTN_AHEAD < len(work):
            pending.append(score(work[idx + ATTN_AHEAD]))
        m_new = jnp.maximum(m, jnp.max(st, axis=0, keepdims=True))
        p = jnp.exp2(st - m_new)
        alpha = jnp.exp2(m - m_new)
        l = alpha * l + jnp.sum(p, axis=0, keepdims=True)
        acc = alpha * acc + _dot(vt_ref[0, 0, :, start:start + size], p.astype(BF16))
        m = m_new
        if last:
            o_ref[0, q0:q0 + tq, :] = (acc / l).T.astype(o_ref.dtype)
    if zero_rows is not None:
        o_ref[0, zero_rows[0]:zero_rows[1], :] = jnp.zeros((zero_rows[1] - zero_rows[0], MLA_V), o_ref.dtype)


ATTN_TQ = 512
ATTN_TK = 512
ATTN_AHEAD = 3


def _attn_call(qt, k, vt, n_ctx, with_ctx):
    bsz, nh, s, dk = k.shape
    n_lat = s - n_ctx
    assert n_lat % ATTN_TQ == 0 and n_lat % ATTN_TK == 0
    ctx_chunk = (n_lat, n_ctx)
    lat_chunks = (ctx_chunk,) + tuple((j * ATTN_TK, ATTN_TK) for j in range(n_lat // ATTN_TK))
    subs = [(j * ATTN_TQ, ATTN_TQ, lat_chunks) for j in range(n_lat // ATTN_TQ)]
    if with_ctx:
        subs.append((n_lat, n_ctx, (ctx_chunk,)))
    return pl.pallas_call(
        functools.partial(_attn_body, subs=tuple(subs), zero_rows=None if with_ctx else (n_lat, s)),
        grid=(bsz, nh),
        in_specs=[pl.BlockSpec((1, 1, dk, s), lambda b, h: (b, h, 0, 0)),
                  pl.BlockSpec((1, 1, s, dk), lambda b, h: (b, h, 0, 0)),
                  pl.BlockSpec((1, 1, MLA_V, s), lambda b, h: (b, h, 0, 0))],
        out_specs=pl.BlockSpec((1, s, MLA_V), lambda b, h: (b, 0, h)),
        out_shape=jax.ShapeDtypeStruct((bsz, s, nh * MLA_V), BF16),
        compiler_params=_cparams(("parallel", "parallel"), 48),
        name="mla_attn",
    )(qt, k, vt)


ML_BLOCK_CHUNKS = 2


def _mlstm_body(xf_ref, gf_ref, xb_ref, gb_ref, bias_ref, hf_ref, hb_ref, c_ref, m_ref):
    lc = ML_CHUNK
    assert lc == LANES

    @pl.when(pl.program_id(0) == 0)
    def _():
        c_ref[...] = jnp.zeros_like(c_ref)
        m_ref[...] = jnp.zeros_like(m_ref)

    nprob = xf_ref.shape[0] * 2 * ML_HEADS
    s_io = lax.broadcasted_iota(jnp.int32, (lc, lc), 0)
    t_io = lax.broadcasted_iota(jnp.int32, (lc, lc), 1)
    lane = lax.broadcasted_iota(jnp.int32, (lc, LANES), 1)
    row = lax.broadcasted_iota(jnp.int32, (LANES, lc), 0)
    rowp = lax.broadcasted_iota(jnp.int32, (nprob, lc), 0)
    bias = bias_ref[...]
    ones_sq = jnp.ones((lc, lc), BF16)
    nsub = xf_ref.shape[1] // lc
    for t in range(nsub):
        _mlstm_chunk((xf_ref, xb_ref), (gf_ref, gb_ref), (hf_ref, hb_ref), c_ref, m_ref,
                     (t * lc, (nsub - 1 - t) * lc), bias, ones_sq, (s_io, t_io, lane, row, rowp))


def _mlstm_chunk(x_refs, g_refs, o_refs, c_ref, m_ref, row0, bias, ones_sq, iotas):
    lc = ML_CHUNK
    ngate = 4 * ML_HEADS
    s_io, t_io, lane, row, rowp = iotas
    nprob = rowp.shape[0]
    probs = []
    c_rows = jnp.zeros((nprob, lc), F32)
    for bb, d in [(bb, d) for bb in range(x_refs[0].shape[0]) for d in range(2)]:
        x_ref, g_ref = x_refs[d], g_refs[d]
        rows = slice(row0[d], row0[d] + lc)
        mask = (s_io <= t_io) if d == 0 else (s_io >= t_io)
        gt = (g_ref[bb, rows, :] + bias).T[0:ngate]
        lf = jax.nn.log_sigmoid(gt)
        hi = lf.astype(BF16)
        r1 = lf - hi.astype(F32)
        mid = r1.astype(BF16)
        lo = (r1 - mid.astype(F32)).astype(BF16)
        sums = _dot(jnp.concatenate([hi, mid, lo], axis=0),
                    jnp.concatenate([mask.astype(BF16), ones_sq], axis=1))
        sums = sums[0:ngate] + sums[ngate:2 * ngate] + sums[2 * ngate:3 * ngate]
        b_run, b_tot = sums[:, 0:lc], sums[:, lc:2 * lc]
        x = x_ref[bb, rows, :]
        for pair in range(ML_HEADS // 2):
            qs = x[:, pair * LANES:(pair + 1) * LANES] * (ML_DH ** -0.5)
            ks = x[:, ML_W + pair * LANES:ML_W + (pair + 1) * LANES]
            vt = x[:, 2 * ML_W + pair * LANES:2 * ML_W + (pair + 1) * LANES].T
            for odd in range(2):
                h = 2 * pair + odd
                ci, cf = d * 2 * ML_HEADS + h, d * 2 * ML_HEADS + ML_HEADS + h
                own = (lane >= ML_DH) if odd else (lane < ML_DH)
                own_r = (row >= ML_DH) if odd else (row < ML_DH)
                den_row = 0 if odd else ML_DH
                j = (bb * 2 + d) * ML_HEADS + h
                brow, irow = b_run[cf:cf + 1], gt[ci:ci + 1]
                c_rows = jnp.where(rowp == j, brow - irow, c_rows)
                probs.append(dict(
                    j=j, bb=bb, d=d, pair=pair, odd=odd, mask=mask, den_row=den_row, brow=brow, irow=irow,
                    btot=b_tot[cf:cf + 1],
                    qm=jnp.where(own, qs, 0.0).astype(BF16), ks=ks.astype(BF16),
                    km=jnp.where(own, ks, 0.0).astype(BF16),
                    vaug=jnp.where(own_r, vt, jnp.where(row == den_row, 1.0, 0.0))))

    c_cols = jnp.concatenate([c_rows, jnp.zeros((LANES - nprob, lc), F32)], axis=0).T

    for p in probs:
        p["c_old"] = c_ref[p["j"]]
        p["kq"] = _dot_nt(p["ks"], p["qm"])
        p["inter"] = _dot_nt(p["c_old"].astype(BF16), p["qm"])
    for p in probs:
        j = p["j"]
        m_prev = m_ref[j:j + 1, :]
        dt = jnp.where(p["mask"], p["brow"] - c_cols[:, j:j + 1], -jnp.inf)
        inter_m = p["brow"] + m_prev
        m_row = jnp.maximum(inter_m, jnp.max(dt, axis=0, keepdims=True))
        p["st"] = (p["kq"] * jnp.exp(dt - m_row)).astype(BF16)
        p["w_inter"] = jnp.exp(inter_m - m_row)
        p["floor"] = jnp.exp(-m_row)
        grow = p["btot"] - p["brow"] + p["irow"]
        m_new = jnp.maximum(p["btot"] + m_prev, jnp.max(grow, axis=1, keepdims=True))
        p["w_old"] = jnp.exp(p["btot"] + m_prev - m_new)
        p["wv"] = (p["vaug"] * jnp.exp(grow - m_new)).astype(BF16)
        m_ref[j:j + 1, :] = m_new
    outs = {}
    for p in probs:
        ht = _dot(p["vaug"].astype(BF16), p["st"]) + p["w_inter"] * p["inter"]
        den = ht[p["den_row"]:p["den_row"] + 1]
        outs[(p["bb"], p["d"], p["pair"], p["odd"])] = ht / jnp.maximum(jnp.abs(den), p["floor"])
    for bb, d, pair in sorted({(p["bb"], p["d"], p["pair"]) for p in probs}):
        both = jnp.where(row < ML_DH, outs[(bb, d, pair, 0)], outs[(bb, d, pair, 1)])
        o_refs[d][bb, row0[d]:row0[d] + lc, pair * LANES:(pair + 1) * LANES] = both.T
    for p in probs:
        c_ref[p["j"]] = p["w_old"] * p["c_old"] + _dot(p["wv"], p["km"])


def _mlstm_call(mq, mg, bias_p, n_ctx):
    bsz, s, _ = mq.shape
    rows = ML_BLOCK_CHUNKS * ML_CHUNK
    assert n_ctx % rows == 0 and s % rows == 0
    nch, ncc = s // rows, n_ctx // rows
    nlc = nch - ncc
    fwd = lambda j: (0, jnp.where(j < ncc, nlc + j, j - ncc), 0)
    bwd = lambda j: (0, nch - 1 - j, 0)
    out = jax.ShapeDtypeStruct((bsz, s, ML_W), F32)
    nstate = bsz * 2 * ML_HEADS
    assert nstate <= LANES
    return pl.pallas_call(
        _mlstm_body,
        grid=(nch,),
        in_specs=[pl.BlockSpec((bsz, rows, MQ_W), fwd), pl.BlockSpec((bsz, rows, MG_W), fwd),
                  pl.BlockSpec((bsz, rows, MQ_W), bwd), pl.BlockSpec((bsz, rows, MG_W), bwd),
                  pl.BlockSpec((1, MG_W), lambda j: (0, 0))],
        out_specs=[pl.BlockSpec((bsz, rows, ML_W), fwd), pl.BlockSpec((bsz, rows, ML_W), bwd)],
        out_shape=[out, out],
        scratch_shapes=[pltpu.VMEM((nstate, LANES, LANES), F32),
                        pltpu.VMEM((nstate, LANES), F32)],
        compiler_params=_cparams(("arbitrary",), 48),
        name="mlstm",
    )(mq, mg, mq, mg, bias_p)


LRU_PITCH_PAD = 8
LRU_UNROLL = 8


def _lru_body(xb_ref, gb_ref, cw_ref, cb_ref, wa_ref, ba_ref, wx_ref, bx_ref, lam_ref, y_ref,
              xp_ref, xs_ref, a_ref, u_ref, *, n_ctx, s_len):
    n_lat = s_len - n_ctx
    pad = SUBLANES
    lat_off = n_ctx + 2 * pad
    zeros = jnp.zeros((pad, LANES), F32)
    xp_ref[0:pad, :] = zeros
    xp_ref[pad + n_ctx:lat_off, :] = zeros
    xp_ref[lat_off + n_lat:lat_off + n_lat + pad, :] = zeros
    xp_ref[pad:pad + n_ctx, :] = xb_ref[0, n_lat:n_lat + n_ctx, :]
    cchunk = 512

    def copy_body(c, _):
        src = pl.multiple_of(c * cchunk, SUBLANES)
        dst = pl.multiple_of(lat_off + c * cchunk, SUBLANES)
        xp_ref[pl.ds(dst, cchunk), :] = xb_ref[0, pl.ds(src, cchunk), :]
        return 0

    lax.fori_loop(0, n_lat // cchunk, copy_body, 0)

    cw = cw_ref[...]
    cb = cb_ref[...]

    def conv(src0, dst0, n):
        acc = cb + xp_ref[src0 - CONV_LEFT:src0 - CONV_LEFT + n, :] * cw[0:1, :]
        for j in range(1, CONV_W):
            acc = acc + xp_ref[src0 - CONV_LEFT + j:src0 - CONV_LEFT + j + n, :] * cw[j:j + 1, :]
        xs_ref[dst0:dst0 + n, :] = acc

    conv(pad, n_lat, n_ctx)
    for c in range(n_lat // cchunk):
        conv(lat_off + c * cchunk, c * cchunk, cchunk)

    seg_lat = n_lat // SUBLANES
    seg_ctx = n_ctx // SUBLANES
    p_lat, p_ctx = seg_lat + LRU_PITCH_PAD, seg_ctx + LRU_PITCH_PAD
    ctx_base = SUBLANES * p_lat
    row_io = lax.broadcasted_iota(jnp.int32, (SUBLANES, LANES), 0)

    def gates(x, d):
        xb16 = x.astype(BF16)
        c1 = (-0.5 * LRU_C) * jax.nn.softplus(-lam_ref[d])
        log_a = c1 * jnp.tanh(0.5 * (_dot(xb16, wa_ref[d]) + ba_ref[d])) + c1
        i = _sigmoid(_dot(xb16, wx_ref[d]) + bx_ref[d])
        a = jnp.exp(log_a)
        return a, jnp.sqrt(jnp.tanh(log_a) * (-1.0 - a * a)) * (i * x)

    def fill(d):
        def lat_body(r, _):
            src = pl.multiple_of(r * seg_lat, SUBLANES)
            dst = pl.multiple_of(r * p_lat, SUBLANES)
            a, u = gates(xs_ref[pl.ds(src, seg_lat), :], d)
            a_ref[d, pl.ds(dst, seg_lat), :] = a
            u_ref[d, pl.ds(dst, seg_lat), :] = u
            return 0

        lax.fori_loop(0, SUBLANES, lat_body, 0)
        a, u = gates(xs_ref[n_lat:n_lat + n_ctx, :], d)
        for r in range(SUBLANES):
            a_ref[d, ctx_base + r * p_ctx:ctx_base + r * p_ctx + seg_ctx, :] = a[r * seg_ctx:(r + 1) * seg_ctx, :]
            u_ref[d, ctx_base + r * p_ctx:ctx_base + r * p_ctx + seg_ctx, :] = u[r * seg_ctx:(r + 1) * seg_ctx, :]

    def scan(base, n, pitch, h0s):
        def block(tb, carry):
            idx = [[pl.ds(base + (tb * LRU_UNROLL + k if d == 0 else n - 1 - tb * LRU_UNROLL - k),
                          SUBLANES, stride=pitch) for k in range(LRU_UNROLL)] for d in range(2)]
            av = [[a_ref[d, i, :] for i in idx[d]] for d in range(2)]
            uv = [[u_ref[d, i, :] for i in idx[d]] for d in range(2)]
            carry = list(carry)
            for k in range(LRU_UNROLL):
                for d in range(2):
                    h, acum = carry[d]
                    h = av[d][k] * h + uv[d][k]
                    acum = acum * av[d][k]
                    carry[d] = (h, acum)
                    uv[d][k], av[d][k] = h, acum
            for d in range(2):
                for k in range(LRU_UNROLL):
                    u_ref[d, idx[d][k], :] = uv[d][k]
                    a_ref[d, idx[d][k], :] = av[d][k]
            return tuple(carry)

        init = (jnp.zeros((SUBLANES, LANES), F32), jnp.ones((SUBLANES, LANES), F32))
        ends = lax.fori_loop(0, n // LRU_UNROLL, block, (init, init))
        result = []
        for d in range(2):
            h_end, a_end = ends[d]
            carry = h0s[d]
            cvec = jnp.zeros((SUBLANES, LANES), F32)
            for r in (range(SUBLANES) if d == 0 else range(SUBLANES - 1, -1, -1)):
                cvec = jnp.where(row_io == r, carry, cvec)
                carry = h_end[r:r + 1, :] + a_end[r:r + 1, :] * carry
            result.append((cvec, carry))
        return result

    fill(0)
    fill(1)
    zero_state = jnp.zeros((1, LANES), F32)
    ctx_res = scan(ctx_base, seg_ctx, p_ctx, (zero_state, zero_state))
    lat_res = scan(0, seg_lat, p_lat, (ctx_res[0][1], ctx_res[1][1]))
    carries = {(d, "ctx"): ctx_res[d][0] for d in range(2)}
    carries.update({(d, "lat"): lat_res[d][0] for d in range(2)})

    def emit(kind, r, dst0, src0, n):
        hsum = None
        for d in range(2):
            c_in = carries[d, kind][r:r + 1, :]
            part = u_ref[d, src0:src0 + n, :] + a_ref[d, src0:src0 + n, :] * c_in
            hsum = part if hsum is None else hsum + part
        y_ref[0, dst0:dst0 + n, :] = jax.nn.gelu(gb_ref[0, dst0:dst0 + n, :]) * hsum

    for r in range(SUBLANES):
        emit("lat", r, r * seg_lat, r * p_lat, seg_lat)
        emit("ctx", r, n_lat + r * seg_ctx, ctx_base + r * p_ctx, seg_ctx)


def _lru_call(r, conv_w, conv_b, wa_bd, b_a, wx_bd, b_x, lam, layer, n_ctx):
    bsz, s, _ = r.shape
    nh = LRU_W // LANES
    n_lat = s - n_ctx
    scan_rows = SUBLANES * (n_lat // SUBLANES + LRU_PITCH_PAD) + SUBLANES * (n_ctx // SUBLANES + LRU_PITCH_PAD)
    vec = lambda b, c: (layer, 0, 0, c)
    return pl.pallas_call(
        functools.partial(_lru_body, n_ctx=n_ctx, s_len=s),
        grid=(bsz, nh),
        in_specs=[pl.BlockSpec((1, s, LANES), lambda b, c: (b, 0, c)),
                  pl.BlockSpec((1, s, LANES), lambda b, c: (b, 0, nh + c)),
                  pl.BlockSpec((None, CONV_W, LANES), lambda b, c: (layer, 0, c)),
                  pl.BlockSpec((None, 1, LANES), lambda b, c: (layer, 0, c)),
                  pl.BlockSpec((None, None, 2, LANES, LANES), lambda b, c: (layer, c, 0, 0, 0)),
                  pl.BlockSpec((None, 2, 1, LANES), vec),
                  pl.BlockSpec((None, None, 2, LANES, LANES), lambda b, c: (layer, c, 0, 0, 0)),
                  pl.BlockSpec((None, 2, 1, LANES), vec),
                  pl.BlockSpec((None, 2, 1, LANES), vec)],
        out_specs=pl.BlockSpec((1, s, LANES), lambda b, c: (b, 0, c)),
        out_shape=jax.ShapeDtypeStruct((bsz, s, LRU_W), F32),
        scratch_shapes=[pltpu.VMEM((s + 3 * SUBLANES, LANES), F32),
                        pltpu.VMEM((s, LANES), F32),
                        pltpu.VMEM((2, scan_rows, LANES), F32),
                        pltpu.VMEM((2, scan_rows, LANES), F32)],
        compiler_params=_cparams(("parallel", "parallel"), 48),
        name="rglru",
    )(r, r, conv_w, conv_b, wa_bd, b_a, wx_bd, b_x, lam)


MLP_FF_CHUNK = 2048


def _mix_mlp_body(x_ref, m_ref, ya_ref, hf_ref, hb_ref, og_ref, yc_ref, wo_ref, hn_ref, w1_ref, w2_ref, fg_ref,
                  o_ref, x1_ref, u_ref, acc_ref, *, tm, ctx_start, final):
    b, i, k = pl.program_id(0), pl.program_id(1), pl.program_id(2)

    @pl.when(k == 0)
    def _():
        hsum = hf_ref[0] + hb_ref[0]
        sq = hsum * hsum
        hi = sq.astype(BF16)
        lo = (sq - hi.astype(F32)).astype(BF16)
        msq = _dot(hi, hn_ref[...]) + _dot(lo, hn_ref[...])
        yb = (_sigmoid(og_ref[0]) * (hsum * lax.rsqrt(msq + EPS))).astype(BF16)
        na, nb = MLA_HEADS * MLA_V, MLA_HEADS * MLA_V + ML_W
        y = (_dot(ya_ref[0], wo_ref[0, 0:na, :]) + _dot(yb, wo_ref[0, na:nb, :])
             + _dot(yc_ref[0].astype(BF16), wo_ref[0, nb:, :]))
        x1 = x_ref[0] + _mod_rows(m_ref, b, i * tm, tm, ctx_start, 2) * y
        x1_ref[...] = x1
        shift = _mod_rows(m_ref, b, i * tm, tm, ctx_start, 3)
        scale = _mod_rows(m_ref, b, i * tm, tm, ctx_start, 4)
        u_ref[...] = (_rms(x1) * (1.0 + scale) + shift).astype(BF16)
        acc_ref[...] = jnp.zeros_like(acc_ref)

    hid = jnp.maximum(_dot(u_ref[...], w1_ref[0]), 0.0)
    acc_ref[...] += _dot((hid * hid).astype(BF16), w2_ref[0])

    @pl.when(k == pl.num_programs(2) - 1)
    def _():
        res = x1_ref[...] + _mod_rows(m_ref, b, i * tm, tm, ctx_start, 5) * acc_ref[...]
        if final:
            res = _rms(res) * fg_ref[...]
        o_ref[0] = res


def _mix_mlp_call(x, mods, ya, hf, hb, mq, yc, w_out, head_avg, w1, w2, final_g, layer, n_ctx, final):
    bsz, s, d = x.shape
    rows_out = s - n_ctx if final else s
    tm = _pick_tile(rows_out, (544, 512, 256))
    fc = MLP_FF_CHUNK
    row = lambda b, i, k: (b, i, 0)
    lsel = lambda b, i, k: (layer, 0, 0)
    return pl.pallas_call(
        functools.partial(_mix_mlp_body, tm=tm, ctx_start=None if final else s - n_ctx, final=final),
        grid=(bsz, rows_out // tm, D_FF // fc),
        in_specs=[pl.BlockSpec((1, tm, d), row),
                  pl.BlockSpec((1, MOD_ROWS, 6 * d), lsel),
                  pl.BlockSpec((1, tm, MLA_HEADS * MLA_V), row),
                  pl.BlockSpec((1, tm, ML_W), row),
                  pl.BlockSpec((1, tm, ML_W), row),
                  pl.BlockSpec((1, tm, ML_W), lambda b, i, k: (b, i, 3)),
                  pl.BlockSpec((1, tm, LRU_W), row),
                  pl.BlockSpec((1, d, d), lsel),
                  pl.BlockSpec((ML_W, ML_W), lambda b, i, k: (0, 0)),
                  pl.BlockSpec((1, d, fc), lambda b, i, k: (layer, 0, k)),
                  pl.BlockSpec((1, fc, d), lambda b, i, k: (layer, k, 0)),
                  pl.BlockSpec((1, d), lambda b, i, k: (0, 0))],
        out_specs=pl.BlockSpec((1, tm, d), row),
        out_shape=jax.ShapeDtypeStruct((bsz, rows_out, d), F32),
        scratch_shapes=[pltpu.VMEM((tm, d), F32), pltpu.VMEM((tm, d), BF16), pltpu.VMEM((tm, d), F32)],
        compiler_params=_cparams(("parallel", "parallel", "arbitrary"), 48),
        name="mix_mlp",
    )(x, mods, ya, hf, hb, mq, yc, w_out, head_avg, w1, w2, final_g)


def _rope_tables(seq, n_ctx):
    half = MLA_ROPE // 2
    row = jnp.repeat(jnp.arange(seq // GRID_W), GRID_W).astype(F32)
    col = jnp.tile(jnp.arange(GRID_W), seq // GRID_W).astype(F32)
    freqs = 1.0 / (ROPE_BASE ** (jnp.arange(0, half, 2, dtype=F32) / half))
    ang = jnp.concatenate([row[:, None] * freqs, col[:, None] * freqs], axis=-1)
    cos = jnp.concatenate([jnp.cos(ang), jnp.ones((n_ctx, half), F32)], axis=0)
    sin = jnp.concatenate([jnp.sin(ang), jnp.zeros((n_ctx, half), F32)], axis=0)
    return cos, sin


def _block_diag_halves(w):
    depth = w.shape[0]
    per = LANES // LRU_BD
    wh = w.reshape(depth, 2, LRU_BLOCKS // per, per, LRU_BD, LRU_BD)
    eye = jnp.eye(per, dtype=w.dtype)
    bd = jnp.einsum("ldcpio,pq->ldcpiqo", wh, eye).reshape(depth, 2, LRU_BLOCKS // per, LANES, LANES)
    return bd.transpose(0, 2, 1, 3, 4).astype(BF16)


def kernel(x, c, ctx, c_ctx, w_mod, b_mod, w_in, mla_g_q, mla_w_uq, mla_g_kv, mla_w_ukv, ml_gate_bias,
           lru_conv_w, lru_conv_b, lru_w_a, lru_b_a, lru_w_x, lru_b_x, lru_lam, w_out, w_ff1, w_ff2, final_g):
    bsz, seq, d = x.shape
    n_ctx = ctx.shape[1]
    depth = w_in.shape[0]
    assert bsz <= CTX_MOD_ROW and d == D_MODEL and n_ctx % 256 == 0

    w_in16 = w_in.astype(BF16)
    zc = lambda n: jnp.zeros((depth, d, n), BF16)
    ml0, ml1 = MLA_IN, MLA_IN + 4 * ML_W
    w_groups = (jnp.concatenate([w_in16[:, :, :ml0], zc(A_W - MLA_IN)], axis=-1),
                w_in16[:, :, ml0:ml1],
                jnp.concatenate([w_in16[:, :, ml1:ml1 + 4 * ML_HEADS], zc(MG_W - 4 * ML_HEADS)], axis=-1),
                w_in16[:, :, ml1 + 4 * ML_HEADS:])
    w_uq_t = mla_w_uq.astype(BF16).transpose(0, 2, 1)
    ukv = mla_w_ukv.reshape(depth, MLA_KV_RANK, MLA_HEADS, MLA_NOPE + MLA_V)
    w_ukv_k = ukv[..., :MLA_NOPE].reshape(depth, MLA_KV_RANK, -1).astype(BF16)
    w_ukv_vt = ukv[..., MLA_NOPE:].reshape(depth, MLA_KV_RANK, -1).astype(BF16).transpose(0, 2, 1)
    g_q = mla_g_q.reshape(depth, 1, MLA_Q_RANK)
    g_kv = mla_g_kv.reshape(depth, 1, MLA_KV_RANK)
    bias_p = jnp.pad(ml_gate_bias, ((0, 0), (0, MG_W - 4 * ML_HEADS)))
    wa_bd, wx_bd = _block_diag_halves(lru_w_a), _block_diag_halves(lru_w_x)
    vec4 = lambda v: v.reshape(depth, 2, 1, LRU_W)
    conv_b = lru_conv_b.reshape(depth, 1, LRU_W)
    w_out16, w1_16, w2_16 = w_out.astype(BF16), w_ff1.astype(BF16), w_ff2.astype(BF16)
    head_avg = jnp.kron(jnp.eye(ML_HEADS, dtype=F32), jnp.full((ML_DH, ML_DH), 1.0 / ML_DH, F32)).astype(BF16)
    cos, sin = _rope_tables(seq, n_ctx)
    rope = (cos, sin, cos.T, sin.T)
    fg = final_g.reshape(1, d)

    cvec = jnp.concatenate([c, jnp.zeros((CTX_MOD_ROW - bsz, d), c.dtype), c_ctx[None, :],
                            jnp.zeros((MOD_ROWS - CTX_MOD_ROW - 1, d), c.dtype)], axis=0)
    mods = _mods_call(cvec, w_mod, b_mod)

    xs = jnp.concatenate([x, ctx], axis=1)
    for l in range(depth):
        last = l == depth - 1
        a, mq, mg, r = _in_call(xs, mods, w_groups, l, n_ctx)
        qt, k, vt = _prep_call(a, g_q, w_uq_t, g_kv, w_ukv_k, w_ukv_vt, rope, l)
        ya = _attn_call(qt, k, vt, n_ctx, with_ctx=not last)
        hf, hb = _mlstm_call(mq, mg, bias_p[l:l + 1], n_ctx)
        yc = _lru_call(r, lru_conv_w, conv_b, wa_bd, vec4(lru_b_a), wx_bd, vec4(lru_b_x), vec4(lru_lam), l, n_ctx)
        xs = _mix_mlp_call(xs, mods, ya, hf, hb, mq, yc, w_out16, head_avg, w1_16, w2_16, fg, l, n_ctx,
                           final=last)
    return xs
```

```python
import functools

import jax
import jax.numpy as jnp
from jax import lax
from jax.experimental import pallas as pl
from jax.experimental.pallas import tpu as pltpu

F32 = jnp.float32
BF16 = jnp.bfloat16

D_MODEL = 1024
DEPTH = 4
GRID_W = 64
N_CTX = 256
MLA_HEADS = 4
MLA_Q_RANK = 256
MLA_KV_RANK = 128
MLA_NOPE = 128
MLA_ROPE = 64
MLA_V = 128
MLA_QK = MLA_NOPE + MLA_ROPE
MLA_SCALE = MLA_QK ** -0.5
ROPE_BASE = 10000.0
ML_HEADS = 4
ML_DH = 64
ML_W = ML_HEADS * ML_DH
ML_CHUNK = 128
LRU_W = 256
LRU_BLOCKS = 4
LRU_BD = LRU_W // LRU_BLOCKS
CONV_W = 4
CONV_LEFT = 2
LRU_C = 8.0
D_FF = 4 * D_MODEL
EPS = 1e-6
MLA_IN = MLA_Q_RANK + MLA_KV_RANK + MLA_ROPE
ML_IN = 4 * ML_W + 4 * ML_HEADS
LRU_IN = 2 * LRU_W

LANES = 128
SUBLANES = 8
MOD_ROWS = 8
CTX_MOD_ROW = 4

A_W = 512
MQ_W = 4 * ML_W
MG_W = LANES
R_W = 2 * LRU_W


def _cparams(sem, vmem_mb):
    return pltpu.CompilerParams(dimension_semantics=sem, vmem_limit_bytes=vmem_mb * 1024 * 1024)


def _mod_rows(m_ref, b, row0, tm, ctx_start, seg):
    lo, hi = seg * D_MODEL, (seg + 1) * D_MODEL
    lat = m_ref[0, pl.ds(b, 1), lo:hi]
    if ctx_start is None:
        return lat
    ctx = m_ref[0, CTX_MOD_ROW:CTX_MOD_ROW + 1, lo:hi]
    rows = row0 + lax.broadcasted_iota(jnp.int32, (tm, 1), 0)
    return jnp.where(rows >= ctx_start, ctx, lat)


def _pick_tile(n, candidates):
    return next(t for t in candidates if n % t == 0)


def _rms(x):
    return x * lax.rsqrt(jnp.mean(x * x, axis=-1, keepdims=True) + EPS)


def _sigmoid(x):
    return 0.5 * jnp.tanh(0.5 * x) + 0.5


def _dot(a, b):
    return jnp.dot(a, b, preferred_element_type=F32)


def _dot_nt(a, b):
    return lax.dot_general(a, b, (((1,), (1,)), ((), ())), preferred_element_type=F32)


def _mods_body(c_ref, w_ref, b_ref, o_ref):
    cv = c_ref[...]
    act = (cv * jax.nn.sigmoid(cv)).astype(BF16)
    o_ref[0] = _dot(act, w_ref[0].astype(BF16)) + b_ref[0]


def _mods_call(cvec, w_mod, b_mod):
    depth, d, n = w_mod.shape
    tn = 1536
    return pl.pallas_call(
        _mods_body,
        grid=(depth, n // tn),
        in_specs=[pl.BlockSpec((MOD_ROWS, d), lambda l, j: (0, 0)),
                  pl.BlockSpec((1, d, tn), lambda l, j: (l, 0, j)),
                  pl.BlockSpec((1, 1, tn), lambda l, j: (l, 0, j))],
        out_specs=pl.BlockSpec((1, MOD_ROWS, tn), lambda l, j: (l, 0, j)),
        out_shape=jax.ShapeDtypeStruct((depth, MOD_ROWS, n), F32),
        compiler_params=_cparams(("arbitrary", "arbitrary"), 40),
        name="mods",
    )(cvec, w_mod, b_mod.reshape(depth, 1, n))


def _in_body(x_ref, m_ref, wa_ref, wq_ref, wg_ref, wr_ref, a_ref, q_ref, g_ref, r_ref, *, tm, ctx_start):
    b, i = pl.program_id(0), pl.program_id(1)
    xn = _rms(x_ref[0])
    shift = _mod_rows(m_ref, b, i * tm, tm, ctx_start, 0)
    scale = _mod_rows(m_ref, b, i * tm, tm, ctx_start, 1)
    u = (xn * (1.0 + scale) + shift).astype(BF16)
    a_ref[0] = _dot(u, wa_ref[0])
    q_ref[0] = _dot(u, wq_ref[0])
    g_ref[0] = _dot(u, wg_ref[0])
    r_ref[0] = _dot(u, wr_ref[0])


def _in_call(x, mods, w_groups, layer, n_ctx):
    bsz, s, d = x.shape
    tm = _pick_tile(s, (1088, 256))
    row = lambda b, i: (b, i, 0)
    return pl.pallas_call(
        functools.partial(_in_body, tm=tm, ctx_start=s - n_ctx),
        grid=(bsz, s // tm),
        in_specs=[pl.BlockSpec((1, tm, d), row),
                  pl.BlockSpec((1, MOD_ROWS, 6 * d), lambda b, i: (layer, 0, 0)),
                  *[pl.BlockSpec((1, d, w), lambda b, i: (layer, 0, 0)) for w in (A_W, MQ_W, MG_W, R_W)]],
        out_specs=[pl.BlockSpec((1, tm, A_W), row), pl.BlockSpec((1, tm, MQ_W), row),
                   pl.BlockSpec((1, tm, MG_W), row), pl.BlockSpec((1, tm, R_W), row)],
        out_shape=[jax.ShapeDtypeStruct((bsz, s, w), F32) for w in (A_W, MQ_W, MG_W, R_W)],
        compiler_params=_cparams(("parallel", "parallel"), 52),
        name="in_proj",
    )(x, mods, *w_groups)


LOG2E = 1.4426950408889634


def _prep_body(a_ref, gq_ref, gkv_ref, wuqt_ref, wukvk_ref, wukvvt_ref, cos_ref, sin_ref, cost_ref, sint_ref,
               qt_ref, k_ref, vt_ref):
    for bb in range(a_ref.shape[0]):
        _prep_sample(bb, a_ref, gq_ref, gkv_ref, wuqt_ref, wukvk_ref, wukvvt_ref, cos_ref, sin_ref, cost_ref,
                     sint_ref, qt_ref, k_ref, vt_ref)


def _prep_sample(bb, a_ref, gq_ref, gkv_ref, wuqt_ref, wukvk_ref, wukvvt_ref, cos_ref, sin_ref, cost_ref, sint_ref,
                 qt_ref, k_ref, vt_ref):
    a = a_ref[bb]
    nq, nkv = MLA_Q_RANK, MLA_Q_RANK + MLA_KV_RANK
    half = MLA_ROPE // 2
    cq_t = (_rms(a[:, 0:nq]) * gq_ref[0]).T.astype(BF16)
    ckv = _rms(a[:, nq:nkv]) * gkv_ref[0]
    q_t = _dot(wuqt_ref[0], cq_t) * (MLA_SCALE * LOG2E)
    cos_t, sin_t = cost_ref[...], sint_ref[...]
    k_nope = _dot(ckv.astype(BF16), wukvk_ref[0])
    v_t = _dot(wukvvt_ref[0], ckv.T.astype(BF16))
    k1, k2 = a[:, nkv:nkv + half], a[:, nkv + half:nkv + 2 * half]
    c32, s32 = cos_ref[...], sin_ref[...]
    kr1 = (k1 * c32 - k2 * s32).astype(BF16)
    kr2 = (k1 * s32 + k2 * c32).astype(BF16)
    for h in range(MLA_HEADS):
        q_h = q_t[h * MLA_QK:(h + 1) * MLA_QK]
        x1, x2 = q_h[MLA_NOPE:MLA_NOPE + half], q_h[MLA_NOPE + half:MLA_QK]
        qt_ref[bb, h, 0:MLA_NOPE, :] = q_h[0:MLA_NOPE].astype(BF16)
        qt_ref[bb, h, MLA_NOPE:MLA_NOPE + half, :] = (x1 * cos_t - x2 * sin_t).astype(BF16)
        qt_ref[bb, h, MLA_NOPE + half:MLA_QK, :] = (x1 * sin_t + x2 * cos_t).astype(BF16)
        k_ref[bb, h, :, 0:MLA_NOPE] = k_nope[:, h * MLA_NOPE:(h + 1) * MLA_NOPE].astype(BF16)
        k_ref[bb, h, :, MLA_NOPE:MLA_NOPE + half] = kr1
        k_ref[bb, h, :, MLA_NOPE + half:MLA_QK] = kr2
        vt_ref[bb, h] = v_t[h * MLA_V:(h + 1) * MLA_V].astype(BF16)


def _prep_call(a, g_q, w_uq_t, g_kv, w_ukv_k, w_ukv_vt, rope, layer):
    bsz, s, _ = a.shape
    tm = 256
    half = MLA_ROPE // 2
    lsel = lambda i: (layer, 0, 0)
    cos, sin, cos_t, sin_t = rope
    return pl.pallas_call(
        _prep_body,
        grid=(s // tm,),
        in_specs=[pl.BlockSpec((bsz, tm, A_W), lambda i: (0, i, 0)),
                  pl.BlockSpec((1, 1, MLA_Q_RANK), lsel),
                  pl.BlockSpec((1, 1, MLA_KV_RANK), lsel),
                  pl.BlockSpec((1, MLA_HEADS * MLA_QK, MLA_Q_RANK), lsel),
                  pl.BlockSpec((1, MLA_KV_RANK, MLA_HEADS * MLA_NOPE), lsel),
                  pl.BlockSpec((1, MLA_HEADS * MLA_V, MLA_KV_RANK), lsel),
                  pl.BlockSpec((tm, half), lambda i: (i, 0)),
                  pl.BlockSpec((tm, half), lambda i: (i, 0)),
                  pl.BlockSpec((half, tm), lambda i: (0, i)),
                  pl.BlockSpec((half, tm), lambda i: (0, i))],
        out_specs=[pl.BlockSpec((bsz, MLA_HEADS, MLA_QK, tm), lambda i: (0, 0, 0, i)),
                   pl.BlockSpec((bsz, MLA_HEADS, tm, MLA_QK), lambda i: (0, 0, i, 0)),
                   pl.BlockSpec((bsz, MLA_HEADS, MLA_V, tm), lambda i: (0, 0, 0, i))],
        out_shape=[jax.ShapeDtypeStruct((bsz, MLA_HEADS, MLA_QK, s), BF16),
                   jax.ShapeDtypeStruct((bsz, MLA_HEADS, s, MLA_QK), BF16),
                   jax.ShapeDtypeStruct((bsz, MLA_HEADS, MLA_V, s), BF16)],
        compiler_params=_cparams(("parallel",), 40),
        name="mla_prep",
    )(a, g_q, g_kv, w_uq_t, w_ukv_k, w_ukv_vt, cos, sin, cos_t, sin_t)


def _attn_body(qt_ref, k_ref, vt_ref, o_ref, *, subs, zero_rows):
    work = [(q0, tq, c, c is chunks[0], c is chunks[-1]) for q0, tq, chunks in subs for c in chunks]
    score = lambda w: _dot(k_ref[0, 0, w[2][0]:w[2][0] + w[2][1], :],
                           qt_ref[0, 0, :, w[0]:w[0] + w[1]])
    pending = [score(w) for w in work[:ATTN_AHEAD]]
    m = l = acc = None
    for idx, (q0, tq, (start, size), first, last) in enumerate(work):
        if first:
            m = jnp.full((1, tq), -jnp.inf, F32)
            l = jnp.zeros((1, tq), F32)
            acc = jnp.zeros((MLA_V, tq), F32)
        st = pending.pop(0)
        if idx + ATTN_AHEAD < len(work):
            pending.append(score(work[idx + ATTN_AHEAD]))
        m_new = jnp.maximum(m, jnp.max(st, axis=0, keepdims=True))
        p = jnp.exp2(st - m_new)
        alpha = jnp.exp2(m - m_new)
        l = alpha * l + jnp.sum(p, axis=0, keepdims=True)
        acc = alpha * acc + _dot(vt_ref[0, 0, :, start:start + size], p.astype(BF16))
        m = m_new
        if last:
            o_ref[0, q0:q0 + tq, :] = (acc / l).T.astype(o_ref.dtype)
    if zero_rows is not None:
        o_ref[0, zero_rows[0]:zero_rows[1], :] = jnp.zeros((zero_rows[1] - zero_rows[0], MLA_V), o_ref.dtype)


ATTN_TQ = 512
ATTN_TK = 512
ATTN_AHEAD = 4


def _attn_call(qt, k, vt, n_ctx, with_ctx):
    bsz, nh, s, dk = k.shape
    n_lat = s - n_ctx
    assert n_lat % ATTN_TQ == 0 and n_lat % ATTN_TK == 0
    ctx_chunk = (n_lat, n_ctx)
    lat_chunks = (ctx_chunk,) + tuple((j * ATTN_TK, ATTN_TK) for j in range(n_lat // ATTN_TK))
    subs = [(j * ATTN_TQ, ATTN_TQ, lat_chunks) for j in range(n_lat // ATTN_TQ)]
    if with_ctx:
        subs.append((n_lat, n_ctx, (ctx_chunk,)))
    return pl.pallas_call(
        functools.partial(_attn_body, subs=tuple(subs), zero_rows=None if with_ctx else (n_lat, s)),
        grid=(bsz, nh),
        in_specs=[pl.BlockSpec((1, 1, dk, s), lambda b, h: (b, h, 0, 0)),
                  pl.BlockSpec((1, 1, s, dk), lambda b, h: (b, h, 0, 0)),
                  pl.BlockSpec((1, 1, MLA_V, s), lambda b, h: (b, h, 0, 0))],
        out_specs=pl.BlockSpec((1, s, MLA_V), lambda b, h: (b, 0, h)),
        out_shape=jax.ShapeDtypeStruct((bsz, s, nh * MLA_V), BF16),
        compiler_params=_cparams(("parallel", "parallel"), 48),
        name="mla_attn",
    )(qt, k, vt)


ML_BLOCK_CHUNKS = 2


def _mlstm_body(xf_ref, gf_ref, xb_ref, gb_ref, bias_ref, hf_ref, hb_ref, c_ref, m_ref):
    lc = ML_CHUNK
    assert lc == LANES

    @pl.when(pl.program_id(0) == 0)
    def _():
        c_ref[...] = jnp.zeros_like(c_ref)
        m_ref[...] = jnp.zeros_like(m_ref)

    nprob = xf_ref.shape[0] * 2 * ML_HEADS
    s_io = lax.broadcasted_iota(jnp.int32, (lc, lc), 0)
    t_io = lax.broadcasted_iota(jnp.int32, (lc, lc), 1)
    lane = lax.broadcasted_iota(jnp.int32, (lc, LANES), 1)
    row = lax.broadcasted_iota(jnp.int32, (LANES, lc), 0)
    rowp = lax.broadcasted_iota(jnp.int32, (nprob, lc), 0)
    bias = bias_ref[...]
    ones_sq = jnp.ones((lc, lc), BF16)
    nsub = xf_ref.shape[1] // lc
    for t in range(nsub):
        _mlstm_chunk((xf_ref, xb_ref), (gf_ref, gb_ref), (hf_ref, hb_ref), c_ref, m_ref,
                     (t * lc, (nsub - 1 - t) * lc), bias, ones_sq, (s_io, t_io, lane, row, rowp))


def _mlstm_chunk(x_refs, g_refs, o_refs, c_ref, m_ref, row0, bias, ones_sq, iotas):
    lc = ML_CHUNK
    ngate = 4 * ML_HEADS
    s_io, t_io, lane, row, rowp = iotas
    nprob = rowp.shape[0]
    probs = []
    c_rows = jnp.zeros((nprob, lc), F32)
    for bb, d in [(bb, d) for bb in range(x_refs[0].shape[0]) for d in range(2)]:
        x_ref, g_ref = x_refs[d], g_refs[d]
        rows = slice(row0[d], row0[d] + lc)
        mask = (s_io <= t_io) if d == 0 else (s_io >= t_io)
        gt = (g_ref[bb, rows, :] + bias).T[0:ngate]
        lf = jax.nn.log_sigmoid(gt)
        hi = lf.astype(BF16)
        r1 = lf - hi.astype(F32)
        mid = r1.astype(BF16)
        lo = (r1 - mid.astype(F32)).astype(BF16)
        sums = _dot(jnp.concatenate([hi, mid, lo], axis=0),
                    jnp.concatenate([mask.astype(BF16), ones_sq], axis=1))
        sums = sums[0:ngate] + sums[ngate:2 * ngate] + sums[2 * ngate:3 * ngate]
        b_run, b_tot = sums[:, 0:lc], sums[:, lc:2 * lc]
        x = x_ref[bb, rows, :]
        for pair in range(ML_HEADS // 2):
            qs = x[:, pair * LANES:(pair + 1) * LANES] * (ML_DH ** -0.5)
            ks = x[:, ML_W + pair * LANES:ML_W + (pair + 1) * LANES]
            vt = x[:, 2 * ML_W + pair * LANES:2 * ML_W + (pair + 1) * LANES].T
            for odd in range(2):
                h = 2 * pair + odd
                ci, cf = d * 2 * ML_HEADS + h, d * 2 * ML_HEADS + ML_HEADS + h
                own = (lane >= ML_DH) if odd else (lane < ML_DH)
                own_r = (row >= ML_DH) if odd else (row < ML_DH)
                den_row = 0 if odd else ML_DH
                j = (bb * 2 + d) * ML_HEADS + h
                brow, irow = b_run[cf:cf + 1], gt[ci:ci + 1]
                c_rows = jnp.where(rowp == j, brow - irow, c_rows)
                probs.append(dict(
                    j=j, bb=bb, d=d, pair=pair, odd=odd, mask=mask, den_row=den_row, brow=brow, irow=irow,
                    btot=b_tot[cf:cf + 1],
                    qm=jnp.where(own, qs, 0.0).astype(BF16), ks=ks.astype(BF16),
                    km=jnp.where(own, ks, 0.0).astype(BF16),
                    vaug=jnp.where(own_r, vt, jnp.where(row == den_row, 1.0, 0.0))))

    c_cols = jnp.concatenate([c_rows, jnp.zeros((LANES - nprob, lc), F32)], axis=0).T

    for p in probs:
        p["c_old"] = c_ref[p["j"]]
        p["kq"] = _dot_nt(p["ks"], p["qm"])
        p["inter"] = _dot_nt(p["c_old"].astype(BF16), p["qm"])
    for p in probs:
        j = p["j"]
        m_prev = m_ref[j:j + 1, :]
        dt = jnp.where(p["mask"], p["brow"] - c_cols[:, j:j + 1], -jnp.inf)
        inter_m = p["brow"] + m_prev
        m_row = jnp.maximum(inter_m, jnp.max(dt, axis=0, keepdims=True))
        p["st"] = (p["kq"] * jnp.exp(dt - m_row)).astype(BF16)
        p["w_inter"] = jnp.exp(inter_m - m_row)
        p["floor"] = jnp.exp(-m_row)
        grow = p["btot"] - p["brow"] + p["irow"]
        m_new = jnp.maximum(p["btot"] + m_prev, jnp.max(grow, axis=1, keepdims=True))
        p["w_old"] = jnp.exp(p["btot"] + m_prev - m_new)
        p["wv"] = (p["vaug"] * jnp.exp(grow - m_new)).astype(BF16)
        m_ref[j:j + 1, :] = m_new
    outs = {}
    for p in probs:
        ht = _dot(p["vaug"].astype(BF16), p["st"]) + p["w_inter"] * p["inter"]
        den = ht[p["den_row"]:p["den_row"] + 1]
        outs[(p["bb"], p["d"], p["pair"], p["odd"])] = ht / jnp.maximum(jnp.abs(den), p["floor"])
    for bb, d, pair in sorted({(p["bb"], p["d"], p["pair"]) for p in probs}):
        both = jnp.where(row < ML_DH, outs[(bb, d, pair, 0)], outs[(bb, d, pair, 1)])
        o_refs[d][bb, row0[d]:row0[d] + lc, pair * LANES:(pair + 1) * LANES] = both.T
    for p in probs:
        c_ref[p["j"]] = p["w_old"] * p["c_old"] + _dot(p["wv"], p["km"])


def _mlstm_call(mq, mg, bias_p, n_ctx):
    bsz, s, _ = mq.shape
    rows = ML_BLOCK_CHUNKS * ML_CHUNK
    assert n_ctx % rows == 0 and s % rows == 0
    nch, ncc = s // rows, n_ctx // rows
    nlc = nch - ncc
    fwd = lambda j: (0, jnp.where(j < ncc, nlc + j, j - ncc), 0)
    bwd = lambda j: (0, nch - 1 - j, 0)
    out = jax.ShapeDtypeStruct((bsz, s, ML_W), F32)
    nstate = bsz * 2 * ML_HEADS
    assert nstate <= LANES
    return pl.pallas_call(
        _mlstm_body,
        grid=(nch,),
        in_specs=[pl.BlockSpec((bsz, rows, MQ_W), fwd), pl.BlockSpec((bsz, rows, MG_W), fwd),
                  pl.BlockSpec((bsz, rows, MQ_W), bwd), pl.BlockSpec((bsz, rows, MG_W), bwd),
                  pl.BlockSpec((1, MG_W), lambda j: (0, 0))],
        out_specs=[pl.BlockSpec((bsz, rows, ML_W), fwd), pl.BlockSpec((bsz, rows, ML_W), bwd)],
        out_shape=[out, out],
        scratch_shapes=[pltpu.VMEM((nstate, LANES, LANES), F32),
                        pltpu.VMEM((nstate, LANES), F32)],
        compiler_params=_cparams(("arbitrary",), 48),
        name="mlstm",
    )(mq, mg, mq, mg, bias_p)


LRU_PITCH_PAD = 8
LRU_UNROLL = 8


def _lru_body(xb_ref, gb_ref, cw_ref, cb_ref, wa_ref, ba_ref, wx_ref, bx_ref, lam_ref, y_ref,
              xp_ref, xs_ref, a_ref, u_ref, *, n_ctx, s_len):
    n_lat = s_len - n_ctx
    pad = SUBLANES
    lat_off = n_ctx + 2 * pad
    zeros = jnp.zeros((pad, LANES), F32)
    xp_ref[0:pad, :] = zeros
    xp_ref[pad + n_ctx:lat_off, :] = zeros
    xp_ref[lat_off + n_lat:lat_off + n_lat + pad, :] = zeros
    xp_ref[pad:pad + n_ctx, :] = xb_ref[0, n_lat:n_lat + n_ctx, :]
    cchunk = 512

    def copy_body(c, _):
        src = pl.multiple_of(c * cchunk, SUBLANES)
        dst = pl.multiple_of(lat_off + c * cchunk, SUBLANES)
        xp_ref[pl.ds(dst, cchunk), :] = xb_ref[0, pl.ds(src, cchunk), :]
        return 0

    lax.fori_loop(0, n_lat // cchunk, copy_body, 0)

    cw = cw_ref[...]
    cb = cb_ref[...]

    def conv(src0, dst0, n):
        acc = cb + xp_ref[src0 - CONV_LEFT:src0 - CONV_LEFT + n, :] * cw[0:1, :]
        for j in range(1, CONV_W):
            acc = acc + xp_ref[src0 - CONV_LEFT + j:src0 - CONV_LEFT + j + n, :] * cw[j:j + 1, :]
        xs_ref[dst0:dst0 + n, :] = acc

    conv(pad, n_lat, n_ctx)
    for c in range(n_lat // cchunk):
        conv(lat_off + c * cchunk, c * cchunk, cchunk)

    seg_lat = n_lat // SUBLANES
    seg_ctx = n_ctx // SUBLANES
    p_lat, p_ctx = seg_lat + LRU_PITCH_PAD, seg_ctx + LRU_PITCH_PAD
    ctx_base = SUBLANES * p_lat
    row_io = lax.broadcasted_iota(jnp.int32, (SUBLANES, LANES), 0)

    def gates(x, d):
        xb16 = x.astype(BF16)
        c1 = (-0.5 * LRU_C) * jax.nn.softplus(-lam_ref[d])
        log_a = c1 * jnp.tanh(0.5 * (_dot(xb16, wa_ref[d]) + ba_ref[d])) + c1
        i = _sigmoid(_dot(xb16, wx_ref[d]) + bx_ref[d])
        a = jnp.exp(log_a)
        return a, jnp.sqrt(jnp.tanh(log_a) * (-1.0 - a * a)) * (i * x)

    def fill(d):
        def lat_body(r, _):
            src = pl.multiple_of(r * seg_lat, SUBLANES)
            dst = pl.multiple_of(r * p_lat, SUBLANES)
            a, u = gates(xs_ref[pl.ds(src, seg_lat), :], d)
            a_ref[d, pl.ds(dst, seg_lat), :] = a
            u_ref[d, pl.ds(dst, seg_lat), :] = u
            return 0

        lax.fori_loop(0, SUBLANES, lat_body, 0)
        a, u = gates(xs_ref[n_lat:n_lat + n_ctx, :], d)
        for r in range(SUBLANES):
            a_ref[d, ctx_base + r * p_ctx:ctx_base + r * p_ctx + seg_ctx, :] = a[r * seg_ctx:(r + 1) * seg_ctx, :]
            u_ref[d, ctx_base + r * p_ctx:ctx_base + r * p_ctx + seg_ctx, :] = u[r * seg_ctx:(r + 1) * seg_ctx, :]

    def scan(base, n, pitch, h0s):
        def block(tb, carry):
            idx = [[pl.ds(base + (tb * LRU_UNROLL + k if d == 0 else n - 1 - tb * LRU_UNROLL - k),
                          SUBLANES, stride=pitch) for k in range(LRU_UNROLL)] for d in range(2)]
            av = [[a_ref[d, i, :] for i in idx[d]] for d in range(2)]
            uv = [[u_ref[d, i, :] for i in idx[d]] for d in range(2)]
            carry = list(carry)
            for k in range(LRU_UNROLL):
                for d in range(2):
                    h, acum = carry[d]
                    h = av[d][k] * h + uv[d][k]
                    acum = acum * av[d][k]
                    carry[d] = (h, acum)
                    uv[d][k], av[d][k] = h, acum
            for d in range(2):
                for k in range(LRU_UNROLL):
                    u_ref[d, idx[d][k], :] = uv[d][k]
                    a_ref[d, idx[d][k], :] = av[d][k]
            return tuple(carry)

        init = (jnp.zeros((SUBLANES, LANES), F32), jnp.ones((SUBLANES, LANES), F32))
        ends = lax.fori_loop(0, n // LRU_UNROLL, block, (init, init))
        result = []
        for d in range(2):
            h_end, a_end = ends[d]
            carry = h0s[d]
            cvec = jnp.zeros((SUBLANES, LANES), F32)
            for r in (range(SUBLANES) if d == 0 else range(SUBLANES - 1, -1, -1)):
                cvec = jnp.where(row_io == r, carry, cvec)
                carry = h_end[r:r + 1, :] + a_end[r:r + 1, :] * carry
            result.append((cvec, carry))
        return result

    fill(0)
    fill(1)
    zero_state = jnp.zeros((1, LANES), F32)
    ctx_res = scan(ctx_base, seg_ctx, p_ctx, (zero_state, zero_state))
    lat_res = scan(0, seg_lat, p_lat, (ctx_res[0][1], ctx_res[1][1]))
    carries = {(d, "ctx"): ctx_res[d][0] for d in range(2)}
    carries.update({(d, "lat"): lat_res[d][0] for d in range(2)})

    def emit(kind, r, dst0, src0, n):
        hsum = None
        for d in range(2):
            c_in = carries[d, kind][r:r + 1, :]
            part = u_ref[d, src0:src0 + n, :] + a_ref[d, src0:src0 + n, :] * c_in
            hsum = part if hsum is None else hsum + part
        y_ref[0, dst0:dst0 + n, :] = jax.nn.gelu(gb_ref[0, dst0:dst0 + n, :]) * hsum

    for r in range(SUBLANES):
        emit("lat", r, r * seg_lat, r * p_lat, seg_lat)
        emit("ctx", r, n_lat + r * seg_ctx, ctx_base + r * p_ctx, seg_ctx)


def _lru_call(r, conv_w, conv_b, wa_bd, b_a, wx_bd, b_x, lam, layer, n_ctx):
    bsz, s, _ = r.shape
    nh = LRU_W // LANES
    n_lat = s - n_ctx
    scan_rows = SUBLANES * (n_lat // SUBLANES + LRU_PITCH_PAD) + SUBLANES * (n_ctx // SUBLANES + LRU_PITCH_PAD)
    vec = lambda b, c: (layer, 0, 0, c)
    return pl.pallas_call(
        functools.partial(_lru_body, n_ctx=n_ctx, s_len=s),
        grid=(bsz, nh),
        in_specs=[pl.BlockSpec((1, s, LANES), lambda b, c: (b, 0, c)),
                  pl.BlockSpec((1, s, LANES), lambda b, c: (b, 0, nh + c)),
                  pl.BlockSpec((None, CONV_W, LANES), lambda b, c: (layer, 0, c)),
                  pl.BlockSpec((None, 1, LANES), lambda b, c: (layer, 0, c)),
                  pl.BlockSpec((None, None, 2, LANES, LANES), lambda b, c: (layer, c, 0, 0, 0)),
                  pl.BlockSpec((None, 2, 1, LANES), vec),
                  pl.BlockSpec((None, None, 2, LANES, LANES), lambda b, c: (layer, c, 0, 0, 0)),
                  pl.BlockSpec((None, 2, 1, LANES), vec),
                  pl.BlockSpec((None, 2, 1, LANES), vec)],
        out_specs=pl.BlockSpec((1, s, LANES), lambda b, c: (b, 0, c)),
        out_shape=jax.ShapeDtypeStruct((bsz, s, LRU_W), F32),
        scratch_shapes=[pltpu.VMEM((s + 3 * SUBLANES, LANES), F32),
                        pltpu.VMEM((s, LANES), F32),
                        pltpu.VMEM((2, scan_rows, LANES), F32),
                        pltpu.VMEM((2, scan_rows, LANES), F32)],
        compiler_params=_cparams(("parallel", "parallel"), 48),
        name="rglru",
    )(r, r, conv_w, conv_b, wa_bd, b_a, wx_bd, b_x, lam)


MLP_FF_CHUNK = 2048


def _mix_mlp_body(x_ref, m_ref, ya_ref, hf_ref, hb_ref, og_ref, yc_ref, wo_ref, hn_ref, w1_ref, w2_ref, fg_ref,
                  o_ref, x1_ref, u_ref, acc_ref, *, tm, ctx_start, final):
    b, i, k = pl.program_id(0), pl.program_id(1), pl.program_id(2)

    @pl.when(k == 0)
    def _():
        hsum = hf_ref[0] + hb_ref[0]
        sq = hsum * hsum
        hi = sq.astype(BF16)
        lo = (sq - hi.astype(F32)).astype(BF16)
        msq = _dot(hi, hn_ref[...]) + _dot(lo, hn_ref[...])
        yb = (_sigmoid(og_ref[0]) * (hsum * lax.rsqrt(msq + EPS))).astype(BF16)
        na, nb = MLA_HEADS * MLA_V, MLA_HEADS * MLA_V + ML_W
        y = (_dot(ya_ref[0], wo_ref[0, 0:na, :]) + _dot(yb, wo_ref[0, na:nb, :])
             + _dot(yc_ref[0].astype(BF16), wo_ref[0, nb:, :]))
        x1 = x_ref[0] + _mod_rows(m_ref, b, i * tm, tm, ctx_start, 2) * y
        x1_ref[...] = x1
        shift = _mod_rows(m_ref, b, i * tm, tm, ctx_start, 3)
        scale = _mod_rows(m_ref, b, i * tm, tm, ctx_start, 4)
        u_ref[...] = (_rms(x1) * (1.0 + scale) + shift).astype(BF16)
        acc_ref[...] = jnp.zeros_like(acc_ref)

    hid = jnp.maximum(_dot(u_ref[...], w1_ref[0]), 0.0)
    acc_ref[...] += _dot((hid * hid).astype(BF16), w2_ref[0])

    @pl.when(k == pl.num_programs(2) - 1)
    def _():
        res = x1_ref[...] + _mod_rows(m_ref, b, i * tm, tm, ctx_start, 5) * acc_ref[...]
        if final:
            res = _rms(res) * fg_ref[...]
        o_ref[0] = res


def _mix_mlp_call(x, mods, ya, hf, hb, mq, yc, w_out, head_avg, w1, w2, final_g, layer, n_ctx, final):
    bsz, s, d = x.shape
    rows_out = s - n_ctx if final else s
    tm = _pick_tile(rows_out, (544, 512, 256))
    fc = MLP_FF_CHUNK
    row = lambda b, i, k: (b, i, 0)
    lsel = lambda b, i, k: (layer, 0, 0)
    return pl.pallas_call(
        functools.partial(_mix_mlp_body, tm=tm, ctx_start=None if final else s - n_ctx, final=final),
        grid=(bsz, rows_out // tm, D_FF // fc),
        in_specs=[pl.BlockSpec((1, tm, d), row),
                  pl.BlockSpec((1, MOD_ROWS, 6 * d), lsel),
                  pl.BlockSpec((1, tm, MLA_HEADS * MLA_V), row),
                  pl.BlockSpec((1, tm, ML_W), row),
                  pl.BlockSpec((1, tm, ML_W), row),
                  pl.BlockSpec((1, tm, ML_W), lambda b, i, k: (b, i, 3)),
                  pl.BlockSpec((1, tm, LRU_W), row),
                  pl.BlockSpec((1, d, d), lsel),
                  pl.BlockSpec((ML_W, ML_W), lambda b, i, k: (0, 0)),
                  pl.BlockSpec((1, d, fc), lambda b, i, k: (layer, 0, k)),
                  pl.BlockSpec((1, fc, d), lambda b, i, k: (layer, k, 0)),
                  pl.BlockSpec((1, d), lambda b, i, k: (0, 0))],
        out_specs=pl.BlockSpec((1, tm, d), row),
        out_shape=jax.ShapeDtypeStruct((bsz, rows_out, d), F32),
        scratch_shapes=[pltpu.VMEM((tm, d), F32), pltpu.VMEM((tm, d), BF16), pltpu.VMEM((tm, d), F32)],
        compiler_params=_cparams(("parallel", "parallel", "arbitrary"), 48),
        name="mix_mlp",
    )(x, mods, ya, hf, hb, mq, yc, w_out, head_avg, w1, w2, final_g)


def _rope_tables(seq, n_ctx):
    half = MLA_ROPE // 2
    row = jnp.repeat(jnp.arange(seq // GRID_W), GRID_W).astype(F32)
    col = jnp.tile(jnp.arange(GRID_W), seq // GRID_W).astype(F32)
    freqs = 1.0 / (ROPE_BASE ** (jnp.arange(0, half, 2, dtype=F32) / half))
    ang = jnp.concatenate([row[:, None] * freqs, col[:, None] * freqs], axis=-1)
    cos = jnp.concatenate([jnp.cos(ang), jnp.ones((n_ctx, half), F32)], axis=0)
    sin = jnp.concatenate([jnp.sin(ang), jnp.zeros((n_ctx, half), F32)], axis=0)
    return cos, sin


def _block_diag_halves(w):
    depth = w.shape[0]
    per = LANES // LRU_BD
    wh = w.reshape(depth, 2, LRU_BLOCKS // per, per, LRU_BD, LRU_BD)
    eye = jnp.eye(per, dtype=w.dtype)
    bd = jnp.einsum("ldcpio,pq->ldcpiqo", wh, eye).reshape(depth, 2, LRU_BLOCKS // per, LANES, LANES)
    return bd.transpose(0, 2, 1, 3, 4).astype(BF16)


def kernel(x, c, ctx, c_ctx, w_mod, b_mod, w_in, mla_g_q, mla_w_uq, mla_g_kv, mla_w_ukv, ml_gate_bias,
           lru_conv_w, lru_conv_b, lru_w_a, lru_b_a, lru_w_x, lru_b_x, lru_lam, w_out, w_ff1, w_ff2, final_g):
    bsz, seq, d = x.shape
    n_ctx = ctx.shape[1]
    depth = w_in.shape[0]
    assert bsz <= CTX_MOD_ROW and d == D_MODEL and n_ctx % 256 == 0

    w_in16 = w_in.astype(BF16)
    zc = lambda n: jnp.zeros((depth, d, n), BF16)
    ml0, ml1 = MLA_IN, MLA_IN + 4 * ML_W
    w_groups = (jnp.concatenate([w_in16[:, :, :ml0], zc(A_W - MLA_IN)], axis=-1),
                w_in16[:, :, ml0:ml1],
                jnp.concatenate([w_in16[:, :, ml1:ml1 + 4 * ML_HEADS], zc(MG_W - 4 * ML_HEADS)], axis=-1),
                w_in16[:, :, ml1 + 4 * ML_HEADS:])
    w_uq_t = mla_w_uq.astype(BF16).transpose(0, 2, 1)
    ukv = mla_w_ukv.reshape(depth, MLA_KV_RANK, MLA_HEADS, MLA_NOPE + MLA_V)
    w_ukv_k = ukv[..., :MLA_NOPE].reshape(depth, MLA_KV_RANK, -1).astype(BF16)
    w_ukv_vt = ukv[..., MLA_NOPE:].reshape(depth, MLA_KV_RANK, -1).astype(BF16).transpose(0, 2, 1)
    g_q = mla_g_q.reshape(depth, 1, MLA_Q_RANK)
    g_kv = mla_g_kv.reshape(depth, 1, MLA_KV_RANK)
    bias_p = jnp.pad(ml_gate_bias, ((0, 0), (0, MG_W - 4 * ML_HEADS)))
    wa_bd, wx_bd = _block_diag_halves(lru_w_a), _block_diag_halves(lru_w_x)
    vec4 = lambda v: v.reshape(depth, 2, 1, LRU_W)
    conv_b = lru_conv_b.reshape(depth, 1, LRU_W)
    w_out16, w1_16, w2_16 = w_out.astype(BF16), w_ff1.astype(BF16), w_ff2.astype(BF16)
    head_avg = jnp.kron(jnp.eye(ML_HEADS, dtype=F32), jnp.full((ML_DH, ML_DH), 1.0 / ML_DH, F32)).astype(BF16)
    cos, sin = _rope_tables(seq, n_ctx)
    rope = (cos, sin, cos.T, sin.T)
    fg = final_g.reshape(1, d)

    cvec = jnp.concatenate([c, jnp.zeros((CTX_MOD_ROW - bsz, d), c.dtype), c_ctx[None, :],
                            jnp.zeros((MOD_ROWS - CTX_MOD_ROW - 1, d), c.dtype)], axis=0)
    mods = _mods_call(cvec, w_mod, b_mod)

    xs = jnp.concatenate([x, ctx], axis=1)
    for l in range(depth):
        last = l == depth - 1
        a, mq, mg, r = _in_call(xs, mods, w_groups, l, n_ctx)
        qt, k, vt = _prep_call(a, g_q, w_uq_t, g_kv, w_ukv_k, w_ukv_vt, rope, l)
        ya = _attn_call(qt, k, vt, n_ctx, with_ctx=not last)
        hf, hb = _mlstm_call(mq, mg, bias_p[l:l + 1], n_ctx)
        yc = _lru_call(r, lru_conv_w, conv_b, wa_bd, vec4(lru_b_a), wx_bd, vec4(lru_b_x), vec4(lru_lam), l, n_ctx)
        xs = _mix_mlp_call(xs, mods, ya, hf, hb, mq, yc, w_out16, head_avg, w1_16, w2_16, fg, l, n_ctx,
                           final=last)
    return xs
```

```python
import functools

import jax
import jax.numpy as jnp
from jax import lax
from jax.experimental import pallas as pl
from jax.experimental.pallas import tpu as pltpu

F32 = jnp.float32
BF16 = jnp.bfloat16

D_MODEL = 1024
DEPTH = 4
GRID_W = 64
N_CTX = 256
MLA_HEADS = 4
MLA_Q_RANK = 256
MLA_KV_RANK = 128
MLA_NOPE = 128
MLA_ROPE = 64
MLA_V = 128
MLA_QK = MLA_NOPE + MLA_ROPE
MLA_SCALE = MLA_QK ** -0.5
ROPE_BASE = 10000.0
ML_HEADS = 4
ML_DH = 64
ML_W = ML_HEADS * ML_DH
ML_CHUNK = 128
LRU_W = 256
LRU_BLOCKS = 4
LRU_BD = LRU_W // LRU_BLOCKS
CONV_W = 4
CONV_LEFT = 2
LRU_C = 8.0
D_FF = 4 * D_MODEL
EPS = 1e-6
MLA_IN = MLA_Q_RANK + MLA_KV_RANK + MLA_ROPE
ML_IN = 4 * ML_W + 4 * ML_HEADS
LRU_IN = 2 * LRU_W

LANES = 128
SUBLANES = 8
MOD_ROWS = 8
CTX_MOD_ROW = 4

A_W = 512
MQ_W = 4 * ML_W
MG_W = LANES
R_W = 2 * LRU_W


def _cparams(sem, vmem_mb):
    return pltpu.CompilerParams(dimension_semantics=sem, vmem_limit_bytes=vmem_mb * 1024 * 1024)


def _mod_rows(m_ref, b, row0, tm, ctx_start, seg):
    lo, hi = seg * D_MODEL, (seg + 1) * D_MODEL
    lat = m_ref[0, pl.ds(b, 1), lo:hi]
    if ctx_start is None:
        return lat
    ctx = m_ref[0, CTX_MOD_ROW:CTX_MOD_ROW + 1, lo:hi]
    rows = row0 + lax.broadcasted_iota(jnp.int32, (tm, 1), 0)
    return jnp.where(rows >= ctx_start, ctx, lat)


def _pick_tile(n, candidates):
    return next(t for t in candidates if n % t == 0)


def _rms(x):
    return x * lax.rsqrt(jnp.mean(x * x, axis=-1, keepdims=True) + EPS)


def _sigmoid(x):
    return 0.5 * jnp.tanh(0.5 * x) + 0.5


def _dot(a, b):
    return jnp.dot(a, b, preferred_element_type=F32)


def _dot_nt(a, b):
    return lax.dot_general(a, b, (((1,), (1,)), ((), ())), preferred_element_type=F32)


def _mods_body(c_ref, w_ref, b_ref, o_ref):
    cv = c_ref[...]
    act = (cv * jax.nn.sigmoid(cv)).astype(BF16)
    o_ref[0] = _dot(act, w_ref[0].astype(BF16)) + b_ref[0]


def _mods_call(cvec, w_mod, b_mod):
    depth, d, n = w_mod.shape
    tn = 1536
    return pl.pallas_call(
        _mods_body,
        grid=(depth, n // tn),
        in_specs=[pl.BlockSpec((MOD_ROWS, d), lambda l, j: (0, 0)),
                  pl.BlockSpec((1, d, tn), lambda l, j: (l, 0, j)),
                  pl.BlockSpec((1, 1, tn), lambda l, j: (l, 0, j))],
        out_specs=pl.BlockSpec((1, MOD_ROWS, tn), lambda l, j: (l, 0, j)),
        out_shape=jax.ShapeDtypeStruct((depth, MOD_ROWS, n), F32),
        compiler_params=_cparams(("arbitrary", "arbitrary"), 40),
        name="mods",
    )(cvec, w_mod, b_mod.reshape(depth, 1, n))


def _in_body(x_ref, m_ref, wa_ref, wq_ref, wg_ref, wr_ref, a_ref, q_ref, g_ref, r_ref, *, tm, ctx_start):
    b, i = pl.program_id(0), pl.program_id(1)
    xn = _rms(x_ref[0])
    shift = _mod_rows(m_ref, b, i * tm, tm, ctx_start, 0)
    scale = _mod_rows(m_ref, b, i * tm, tm, ctx_start, 1)
    u = (xn * (1.0 + scale) + shift).astype(BF16)
    a_ref[0] = _dot(u, wa_ref[0])
    q_ref[0] = _dot(u, wq_ref[0])
    g_ref[0] = _dot(u, wg_ref[0])
    r_ref[0] = _dot(u, wr_ref[0])


def _in_call(x, mods, w_groups, layer, n_ctx):
    bsz, s, d = x.shape
    tm = _pick_tile(s, (1088, 256))
    row = lambda b, i: (b, i, 0)
    return pl.pallas_call(
        functools.partial(_in_body, tm=tm, ctx_start=s - n_ctx),
        grid=(bsz, s // tm),
        in_specs=[pl.BlockSpec((1, tm, d), row),
                  pl.BlockSpec((1, MOD_ROWS, 6 * d), lambda b, i: (layer, 0, 0)),
                  *[pl.BlockSpec((1, d, w), lambda b, i: (layer, 0, 0)) for w in (A_W, MQ_W, MG_W, R_W)]],
        out_specs=[pl.BlockSpec((1, tm, A_W), row), pl.BlockSpec((1, tm, MQ_W), row),
                   pl.BlockSpec((1, tm, MG_W), row), pl.BlockSpec((1, tm, R_W), row)],
        out_shape=[jax.ShapeDtypeStruct((bsz, s, w), F32) for w in (A_W, MQ_W, MG_W, R_W)],
        compiler_params=_cparams(("parallel", "parallel"), 52),
        name="in_proj",
    )(x, mods, *w_groups)


LOG2E = 1.4426950408889634


def _prep_body(a_ref, gq_ref, gkv_ref, wuqt_ref, wukvk_ref, wukvvt_ref, cos_ref, sin_ref, cost_ref, sint_ref,
               qt_ref, k_ref, vt_ref):
    for bb in range(a_ref.shape[0]):
        _prep_sample(bb, a_ref, gq_ref, gkv_ref, wuqt_ref, wukvk_ref, wukvvt_ref, cos_ref, sin_ref, cost_ref,
                     sint_ref, qt_ref, k_ref, vt_ref)


def _prep_sample(bb, a_ref, gq_ref, gkv_ref, wuqt_ref, wukvk_ref, wukvvt_ref, cos_ref, sin_ref, cost_ref, sint_ref,
                 qt_ref, k_ref, vt_ref):
    a = a_ref[bb]
    nq, nkv = MLA_Q_RANK, MLA_Q_RANK + MLA_KV_RANK
    half = MLA_ROPE // 2
    cq_t = (_rms(a[:, 0:nq]) * gq_ref[0]).T.astype(BF16)
    ckv = _rms(a[:, nq:nkv]) * gkv_ref[0]
    q_t = _dot(wuqt_ref[0], cq_t) * (MLA_SCALE * LOG2E)
    cos_t, sin_t = cost_ref[...], sint_ref[...]
    k_nope = _dot(ckv.astype(BF16), wukvk_ref[0])
    v_t = _dot(wukvvt_ref[0], ckv.T.astype(BF16))
    k1, k2 = a[:, nkv:nkv + half], a[:, nkv + half:nkv + 2 * half]
    c32, s32 = cos_ref[...], sin_ref[...]
    kr1 = (k1 * c32 - k2 * s32).astype(BF16)
    kr2 = (k1 * s32 + k2 * c32).astype(BF16)
    for h in range(MLA_HEADS):
        q_h = q_t[h * MLA_QK:(h + 1) * MLA_QK]
        x1, x2 = q_h[MLA_NOPE:MLA_NOPE + half], q_h[MLA_NOPE + half:MLA_QK]
        qt_ref[bb, h, 0:MLA_NOPE, :] = q_h[0:MLA_NOPE].astype(BF16)
        qt_ref[bb, h, MLA_NOPE:MLA_NOPE + half, :] = (x1 * cos_t - x2 * sin_t).astype(BF16)
        qt_ref[bb, h, MLA_NOPE + half:MLA_QK, :] = (x1 * sin_t + x2 * cos_t).astype(BF16)
        k_ref[bb, h, :, 0:MLA_NOPE] = k_nope[:, h * MLA_NOPE:(h + 1) * MLA_NOPE].astype(BF16)
        k_ref[bb, h, :, MLA_NOPE:MLA_NOPE + half] = kr1
        k_ref[bb, h, :, MLA_NOPE + half:MLA_QK] = kr2
        vt_ref[bb, h] = v_t[h * MLA_V:(h + 1) * MLA_V].astype(BF16)


def _prep_call(a, g_q, w_uq_t, g_kv, w_ukv_k, w_ukv_vt, rope, layer):
    bsz, s, _ = a.shape
    tm = 256
    half = MLA_ROPE // 2
    lsel = lambda i: (layer, 0, 0)
    cos, sin, cos_t, sin_t = rope
    return pl.pallas_call(
        _prep_body,
        grid=(s // tm,),
        in_specs=[pl.BlockSpec((bsz, tm, A_W), lambda i: (0, i, 0)),
                  pl.BlockSpec((1, 1, MLA_Q_RANK), lsel),
                  pl.BlockSpec((1, 1, MLA_KV_RANK), lsel),
                  pl.BlockSpec((1, MLA_HEADS * MLA_QK, MLA_Q_RANK), lsel),
                  pl.BlockSpec((1, MLA_KV_RANK, MLA_HEADS * MLA_NOPE), lsel),
                  pl.BlockSpec((1, MLA_HEADS * MLA_V, MLA_KV_RANK), lsel),
                  pl.BlockSpec((tm, half), lambda i: (i, 0)),
                  pl.BlockSpec((tm, half), lambda i: (i, 0)),
                  pl.BlockSpec((half, tm), lambda i: (0, i)),
                  pl.BlockSpec((half, tm), lambda i: (0, i))],
        out_specs=[pl.BlockSpec((bsz, MLA_HEADS, MLA_QK, tm), lambda i: (0, 0, 0, i)),
                   pl.BlockSpec((bsz, MLA_HEADS, tm, MLA_QK), lambda i: (0, 0, i, 0)),
                   pl.BlockSpec((bsz, MLA_HEADS, MLA_V, tm), lambda i: (0, 0, 0, i))],
        out_shape=[jax.ShapeDtypeStruct((bsz, MLA_HEADS, MLA_QK, s), BF16),
                   jax.ShapeDtypeStruct((bsz, MLA_HEADS, s, MLA_QK), BF16),
                   jax.ShapeDtypeStruct((bsz, MLA_HEADS, MLA_V, s), BF16)],
        compiler_params=_cparams(("parallel",), 40),
        name="mla_prep",
    )(a, g_q, g_kv, w_uq_t, w_ukv_k, w_ukv_vt, cos, sin, cos_t, sin_t)


def _attn_body(qt_ref, k_ref, vt_ref, o_ref, *, subs, zero_rows):
    work = [(q0, tq, c, c is chunks[0], c is chunks[-1]) for q0, tq, chunks in subs for c in chunks]
    score = lambda w: _dot(k_ref[0, 0, w[2][0]:w[2][0] + w[2][1], :],
                           qt_ref[0, 0, :, w[0]:w[0] + w[1]])
    pending = [score(w) for w in work[:ATTN_AHEAD]]
    m = l = acc = None
    for idx, (q0, tq, (start, size), first, last) in enumerate(work):
        if first:
            m = jnp.full((1, tq), -jnp.inf, F32)
            l = jnp.zeros((1, tq), F32)
            acc = jnp.zeros((MLA_V, tq), F32)
        st = pending.pop(0)
        if idx + ATTN_AHEAD < len(work):
            pending.append(score(work[idx + ATTN_AHEAD]))
        m_new = jnp.maximum(m, jnp.max(st, axis=0, keepdims=True))
        p = jnp.exp2(st - m_new)
        alpha = jnp.exp2(m - m_new)
        l = alpha * l + jnp.sum(p, axis=0, keepdims=True)
        acc = alpha * acc + _dot(vt_ref[0, 0, :, start:start + size], p.astype(BF16))
        m = m_new
        if last:
            o_ref[0, q0:q0 + tq, :] = (acc / l).T.astype(o_ref.dtype)
    if zero_rows is not None:
        o_ref[0, zero_rows[0]:zero_rows[1], :] = jnp.zeros((zero_rows[1] - zero_rows[0], MLA_V), o_ref.dtype)


ATTN_TQ = 512
ATTN_TK = 1024
ATTN_AHEAD = 2


def _attn_call(qt, k, vt, n_ctx, with_ctx):
    bsz, nh, s, dk = k.shape
    n_lat = s - n_ctx
    assert n_lat % ATTN_TQ == 0 and n_lat % ATTN_TK == 0
    ctx_chunk = (n_lat, n_ctx)
    lat_chunks = (ctx_chunk,) + tuple((j * ATTN_TK, ATTN_TK) for j in range(n_lat // ATTN_TK))
    subs = [(j * ATTN_TQ, ATTN_TQ, lat_chunks) for j in range(n_lat // ATTN_TQ)]
    if with_ctx:
        subs.append((n_lat, n_ctx, (ctx_chunk,)))
    return pl.pallas_call(
        functools.partial(_attn_body, subs=tuple(subs), zero_rows=None if with_ctx else (n_lat, s)),
        grid=(bsz, nh),
        in_specs=[pl.BlockSpec((1, 1, dk, s), lambda b, h: (b, h, 0, 0)),
                  pl.BlockSpec((1, 1, s, dk), lambda b, h: (b, h, 0, 0)),
                  pl.BlockSpec((1, 1, MLA_V, s), lambda b, h: (b, h, 0, 0))],
        out_specs=pl.BlockSpec((1, s, MLA_V), lambda b, h: (b, 0, h)),
        out_shape=jax.ShapeDtypeStruct((bsz, s, nh * MLA_V), BF16),
        compiler_params=_cparams(("parallel", "parallel"), 48),
        name="mla_attn",
    )(qt, k, vt)


ML_BLOCK_CHUNKS = 2


def _mlstm_body(xf_ref, gf_ref, xb_ref, gb_ref, bias_ref, hf_ref, hb_ref, c_ref, m_ref):
    lc = ML_CHUNK
    assert lc == LANES

    @pl.when(pl.program_id(0) == 0)
    def _():
        c_ref[...] = jnp.zeros_like(c_ref)
        m_ref[...] = jnp.zeros_like(m_ref)

    nprob = xf_ref.shape[0] * 2 * ML_HEADS
    s_io = lax.broadcasted_iota(jnp.int32, (lc, lc), 0)
    t_io = lax.broadcasted_iota(jnp.int32, (lc, lc), 1)
    lane = lax.broadcasted_iota(jnp.int32, (lc, LANES), 1)
    row = lax.broadcasted_iota(jnp.int32, (LANES, lc), 0)
    rowp = lax.broadcasted_iota(jnp.int32, (nprob, lc), 0)
    bias = bias_ref[...]
    ones_sq = jnp.ones((lc, lc), BF16)
    nsub = xf_ref.shape[1] // lc
    for t in range(nsub):
        _mlstm_chunk((xf_ref, xb_ref), (gf_ref, gb_ref), (hf_ref, hb_ref), c_ref, m_ref,
                     (t * lc, (nsub - 1 - t) * lc), bias, ones_sq, (s_io, t_io, lane, row, rowp))


def _mlstm_chunk(x_refs, g_refs, o_refs, c_ref, m_ref, row0, bias, ones_sq, iotas):
    lc = ML_CHUNK
    ngate = 4 * ML_HEADS
    s_io, t_io, lane, row, rowp = iotas
    nprob = rowp.shape[0]
    probs = []
    c_rows = jnp.zeros((nprob, lc), F32)
    for bb, d in [(bb, d) for bb in range(x_refs[0].shape[0]) for d in range(2)]:
        x_ref, g_ref = x_refs[d], g_refs[d]
        rows = slice(row0[d], row0[d] + lc)
        mask = (s_io <= t_io) if d == 0 else (s_io >= t_io)
        gt = (g_ref[bb, rows, :] + bias).T[0:ngate]
        lf = jax.nn.log_sigmoid(gt)
        hi = lf.astype(BF16)
        r1 = lf - hi.astype(F32)
        mid = r1.astype(BF16)
        lo = (r1 - mid.astype(F32)).astype(BF16)
        sums = _dot(jnp.concatenate([hi, mid, lo], axis=0),
                    jnp.concatenate([mask.astype(BF16), ones_sq], axis=1))
        sums = sums[0:ngate] + sums[ngate:2 * ngate] + sums[2 * ngate:3 * ngate]
        b_run, b_tot = sums[:, 0:lc], sums[:, lc:2 * lc]
        x = x_ref[bb, rows, :]
        for pair in range(ML_HEADS // 2):
            qs = x[:, pair * LANES:(pair + 1) * LANES] * (ML_DH ** -0.5)
            ks = x[:, ML_W + pair * LANES:ML_W + (pair + 1) * LANES]
            vt = x[:, 2 * ML_W + pair * LANES:2 * ML_W + (pair + 1) * LANES].T
            for odd in range(2):
                h = 2 * pair + odd
                ci, cf = d * 2 * ML_HEADS + h, d * 2 * ML_HEADS + ML_HEADS + h
                own = (lane >= ML_DH) if odd else (lane < ML_DH)
                own_r = (row >= ML_DH) if odd else (row < ML_DH)
                den_row = 0 if odd else ML_DH
                j = (bb * 2 + d) * ML_HEADS + h
                brow, irow = b_run[cf:cf + 1], gt[ci:ci + 1]
                c_rows = jnp.where(rowp == j, brow - irow, c_rows)
                probs.append(dict(
                    j=j, bb=bb, d=d, pair=pair, odd=odd, mask=mask, den_row=den_row, brow=brow, irow=irow,
                    btot=b_tot[cf:cf + 1],
                    qm=jnp.where(own, qs, 0.0).astype(BF16), ks=ks.astype(BF16),
                    km=jnp.where(own, ks, 0.0).astype(BF16),
                    vaug=jnp.where(own_r, vt, jnp.where(row == den_row, 1.0, 0.0))))

    c_cols = jnp.concatenate([c_rows, jnp.zeros((LANES - nprob, lc), F32)], axis=0).T

    for p in probs:
        p["c_old"] = c_ref[p["j"]]
        p["kq"] = _dot_nt(p["ks"], p["qm"])
        p["inter"] = _dot_nt(p["c_old"].astype(BF16), p["qm"])
    for p in probs:
        j = p["j"]
        m_prev = m_ref[j:j + 1, :]
        dt = jnp.where(p["mask"], p["brow"] - c_cols[:, j:j + 1], -jnp.inf)
        inter_m = p["brow"] + m_prev
        m_row = jnp.maximum(inter_m, jnp.max(dt, axis=0, keepdims=True))
        p["st"] = (p["kq"] * jnp.exp(dt - m_row)).astype(BF16)
        p["w_inter"] = jnp.exp(inter_m - m_row)
        p["floor"] = jnp.exp(-m_row)
        grow = p["btot"] - p["brow"] + p["irow"]
        m_new = jnp.maximum(p["btot"] + m_prev, jnp.max(grow, axis=1, keepdims=True))
        p["w_old"] = jnp.exp(p["btot"] + m_prev - m_new)
        p["wv"] = (p["vaug"] * jnp.exp(grow - m_new)).astype(BF16)
        m_ref[j:j + 1, :] = m_new
    outs = {}
    for p in probs:
        ht = _dot(p["vaug"].astype(BF16), p["st"]) + p["w_inter"] * p["inter"]
        den = ht[p["den_row"]:p["den_row"] + 1]
        outs[(p["bb"], p["d"], p["pair"], p["odd"])] = ht / jnp.maximum(jnp.abs(den), p["floor"])
    for bb, d, pair in sorted({(p["bb"], p["d"], p["pair"]) for p in probs}):
        both = jnp.where(row < ML_DH, outs[(bb, d, pair, 0)], outs[(bb, d, pair, 1)])
        o_refs[d][bb, row0[d]:row0[d] + lc, pair * LANES:(pair + 1) * LANES] = both.T
    for p in probs:
        c_ref[p["j"]] = p["w_old"] * p["c_old"] + _dot(p["wv"], p["km"])


def _mlstm_call(mq, mg, bias_p, n_ctx):
    bsz, s, _ = mq.shape
    rows = ML_BLOCK_CHUNKS * ML_CHUNK
    assert n_ctx % rows == 0 and s % rows == 0
    nch, ncc = s // rows, n_ctx // rows
    nlc = nch - ncc
    fwd = lambda j: (0, jnp.where(j < ncc, nlc + j, j - ncc), 0)
    bwd = lambda j: (0, nch - 1 - j, 0)
    out = jax.ShapeDtypeStruct((bsz, s, ML_W), F32)
    nstate = bsz * 2 * ML_HEADS
    assert nstate <= LANES
    return pl.pallas_call(
        _mlstm_body,
        grid=(nch,),
        in_specs=[pl.BlockSpec((bsz, rows, MQ_W), fwd), pl.BlockSpec((bsz, rows, MG_W), fwd),
                  pl.BlockSpec((bsz, rows, MQ_W), bwd), pl.BlockSpec((bsz, rows, MG_W), bwd),
                  pl.BlockSpec((1, MG_W), lambda j: (0, 0))],
        out_specs=[pl.BlockSpec((bsz, rows, ML_W), fwd), pl.BlockSpec((bsz, rows, ML_W), bwd)],
        out_shape=[out, out],
        scratch_shapes=[pltpu.VMEM((nstate, LANES, LANES), F32),
                        pltpu.VMEM((nstate, LANES), F32)],
        compiler_params=_cparams(("arbitrary",), 48),
        name="mlstm",
    )(mq, mg, mq, mg, bias_p)


LRU_PITCH_PAD = 8
LRU_UNROLL = 8


def _lru_body(xb_ref, gb_ref, cw_ref, cb_ref, wa_ref, ba_ref, wx_ref, bx_ref, lam_ref, y_ref,
              xp_ref, xs_ref, a_ref, u_ref, *, n_ctx, s_len):
    n_lat = s_len - n_ctx
    pad = SUBLANES
    lat_off = n_ctx + 2 * pad
    zeros = jnp.zeros((pad, LANES), F32)
    xp_ref[0:pad, :] = zeros
    xp_ref[pad + n_ctx:lat_off, :] = zeros
    xp_ref[lat_off + n_lat:lat_off + n_lat + pad, :] = zeros
    xp_ref[pad:pad + n_ctx, :] = xb_ref[0, n_lat:n_lat + n_ctx, :]
    cchunk = 512

    def copy_body(c, _):
        src = pl.multiple_of(c * cchunk, SUBLANES)
        dst = pl.multiple_of(lat_off + c * cchunk, SUBLANES)
        xp_ref[pl.ds(dst, cchunk), :] = xb_ref[0, pl.ds(src, cchunk), :]
        return 0

    lax.fori_loop(0, n_lat // cchunk, copy_body, 0)

    cw = cw_ref[...]
    cb = cb_ref[...]

    def conv(src0, dst0, n):
        acc = cb + xp_ref[src0 - CONV_LEFT:src0 - CONV_LEFT + n, :] * cw[0:1, :]
        for j in range(1, CONV_W):
            acc = acc + xp_ref[src0 - CONV_LEFT + j:src0 - CONV_LEFT + j + n, :] * cw[j:j + 1, :]
        xs_ref[dst0:dst0 + n, :] = acc

    conv(pad, n_lat, n_ctx)
    for c in range(n_lat // cchunk):
        conv(lat_off + c * cchunk, c * cchunk, cchunk)

    seg_lat = n_lat // SUBLANES
    seg_ctx = n_ctx // SUBLANES
    p_lat, p_ctx = seg_lat + LRU_PITCH_PAD, seg_ctx + LRU_PITCH_PAD
    ctx_base = SUBLANES * p_lat
    row_io = lax.broadcasted_iota(jnp.int32, (SUBLANES, LANES), 0)

    def gates(x, d):
        xb16 = x.astype(BF16)
        c1 = (-0.5 * LRU_C) * jax.nn.softplus(-lam_ref[d])
        log_a = c1 * jnp.tanh(0.5 * (_dot(xb16, wa_ref[d]) + ba_ref[d])) + c1
        i = _sigmoid(_dot(xb16, wx_ref[d]) + bx_ref[d])
        a = jnp.exp(log_a)
        return a, jnp.sqrt(jnp.tanh(log_a) * (-1.0 - a * a)) * (i * x)

    def fill(d):
        def lat_body(r, _):
            src = pl.multiple_of(r * seg_lat, SUBLANES)
            dst = pl.multiple_of(r * p_lat, SUBLANES)
            a, u = gates(xs_ref[pl.ds(src, seg_lat), :], d)
            a_ref[d, pl.ds(dst, seg_lat), :] = a
            u_ref[d, pl.ds(dst, seg_lat), :] = u
            return 0

        lax.fori_loop(0, SUBLANES, lat_body, 0)
        a, u = gates(xs_ref[n_lat:n_lat + n_ctx, :], d)
        for r in range(SUBLANES):
            a_ref[d, ctx_base + r * p_ctx:ctx_base + r * p_ctx + seg_ctx, :] = a[r * seg_ctx:(r + 1) * seg_ctx, :]
            u_ref[d, ctx_base + r * p_ctx:ctx_base + r * p_ctx + seg_ctx, :] = u[r * seg_ctx:(r + 1) * seg_ctx, :]

    def scan(base, n, pitch, h0s):
        def block(tb, carry):
            idx = [[pl.ds(base + (tb * LRU_UNROLL + k if d == 0 else n - 1 - tb * LRU_UNROLL - k),
                          SUBLANES, stride=pitch) for k in range(LRU_UNROLL)] for d in range(2)]
            av = [[a_ref[d, i, :] for i in idx[d]] for d in range(2)]
            uv = [[u_ref[d, i, :] for i in idx[d]] for d in range(2)]
            carry = list(carry)
            for k in range(LRU_UNROLL):
                for d in range(2):
                    h, acum = carry[d]
                    h = av[d][k] * h + uv[d][k]
                    acum = acum * av[d][k]
                    carry[d] = (h, acum)
                    uv[d][k], av[d][k] = h, acum
            for d in range(2):
                for k in range(LRU_UNROLL):
                    u_ref[d, idx[d][k], :] = uv[d][k]
                    a_ref[d, idx[d][k], :] = av[d][k]
            return tuple(carry)

        init = (jnp.zeros((SUBLANES, LANES), F32), jnp.ones((SUBLANES, LANES), F32))
        ends = lax.fori_loop(0, n // LRU_UNROLL, block, (init, init))
        result = []
        for d in range(2):
            h_end, a_end = ends[d]
            carry = h0s[d]
            cvec = jnp.zeros((SUBLANES, LANES), F32)
            for r in (range(SUBLANES) if d == 0 else range(SUBLANES - 1, -1, -1)):
                cvec = jnp.where(row_io == r, carry, cvec)
                carry = h_end[r:r + 1, :] + a_end[r:r + 1, :] * carry
            result.append((cvec, carry))
        return result

    fill(0)
    fill(1)
    zero_state = jnp.zeros((1, LANES), F32)
    ctx_res = scan(ctx_base, seg_ctx, p_ctx, (zero_state, zero_state))
    lat_res = scan(0, seg_lat, p_lat, (ctx_res[0][1], ctx_res[1][1]))
    carries = {(d, "ctx"): ctx_res[d][0] for d in range(2)}
    carries.update({(d, "lat"): lat_res[d][0] for d in range(2)})

    def emit(kind, r, dst0, src0, n):
        hsum = None
        for d in range(2):
            c_in = carries[d, kind][r:r + 1, :]
            part = u_ref[d, src0:src0 + n, :] + a_ref[d, src0:src0 + n, :] * c_in
            hsum = part if hsum is None else hsum + part
        y_ref[0, dst0:dst0 + n, :] = jax.nn.gelu(gb_ref[0, dst0:dst0 + n, :]) * hsum

    for r in range(SUBLANES):
        emit("lat", r, r * seg_lat, r * p_lat, seg_lat)
        emit("ctx", r, n_lat + r * seg_ctx, ctx_base + r * p_ctx, seg_ctx)


def _lru_call(r, conv_w, conv_b, wa_bd, b_a, wx_bd, b_x, lam, layer, n_ctx):
    bsz, s, _ = r.shape
    nh = LRU_W // LANES
    n_lat = s - n_ctx
    scan_rows = SUBLANES * (n_lat // SUBLANES + LRU_PITCH_PAD) + SUBLANES * (n_ctx // SUBLANES + LRU_PITCH_PAD)
    vec = lambda b, c: (layer, 0, 0, c)
    return pl.pallas_call(
        functools.partial(_lru_body, n_ctx=n_ctx, s_len=s),
        grid=(bsz, nh),
        in_specs=[pl.BlockSpec((1, s, LANES), lambda b, c: (b, 0, c)),
                  pl.BlockSpec((1, s, LANES), lambda b, c: (b, 0, nh + c)),
                  pl.BlockSpec((None, CONV_W, LANES), lambda b, c: (layer, 0, c)),
                  pl.BlockSpec((None, 1, LANES), lambda b, c: (layer, 0, c)),
                  pl.BlockSpec((None, None, 2, LANES, LANES), lambda b, c: (layer, c, 0, 0, 0)),
                  pl.BlockSpec((None, 2, 1, LANES), vec),
                  pl.BlockSpec((None, None, 2, LANES, LANES), lambda b, c: (layer, c, 0, 0, 0)),
                  pl.BlockSpec((None, 2, 1, LANES), vec),
                  pl.BlockSpec((None, 2, 1, LANES), vec)],
        out_specs=pl.BlockSpec((1, s, LANES), lambda b, c: (b, 0, c)),
        out_shape=jax.ShapeDtypeStruct((bsz, s, LRU_W), F32),
        scratch_shapes=[pltpu.VMEM((s + 3 * SUBLANES, LANES), F32),
                        pltpu.VMEM((s, LANES), F32),
                        pltpu.VMEM((2, scan_rows, LANES), F32),
                        pltpu.VMEM((2, scan_rows, LANES), F32)],
        compiler_params=_cparams(("parallel", "parallel"), 48),
        name="rglru",
    )(r, r, conv_w, conv_b, wa_bd, b_a, wx_bd, b_x, lam)


MLP_FF_CHUNK = 2048


def _mix_mlp_body(x_ref, m_ref, ya_ref, hf_ref, hb_ref, og_ref, yc_ref, wo_ref, hn_ref, w1_ref, w2_ref, fg_ref,
                  o_ref, x1_ref, u_ref, acc_ref, *, tm, ctx_start, final):
    b, i, k = pl.program_id(0), pl.program_id(1), pl.program_id(2)

    @pl.when(k == 0)
    def _():
        hsum = hf_ref[0] + hb_ref[0]
        sq = hsum * hsum
        hi = sq.astype(BF16)
        lo = (sq - hi.astype(F32)).astype(BF16)
        msq = _dot(hi, hn_ref[...]) + _dot(lo, hn_ref[...])
        yb = (_sigmoid(og_ref[0]) * (hsum * lax.rsqrt(msq + EPS))).astype(BF16)
        na, nb = MLA_HEADS * MLA_V, MLA_HEADS * MLA_V + ML_W
        y = (_dot(ya_ref[0], wo_ref[0, 0:na, :]) + _dot(yb, wo_ref[0, na:nb, :])
             + _dot(yc_ref[0].astype(BF16), wo_ref[0, nb:, :]))
        x1 = x_ref[0] + _mod_rows(m_ref, b, i * tm, tm, ctx_start, 2) * y
        x1_ref[...] = x1
        shift = _mod_rows(m_ref, b, i * tm, tm, ctx_start, 3)
        scale = _mod_rows(m_ref, b, i * tm, tm, ctx_start, 4)
        u_ref[...] = (_rms(x1) * (1.0 + scale) + shift).astype(BF16)
        acc_ref[...] = jnp.zeros_like(acc_ref)

    hid = jnp.maximum(_dot(u_ref[...], w1_ref[0]), 0.0)
    acc_ref[...] += _dot((hid * hid).astype(BF16), w2_ref[0])

    @pl.when(k == pl.num_programs(2) - 1)
    def _():
        res = x1_ref[...] + _mod_rows(m_ref, b, i * tm, tm, ctx_start, 5) * acc_ref[...]
        if final:
            res = _rms(res) * fg_ref[...]
        o_ref[0] = res


def _mix_mlp_call(x, mods, ya, hf, hb, mq, yc, w_out, head_avg, w1, w2, final_g, layer, n_ctx, final):
    bsz, s, d = x.shape
    rows_out = s - n_ctx if final else s
    tm = _pick_tile(rows_out, (544, 512, 256))
    fc = MLP_FF_CHUNK
    row = lambda b, i, k: (b, i, 0)
    lsel = lambda b, i, k: (layer, 0, 0)
    return pl.pallas_call(
        functools.partial(_mix_mlp_body, tm=tm, ctx_start=None if final else s - n_ctx, final=final),
        grid=(bsz, rows_out // tm, D_FF // fc),
        in_specs=[pl.BlockSpec((1, tm, d), row),
                  pl.BlockSpec((1, MOD_ROWS, 6 * d), lsel),
                  pl.BlockSpec((1, tm, MLA_HEADS * MLA_V), row),
                  pl.BlockSpec((1, tm, ML_W), row),
                  pl.BlockSpec((1, tm, ML_W), row),
                  pl.BlockSpec((1, tm, ML_W), lambda b, i, k: (b, i, 3)),
                  pl.BlockSpec((1, tm, LRU_W), row),
                  pl.BlockSpec((1, d, d), lsel),
                  pl.BlockSpec((ML_W, ML_W), lambda b, i, k: (0, 0)),
                  pl.BlockSpec((1, d, fc), lambda b, i, k: (layer, 0, k)),
                  pl.BlockSpec((1, fc, d), lambda b, i, k: (layer, k, 0)),
                  pl.BlockSpec((1, d), lambda b, i, k: (0, 0))],
        out_specs=pl.BlockSpec((1, tm, d), row),
        out_shape=jax.ShapeDtypeStruct((bsz, rows_out, d), F32),
        scratch_shapes=[pltpu.VMEM((tm, d), F32), pltpu.VMEM((tm, d), BF16), pltpu.VMEM((tm, d), F32)],
        compiler_params=_cparams(("parallel", "parallel", "arbitrary"), 48),
        name="mix_mlp",
    )(x, mods, ya, hf, hb, mq, yc, w_out, head_avg, w1, w2, final_g)


def _rope_tables(seq, n_ctx):
    half = MLA_ROPE // 2
    row = jnp.repeat(jnp.arange(seq // GRID_W), GRID_W).astype(F32)
    col = jnp.tile(jnp.arange(GRID_W), seq // GRID_W).astype(F32)
    freqs = 1.0 / (ROPE_BASE ** (jnp.arange(0, half, 2, dtype=F32) / half))
    ang = jnp.concatenate([row[:, None] * freqs, col[:, None] * freqs], axis=-1)
    cos = jnp.concatenate([jnp.cos(ang), jnp.ones((n_ctx, half), F32)], axis=0)
    sin = jnp.concatenate([jnp.sin(ang), jnp.zeros((n_ctx, half), F32)], axis=0)
    return cos, sin


def _block_diag_halves(w):
    depth = w.shape[0]
    per = LANES // LRU_BD
    wh = w.reshape(depth, 2, LRU_BLOCKS // per, per, LRU_BD, LRU_BD)
    eye = jnp.eye(per, dtype=w.dtype)
    bd = jnp.einsum("ldcpio,pq->ldcpiqo", wh, eye).reshape(depth, 2, LRU_BLOCKS // per, LANES, LANES)
    return bd.transpose(0, 2, 1, 3, 4).astype(BF16)


def kernel(x, c, ctx, c_ctx, w_mod, b_mod, w_in, mla_g_q, mla_w_uq, mla_g_kv, mla_w_ukv, ml_gate_bias,
           lru_conv_w, lru_conv_b, lru_w_a, lru_b_a, lru_w_x, lru_b_x, lru_lam, w_out, w_ff1, w_ff2, final_g):
    bsz, seq, d = x.shape
    n_ctx = ctx.shape[1]
    depth = w_in.shape[0]
    assert bsz <= CTX_MOD_ROW and d == D_MODEL and n_ctx % 256 == 0

    w_in16 = w_in.astype(BF16)
    zc = lambda n: jnp.zeros((depth, d, n), BF16)
    ml0, ml1 = MLA_IN, MLA_IN + 4 * ML_W
    w_groups = (jnp.concatenate([w_in16[:, :, :ml0], zc(A_W - MLA_IN)], axis=-1),
                w_in16[:, :, ml0:ml1],
                jnp.concatenate([w_in16[:, :, ml1:ml1 + 4 * ML_HEADS], zc(MG_W - 4 * ML_HEADS)], axis=-1),
                w_in16[:, :, ml1 + 4 * ML_HEADS:])
    w_uq_t = mla_w_uq.astype(BF16).transpose(0, 2, 1)
    ukv = mla_w_ukv.reshape(depth, MLA_KV_RANK, MLA_HEADS, MLA_NOPE + MLA_V)
    w_ukv_k = ukv[..., :MLA_NOPE].reshape(depth, MLA_KV_RANK, -1).astype(BF16)
    w_ukv_vt = ukv[..., MLA_NOPE:].reshape(depth, MLA_KV_RANK, -1).astype(BF16).transpose(0, 2, 1)
    g_q = mla_g_q.reshape(depth, 1, MLA_Q_RANK)
    g_kv = mla_g_kv.reshape(depth, 1, MLA_KV_RANK)
    bias_p = jnp.pad(ml_gate_bias, ((0, 0), (0, MG_W - 4 * ML_HEADS)))
    wa_bd, wx_bd = _block_diag_halves(lru_w_a), _block_diag_halves(lru_w_x)
    vec4 = lambda v: v.reshape(depth, 2, 1, LRU_W)
    conv_b = lru_conv_b.reshape(depth, 1, LRU_W)
    w_out16, w1_16, w2_16 = w_out.astype(BF16), w_ff1.astype(BF16), w_ff2.astype(BF16)
    head_avg = jnp.kron(jnp.eye(ML_HEADS, dtype=F32), jnp.full((ML_DH, ML_DH), 1.0 / ML_DH, F32)).astype(BF16)
    cos, sin = _rope_tables(seq, n_ctx)
    rope = (cos, sin, cos.T, sin.T)
    fg = final_g.reshape(1, d)

    cvec = jnp.concatenate([c, jnp.zeros((CTX_MOD_ROW - bsz, d), c.dtype), c_ctx[None, :],
                            jnp.zeros((MOD_ROWS - CTX_MOD_ROW - 1, d), c.dtype)], axis=0)
    mods = _mods_call(cvec, w_mod, b_mod)

    xs = jnp.concatenate([x, ctx], axis=1)
    for l in range(depth):
        last = l == depth - 1
        a, mq, mg, r = _in_call(xs, mods, w_groups, l, n_ctx)
        qt, k, vt = _prep_call(a, g_q, w_uq_t, g_kv, w_ukv_k, w_ukv_vt, rope, l)
        ya = _attn_call(qt, k, vt, n_ctx, with_ctx=not last)
        hf, hb = _mlstm_call(mq, mg, bias_p[l:l + 1], n_ctx)
        yc = _lru_call(r, lru_conv_w, conv_b, wa_bd, vec4(lru_b_a), wx_bd, vec4(lru_b_x), vec4(lru_lam), l, n_ctx)
        xs = _mix_mlp_call(xs, mods, ya, hf, hb, mq, yc, w_out16, head_avg, w1_16, w2_16, fg, l, n_ctx,
                           final=last)
    return xs
```
